```python
import math
import jax, jax.numpy as jnp
from jax import lax
import numpy as np

D_MODEL = 2048
BATCH = 8
SEQ = 2048
DEPTH = 4

HEAD_DIM = 128
N_MIX_HEADS = D_MODEL // HEAD_DIM
N_MIXERS = 4
GROUP_HEADS = N_MIX_HEADS // N_MIXERS
GROUP_W = GROUP_HEADS * HEAD_DIM
MIX_W = N_MIXERS * GROUP_W
BLOCK_Q = 128
ROPE_THETA = 500000.0
FOX_HEADS = GROUP_HEADS
MLA_HEADS = GROUP_HEADS
MLA_Q_RANK = 384
MLA_KV_RANK = 256
MLA_NOPE = 128
MLA_ROPE = 64
MLA_V = 128
SGU_GROUPS = GROUP_HEADS
SGU_CH = HEAD_DIM
SGU_CHUNK = 128
DIFF_HEADS = GROUP_HEADS
DIFF_D = HEAD_DIM // 2
DIFF_ROT = DIFF_D // 4
MEM_LEN = 256
CROSS_HEADS = 4
CROSS_HEAD_DIM = 128
CROSS_W = CROSS_HEADS * CROSS_HEAD_DIM
FFN_HIDDEN = ((8 * D_MODEL // 3 + 255) // 256) * 256
EPS = 1e-6

IN_SIZES = (GROUP_W, GROUP_W, GROUP_W, FOX_HEADS,
            MLA_Q_RANK, MLA_KV_RANK, MLA_ROPE,
            2 * GROUP_W,
            GROUP_W, GROUP_W, GROUP_W)
N_IN = 3 * GROUP_W + FOX_HEADS + MLA_Q_RANK + MLA_KV_RANK + MLA_ROPE + 2 * GROUP_W + 3 * GROUP_W

kernel_name = 'hybrid_parallel_head_group_decoder'

F32 = jnp.float32


def rmsnorm(x, g):
    xf = x.astype(F32)
    y = xf * lax.rsqrt(jnp.mean(xf * xf, axis=-1, keepdims=True) + EPS)
    return (y * g.astype(F32)).astype(x.dtype)


def layernorm(x, g, b):
    xf = x.astype(F32)
    mu = jnp.mean(xf, axis=-1, keepdims=True)
    xc = xf - mu
    y = xc * lax.rsqrt(jnp.mean(xc * xc, axis=-1, keepdims=True) + EPS)
    return (y * g.astype(F32) + b.astype(F32)).astype(x.dtype)


def rope_tables(positions, rot_dim):
    freqs = ROPE_THETA ** (-jnp.arange(0, rot_dim, 2, dtype=F32) / rot_dim)
    ang = positions.astype(F32)[..., None] * freqs
    return jnp.cos(ang), jnp.sin(ang)


def apply_rope(x, cos, sin):
    half = cos.shape[-1]
    r = 2 * half
    c = cos[:, :, None, :].astype(x.dtype)
    s = sin[:, :, None, :].astype(x.dtype)
    x1, x2, xp = x[..., :half], x[..., half:r], x[..., r:]
    return jnp.concatenate([x1 * c - x2 * s, x1 * s + x2 * c, xp], axis=-1)


def _heads(a, h):
    b, s, _ = a.shape
    return a.reshape(b, s, h, -1)


def _bhsd(a):
    return a.transpose(0, 2, 1, 3)


def _merge(o):
    b, h, s, d = o.shape
    return o.transpose(0, 2, 1, 3).reshape(b, s, h * d)


def _to_blocks(a):
    b, h, s = a.shape[:3]
    a = a.reshape((b, h, s // BLOCK_Q, BLOCK_Q) + a.shape[3:])
    return jnp.moveaxis(a, 2, 0)


def _from_blocks(o):
    nb, b, h, blk, d = o.shape
    return jnp.moveaxis(o, 0, 2).reshape(b, h, nb * blk, d)


def _masked_softmax(scores, blk_idx):
    s = scores.shape[-1]
    q_pos = blk_idx * BLOCK_Q + jnp.arange(BLOCK_Q)
    mask = jnp.arange(s)[None, :] <= q_pos[:, None]
    return jax.nn.softmax(jnp.where(mask, scores, -jnp.inf), axis=-1)


def causal_attention(q, k, v, scale, log_decay=None):
    nb = q.shape[2] // BLOCK_Q
    xs = (jnp.arange(nb), _to_blocks(q))
    if log_decay is not None:
        xs = xs + (_to_blocks(log_decay),)

    def body(args):
        i, qb = args[0], args[1]
        sc = jnp.einsum('bhqd,bhkd->bhqk', qb, k).astype(F32) * scale
        if log_decay is not None:
            sc = sc + args[2].astype(F32)[..., None] - log_decay.astype(F32)[:, :, None, :]
        p = _masked_softmax(sc, i)
        return jnp.einsum('bhqk,bhkd->bhqd', p.astype(v.dtype), v)

    return _from_blocks(lax.map(body, xs))


def differential_attention(q1, q2, k1, k2, v, scale, lam):
    nb = q1.shape[2] // BLOCK_Q

    def body(args):
        i, q1b, q2b = args
        p1 = _masked_softmax(jnp.einsum('bhqd,bhkd->bhqk', q1b, k1).astype(F32) * scale, i)
        p2 = _masked_softmax(jnp.einsum('bhqd,bhkd->bhqk', q2b, k2).astype(F32) * scale, i)
        p = p1 - lam * p2
        return jnp.einsum('bhqk,bhkd->bhqd', p.astype(v.dtype), v)

    return _from_blocks(lax.map(body, (jnp.arange(nb), _to_blocks(q1), _to_blocks(q2))))


def fox_mixer(q, k, v, f_logit, b_f):
    log_f = jax.nn.log_sigmoid((f_logit + b_f).astype(F32))
    cum = jnp.cumsum(log_f, axis=1).transpose(0, 2, 1)
    o = causal_attention(_bhsd(_heads(q, FOX_HEADS)), _bhsd(_heads(k, FOX_HEADS)),
                         _bhsd(_heads(v, FOX_HEADS)), HEAD_DIM ** -0.5, cum)
    return _merge(o)


def mla_mixer(c_q, c_kv, k_rope, g_cq, g_ckv, w_uq, w_ukv, cos, sin):
    q = _heads(rmsnorm(c_q, g_cq) @ w_uq, MLA_HEADS)
    kv = _heads(rmsnorm(c_kv, g_ckv) @ w_ukv, MLA_HEADS)
    q = jnp.concatenate([q[..., :MLA_NOPE], apply_rope(q[..., MLA_NOPE:], cos, sin)], axis=-1)
    kr = apply_rope(k_rope[:, :, None, :], cos, sin)
    k = jnp.concatenate([kv[..., :MLA_NOPE],
                         jnp.broadcast_to(kr, kv.shape[:3] + (MLA_ROPE,))], axis=-1)
    v = kv[..., MLA_NOPE:]
    o = causal_attention(_bhsd(q), _bhsd(k), _bhsd(v), (MLA_NOPE + MLA_ROPE) ** -0.5)
    return _merge(o)


def sgu_mixer(uv, ln_g, ln_b, w_s, b_s):
    z = jax.nn.gelu(uv)
    u, v = z[..., :GROUP_W], z[..., GROUP_W:]
    v = layernorm(v, ln_g, ln_b)
    b, s, _ = v.shape
    v = v.reshape(b, s // SGU_CHUNK, SGU_CHUNK, SGU_GROUPS, SGU_CH)
    w = jnp.tril(w_s)
    mixed = jnp.einsum('gts,bnsgc->bntgc', w, v) + b_s.T[None, None, :, :, None]
    return u * mixed.reshape(b, s, GROUP_W)


def diff_mixer(q, k, v, lq1, lk1, lq2, lk2, g_diff, lam_init, cos, sin):
    q = _heads(q, DIFF_HEADS)
    k = _heads(k, DIFF_HEADS)
    v = _heads(v, DIFF_HEADS)
    q1 = apply_rope(q[..., :DIFF_D], cos, sin)
    q2 = apply_rope(q[..., DIFF_D:], cos, sin)
    k1 = apply_rope(k[..., :DIFF_D], cos, sin)
    k2 = apply_rope(k[..., DIFF_D:], cos, sin)
    lam = (jnp.exp(jnp.sum(lq1.astype(F32) * lk1.astype(F32)))
           - jnp.exp(jnp.sum(lq2.astype(F32) * lk2.astype(F32))) + lam_init)
    o = differential_attention(_bhsd(q1), _bhsd(q2), _bhsd(k1), _bhsd(k2), _bhsd(v),
                               DIFF_D ** -0.5, lam)
    o = rmsnorm(o, g_diff) * (1.0 - lam_init)
    return _merge(o)


def memory_cross_attention(xn, mem_n, w_q, w_k, w_v, w_o):
    q = _heads(xn @ w_q, CROSS_HEADS)
    k = _heads(mem_n @ w_k, CROSS_HEADS)
    v = _heads(mem_n @ w_v, CROSS_HEADS)
    sc = jnp.einsum('bshd,bmhd->bhsm', q, k).astype(F32) * CROSS_HEAD_DIM ** -0.5
    p = jax.nn.softmax(sc, axis=-1)
    o = jnp.einsum('bhsm,bmhd->bshd', p.astype(v.dtype), v)
    b, s = o.shape[:2]
    return o.reshape(b, s, CROSS_W) @ w_o


def swiglu(xn, w_gate, w_up, w_down):
    return (jax.nn.silu(xn @ w_gate) * (xn @ w_up)) @ w_down


def _split_points():
    pts, acc = [], 0
    for sz in IN_SIZES[:-1]:
        acc += sz
        pts.append(acc)
    return pts


def setup_inputs(seed: int = 0) -> dict:
    key = jax.random.key(seed)
    ks = jax.random.split(key, 32)

    def nrm(k, shape, scale):
        return jax.random.normal(k, shape, F32) * scale

    def gain(k, shape):
        return 1.0 + 0.02 * jax.random.normal(k, shape, F32)

    L, D = DEPTH, D_MODEL
    return {
        'x': nrm(ks[0], (BATCH, SEQ, D), 1.0),
        'mem': nrm(ks[1], (BATCH, MEM_LEN, D), 1.0),
        'positions': jnp.broadcast_to(jnp.arange(SEQ, dtype=jnp.int32)[None, :], (BATCH, SEQ)),
        'g_mix': gain(ks[2], (L, D)),
        'w_in': nrm(ks[3], (L, D, N_IN), D ** -0.5),
        'b_f': 3.0 + 0.5 * jax.random.normal(ks[4], (L, FOX_HEADS), F32),
        'g_cq': gain(ks[5], (L, MLA_Q_RANK)),
        'g_ckv': gain(ks[6], (L, MLA_KV_RANK)),
        'w_uq': nrm(ks[7], (L, MLA_Q_RANK, MLA_HEADS * (MLA_NOPE + MLA_ROPE)), MLA_Q_RANK ** -0.5),
        'w_ukv': nrm(ks[8], (L, MLA_KV_RANK, MLA_HEADS * (MLA_NOPE + MLA_V)), MLA_KV_RANK ** -0.5),
        'sgu_ln_g': gain(ks[9], (L, GROUP_W)),
        'sgu_ln_b': nrm(ks[10], (L, GROUP_W), 0.02),
        'w_s': nrm(ks[11], (L, SGU_GROUPS, SGU_CHUNK, SGU_CHUNK), SGU_CHUNK ** -0.5),
        'b_s': 1.0 + 0.1 * jax.random.normal(ks[12], (L, SGU_GROUPS, SGU_CHUNK), F32),
        'lam_q1': nrm(ks[13], (L, DIFF_D), 0.1),
        'lam_k1': nrm(ks[14], (L, DIFF_D), 0.1),
        'lam_q2': nrm(ks[15], (L, DIFF_D), 0.1),
        'lam_k2': nrm(ks[16], (L, DIFF_D), 0.1),
        'g_diff': gain(ks[17], (L, 2 * DIFF_D)),
        'w_o': nrm(ks[18], (L, MIX_W, D), MIX_W ** -0.5),
        'g_mem': gain(ks[19], (D,)),
        'g_cross': gain(ks[20], (L, D)),
        'w_cq': nrm(ks[21], (L, D, CROSS_W), D ** -0.5),
        'w_ck': nrm(ks[22], (L, D, CROSS_W), D ** -0.5),
        'w_cv': nrm(ks[23], (L, D, CROSS_W), D ** -0.5),
        'w_co': nrm(ks[24], (L, CROSS_W, D), CROSS_W ** -0.5),
        'g_ffn': gain(ks[25], (L, D)),
        'w_gate': nrm(ks[26], (L, D, FFN_HIDDEN), D ** -0.5),
        'w_up': nrm(ks[27], (L, D, FFN_HIDDEN), D ** -0.5),
        'w_down': nrm(ks[28], (L, FFN_HIDDEN, D), FFN_HIDDEN ** -0.5),
        'g_final': gain(ks[29], (D,)),
    }


def reference(x, mem, positions, g_mix, w_in, b_f, g_cq, g_ckv, w_uq, w_ukv, sgu_ln_g, sgu_ln_b,
              w_s, b_s, lam_q1, lam_k1, lam_q2, lam_k2, g_diff, w_o, g_mem, g_cross, w_cq, w_ck,
              w_cv, w_co, g_ffn, w_gate, w_up, w_down, g_final):
    cos_mla, sin_mla = rope_tables(positions, MLA_ROPE)
    cos_d, sin_d = rope_tables(positions, DIFF_ROT)
    mem_n = rmsnorm(mem, g_mem)
    pts = _split_points()
    for l in range(DEPTH):
        h = rmsnorm(x, g_mix[l])
        (fq, fk, fv, ff, cq, ckv, kr, uv, dq, dk, dv) = jnp.split(h @ w_in[l], pts, axis=-1)
        lam_init = 0.8 - 0.6 * math.exp(-0.3 * l)
        y_a = fox_mixer(fq, fk, fv, ff, b_f[l])
        y_b = mla_mixer(cq, ckv, kr, g_cq[l], g_ckv[l], w_uq[l], w_ukv[l], cos_mla, sin_mla)
        y_c = sgu_mixer(uv, sgu_ln_g[l], sgu_ln_b[l], w_s[l], b_s[l])
        y_d = diff_mixer(dq, dk, dv, lam_q1[l], lam_k1[l], lam_q2[l], lam_k2[l], g_diff[l],
                         lam_init, cos_d, sin_d)
        x = x + jnp.concatenate([y_a, y_b, y_c, y_d], axis=-1) @ w_o[l]
        x = x + memory_cross_attention(rmsnorm(x, g_cross[l]), mem_n, w_cq[l], w_ck[l], w_cv[l], w_co[l])
        x = x + swiglu(rmsnorm(x, g_ffn[l]), w_gate[l], w_up[l], w_down[l])
    return rmsnorm(x, g_final)
```

```python
import functools
import math

import jax
import jax.numpy as jnp
from jax import lax
from jax.experimental import pallas as pl
from jax.experimental.pallas import tpu as pltpu

F32 = jnp.float32
BF16 = jnp.bfloat16

HEAD_DIM = 128
GROUP_HEADS = 4
GROUP_W = GROUP_HEADS * HEAD_DIM
MLA_Q_RANK = 384
MLA_KV_RANK = 256
MLA_NOPE = 128
MLA_ROPE = 64
MLA_V = 128
MLA_QK_PAD = 256
SGU_CHUNK = 128
DIFF_D = HEAD_DIM // 2
DIFF_ROT = DIFF_D // 4
CROSS_HEADS = 4
CROSS_HEAD_DIM = 128
CROSS_W = CROSS_HEADS * CROSS_HEAD_DIM
ROPE_THETA = 500000.0
EPS = 1e-6
LANES = 128

P_FOX = 0
P_DIFF = 3 * GROUP_W
P_SGU = 6 * GROUP_W
P_MLA = 8 * GROUP_W
P_W = 10 * GROUP_W
MLA_BLOCK_W = 2 * GROUP_W

VMEM_LIMIT = 52 * 2**20


def _params(*sem):
    return pltpu.CompilerParams(dimension_semantics=sem, vmem_limit_bytes=VMEM_LIMIT)


def _rms(x, g):
    return x * lax.rsqrt(jnp.mean(x * x, axis=-1, keepdims=True) + EPS) * g


def _dot(a, b):
    return jnp.dot(a, b, preferred_element_type=F32)


def _dot_nt(a, b):
    return lax.dot_general(a, b, (((1,), (1,)), ((), ())), preferred_element_type=F32)


def _in_proj_kernel(x_ref, g_ref, w_ref, wf_ref, p_ref, f_ref, xn_ref):
    @pl.when(pl.program_id(1) == 0)
    def _():
        xb = _rms(x_ref[...], g_ref[...]).astype(BF16)
        xn_ref[...] = xb
        f_ref[...] = _dot(xb, wf_ref[...])

    p_ref[...] = _dot(xn_ref[...], w_ref[...]).astype(p_ref.dtype)


def _in_proj(x, g, w, wf, tm=1024, tn=1024):
    t, d = x.shape
    n = w.shape[1]
    return pl.pallas_call(
        _in_proj_kernel,
        grid=(t // tm, n // tn),
        in_specs=[pl.BlockSpec((tm, d), lambda i, j: (i, 0)),
                  pl.BlockSpec((1, d), lambda i, j: (0, 0)),
                  pl.BlockSpec((d, tn), lambda i, j: (0, j)),
                  pl.BlockSpec((d, LANES), lambda i, j: (0, 0))],
        out_specs=[pl.BlockSpec((tm, tn), lambda i, j: (i, j)),
                   pl.BlockSpec((tm, LANES), lambda i, j: (i, 0))],
        out_shape=[jax.ShapeDtypeStruct((t, n), BF16),
                   jax.ShapeDtypeStruct((t, LANES), F32)],
        scratch_shapes=[pltpu.VMEM((tm, d), BF16)],
        compiler_params=_params("parallel", "arbitrary"),
        name="in_proj",
    )(x, g, w, wf)


def _norm_proj_kernel(x_ref, g_ref, w_ref, o_ref):
    xb = _rms(x_ref[...], g_ref[...]).astype(BF16)
    o_ref[...] = _dot(xb, w_ref[...]).astype(o_ref.dtype)


def _norm_proj(x, g, w, tm=512):
    t, d = x.shape
    n = w.shape[1]
    return pl.pallas_call(
        _norm_proj_kernel,
        grid=(t // tm,),
        in_specs=[pl.BlockSpec((tm, d), lambda i: (i, 0)),
                  pl.BlockSpec((1, d), lambda i: (0, 0)),
                  pl.BlockSpec((d, n), lambda i: (0, 0))],
        out_specs=pl.BlockSpec((tm, n), lambda i: (i, 0)),
        out_shape=jax.ShapeDtypeStruct((t, n), BF16),
        compiler_params=_params("parallel"),
        name="mem_kv_proj",
    )(x, g, w)


def _fox_gate_kernel(f_ref, b_ref, col_ref, row_ref):
    x = f_ref[0] + b_ref[...]
    c = jnp.minimum(x, 0.0) - jnp.log1p(jnp.exp(-jnp.abs(x)))
    s = x.shape[0]
    ridx = lax.broadcasted_iota(jnp.int32, x.shape, 0)
    shift = 1
    while shift < s:
        c = c + jnp.where(ridx >= shift, pltpu.roll(c, shift, 0), 0.0)
        shift *= 2
    col_ref[0] = c
    row_ref[0] = c.T[0:8, :]


def _fox_gate(fb, b_f):
    b, s, _ = fb.shape
    return pl.pallas_call(
        _fox_gate_kernel,
        grid=(b,),
        in_specs=[pl.BlockSpec((1, s, LANES), lambda i: (i, 0, 0)),
                  pl.BlockSpec((1, LANES), lambda i: (0, 0))],
        out_specs=[pl.BlockSpec((1, s, LANES), lambda i: (i, 0, 0)),
                   pl.BlockSpec((1, 8, s), lambda i: (i, 0, 0))],
        out_shape=[jax.ShapeDtypeStruct((b, s, LANES), F32),
                   jax.ShapeDtypeStruct((b, 8, s), F32)],
        compiler_params=_params("parallel"),
        name="fox_gate",
    )(fb, b_f)


def _flash_head(q, k_ref, v_ref, h, dk, dv, qi, tq, scale, fq=None, frow_ref=None):
    r = q.shape[0]

    def scores(j):
        start = pl.multiple_of(j * tq, tq)
        k = k_ref[0, pl.ds(start, tq), h * dk:(h + 1) * dk]
        s = _dot_nt(q, k) * scale
        if fq is not None:
            s = s + fq - frow_ref[0, h:h + 1, pl.ds(start, tq)]
        return s, start

    def update(s, start, carry):
        m, l, acc = carry
        m_new = jnp.maximum(m, jnp.max(s, axis=-1, keepdims=True))
        alpha = jnp.exp(m - m_new)
        p = jnp.exp(s - m_new)
        l = alpha * l + jnp.sum(p, axis=-1, keepdims=True)
        v = v_ref[0, pl.ds(start, tq), h * dv:(h + 1) * dv]
        acc = alpha * acc + _dot(p.astype(BF16), v)
        return m_new, l, acc

    def body(j, carry):
        s, start = scores(j)
        return update(s, start, carry)

    init = (jnp.full((r, 1), -jnp.inf, F32), jnp.zeros((r, 1), F32), jnp.zeros((r, dv), F32))
    carry = lax.fori_loop(0, qi, body, init)
    s, start = scores(qi)
    row = lax.broadcasted_iota(jnp.int32, (r, tq), 0) & (tq - 1)
    col = lax.broadcasted_iota(jnp.int32, (r, tq), 1)
    s = jnp.where(col <= row, s, -jnp.inf)
    _, l, acc = update(s, start, carry)
    return acc, l


def _fox_attn_kernel(q_ref, k_ref, v_ref, fcol_ref, frow_ref, o_ref, *, tq, scale):
    qi = pl.program_id(1)
    for h in range(GROUP_HEADS):
        q = q_ref[0, :, h * HEAD_DIM:(h + 1) * HEAD_DIM]
        fq = fcol_ref[0, :, h:h + 1]
        acc, l = _flash_head(q, k_ref, v_ref, h, HEAD_DIM, HEAD_DIM, qi, tq, scale, fq, frow_ref)
        o_ref[0, :, h * HEAD_DIM:(h + 1) * HEAD_DIM] = (acc / l).astype(o_ref.dtype)


def _fox_attn(p3, fcol, frow, tq=256):
    b, s, _ = p3.shape
    w = GROUP_W
    c0 = P_FOX // w
    return pl.pallas_call(
        functools.partial(_fox_attn_kernel, tq=tq, scale=HEAD_DIM ** -0.5),
        grid=(b, s // tq),
        in_specs=[pl.BlockSpec((1, tq, w), lambda i, j: (i, j, c0)),
                  pl.BlockSpec((1, s, w), lambda i, j: (i, 0, c0 + 1)),
                  pl.BlockSpec((1, s, w), lambda i, j: (i, 0, c0 + 2)),
                  pl.BlockSpec((1, tq, LANES), lambda i, j: (i, j, 0)),
                  pl.BlockSpec((1, 8, s), lambda i, j: (i, 0, 0))],
        out_specs=pl.BlockSpec((1, tq, w), lambda i, j: (i, j, 0)),
        out_shape=jax.ShapeDtypeStruct((b, s, w), BF16),
        compiler_params=_params("parallel", "arbitrary"),
        name="fox_attn",
    )(p3, p3, p3, fcol, frow)


def _rope_pairs(x, cos, sin, half, period):
    lane = lax.broadcasted_iota(jnp.int32, x.shape, x.ndim - 1)
    first = (lane & (period - 1)) < half
    rot = jnp.where(first, -pltpu.roll(x, LANES - half, x.ndim - 1), pltpu.roll(x, half, x.ndim - 1))
    return x * cos + rot * sin


def _mla_prep_kernel(p_ref, gq_ref, gkv_ref, wq_ref, wkv_ref, cos_ref, sin_ref,
                     q_ref, k_ref, v_ref):
    blk = p_ref[0].astype(F32)
    cos = cos_ref[0]
    sin = sin_ref[0]
    cq = blk[:, 0:MLA_Q_RANK]
    ckv = blk[:, MLA_Q_RANK:MLA_Q_RANK + MLA_KV_RANK]
    kr = blk[:, MLA_Q_RANK + MLA_KV_RANK:MLA_Q_RANK + MLA_KV_RANK + LANES]
    q = _dot(_rms(cq, gq_ref[...]).astype(BF16), wq_ref[...])
    kv = _dot(_rms(ckv, gkv_ref[...]).astype(BF16), wkv_ref[...])
    kr = _rope_pairs(kr, cos, sin, MLA_ROPE // 2, LANES).astype(BF16)
    for h in range(GROUP_HEADS):
        o = h * MLA_QK_PAD
        q_ref[0, :, o:o + LANES] = q[:, o:o + LANES].astype(BF16)
        q_ref[0, :, o + LANES:o + 2 * LANES] = _rope_pairs(
            q[:, o + LANES:o + 2 * LANES], cos, sin, MLA_ROPE // 2, LANES).astype(BF16)
        k_ref[0, :, o:o + LANES] = kv[:, o:o + LANES].astype(BF16)
        k_ref[0, :, o + LANES:o + 2 * LANES] = kr
        v_ref[0, :, h * MLA_V:(h + 1) * MLA_V] = kv[:, o + LANES:o + 2 * LANES].astype(BF16)


def _mla_prep(p3, g_cq, g_ckv, w_uq, w_ukv, cos, sin, tm=512):
    b, s, _ = p3.shape
    c0 = P_MLA // MLA_BLOCK_W
    hq = GROUP_HEADS * MLA_QK_PAD
    return pl.pallas_call(
        _mla_prep_kernel,
        grid=(b, s // tm),
        in_specs=[pl.BlockSpec((1, tm, MLA_BLOCK_W), lambda i, j: (i, j, c0)),
                  pl.BlockSpec((1, MLA_Q_RANK), lambda i, j: (0, 0)),
                  pl.BlockSpec((1, MLA_KV_RANK), lambda i, j: (0, 0)),
                  pl.BlockSpec((MLA_Q_RANK, hq), lambda i, j: (0, 0)),
                  pl.BlockSpec((MLA_KV_RANK, hq), lambda i, j: (0, 0)),
                  pl.BlockSpec((1, tm, LANES), lambda i, j: (i, j, 0)),
                  pl.BlockSpec((1, tm, LANES), lambda i, j: (i, j, 0))],
        out_specs=[pl.BlockSpec((1, tm, hq), lambda i, j: (i, j, 0)),
                   pl.BlockSpec((1, tm, hq), lambda i, j: (i, j, 0)),
                   pl.BlockSpec((1, tm, GROUP_W), lambda i, j: (i, j, 0))],
        out_shape=[jax.ShapeDtypeStruct((b, s, hq), BF16),
                   jax.ShapeDtypeStruct((b, s, hq), BF16),
                   jax.ShapeDtypeStruct((b, s, GROUP_W), BF16)],
        compiler_params=_params("parallel", "parallel"),
        name="mla_prep",
    )(p3, g_cq, g_ckv, w_uq, w_ukv, cos, sin)


def _mla_attn_kernel(q_ref, k_ref, v_ref, o_ref, *, tq, scale):
    qi = pl.program_id(1)
    for h in range(GROUP_HEADS):
        q = q_ref[0, :, h * MLA_QK_PAD:(h + 1) * MLA_QK_PAD]
        acc, l = _flash_head(q, k_ref, v_ref, h, MLA_QK_PAD, MLA_V, qi, tq, scale)
        o_ref[0, :, h * MLA_V:(h + 1) * MLA_V] = (acc / l).astype(o_ref.dtype)


def _mla_attn(qm, km, vm, tq=256):
    b, s, hq = qm.shape
    return pl.pallas_call(
        functools.partial(_mla_attn_kernel, tq=tq, scale=(MLA_NOPE + MLA_ROPE) ** -0.5),
        grid=(b, s // tq),
        in_specs=[pl.BlockSpec((1, tq, hq), lambda i, j: (i, j, 0)),
                  pl.BlockSpec((1, s, hq), lambda i, j: (i, 0, 0)),
                  pl.BlockSpec((1, s, GROUP_W), lambda i, j: (i, 0, 0))],
        out_specs=pl.BlockSpec((1, tq, GROUP_W), lambda i, j: (i, j, 0)),
        out_shape=jax.ShapeDtypeStruct((b, s, GROUP_W), BF16),
        compiler_params=_params("parallel", "arbitrary"),
        name="mla_attn",
    )(qm, km, vm)


def _diff_prep_kernel(q_ref, k_ref, cos_ref, sin_ref, qo_ref, ko_ref):
    cos = cos_ref[0]
    sin = sin_ref[0]
    for src, dst in ((q_ref, qo_ref), (k_ref, ko_ref)):
        for c in range(GROUP_W // LANES):
            x = src[0, :, c * LANES:(c + 1) * LANES].astype(F32)
            dst[0, :, c * LANES:(c + 1) * LANES] = _rope_pairs(
                x, cos, sin, DIFF_ROT // 2, DIFF_D).astype(BF16)


def _diff_prep(p3, cos, sin, tm=1024):
    b, s, _ = p3.shape
    c0 = P_DIFF // GROUP_W
    return pl.pallas_call(
        _diff_prep_kernel,
        grid=(b, s // tm),
        in_specs=[pl.BlockSpec((1, tm, GROUP_W), lambda i, j: (i, j, c0)),
                  pl.BlockSpec((1, tm, GROUP_W), lambda i, j: (i, j, c0 + 1)),
                  pl.BlockSpec((1, tm, LANES), lambda i, j: (i, j, 0)),
                  pl.BlockSpec((1, tm, LANES), lambda i, j: (i, j, 0))],
        out_specs=[pl.BlockSpec((1, tm, GROUP_W), lambda i, j: (i, j, 0)),
                   pl.BlockSpec((1, tm, GROUP_W), lambda i, j: (i, j, 0))],
        out_shape=[jax.ShapeDtypeStruct((b, s, GROUP_W), BF16),
                   jax.ShapeDtypeStruct((b, s, GROUP_W), BF16)],
        compiler_params=_params("parallel", "parallel"),
        name="diff_prep",
    )(p3, p3, cos, sin)


def _diff_attn_kernel(q_ref, k_ref, v_ref, lam_ref, g_ref, o_ref, *, tq, scale, lam_init):
    qi = pl.program_id(1)
    lv = lam_ref[...]
    lam = (jnp.exp(jnp.sum(lv[0:1] * lv[1:2], axis=-1, keepdims=True))
           - jnp.exp(jnp.sum(lv[2:3] * lv[3:4], axis=-1, keepdims=True)) + lam_init)
    lane = lax.broadcasted_iota(jnp.int32, (tq, HEAD_DIM), 1)
    for h in range(GROUP_HEADS):
        q = q_ref[0, :, h * HEAD_DIM:(h + 1) * HEAD_DIM]
        zero = jnp.zeros_like(q)
        qs = jnp.concatenate([jnp.where(lane < DIFF_D, q, zero),
                              jnp.where(lane >= DIFF_D, q, zero)], axis=0)
        acc, l = _flash_head(qs, k_ref, v_ref, h, HEAD_DIM, HEAD_DIM, qi, tq, scale)
        o = acc / l
        o = o[0:tq] - lam * o[tq:2 * tq]
        o = _rms(o, g_ref[...]) * (1.0 - lam_init)
        o_ref[0, :, h * HEAD_DIM:(h + 1) * HEAD_DIM] = o.astype(o_ref.dtype)


def _diff_attn(qr, kr, p3, lamv, g_diff, lam_init, tq=256):
    b, s, w = qr.shape
    cv = P_DIFF // w + 2
    return pl.pallas_call(
        functools.partial(_diff_attn_kernel, tq=tq, scale=DIFF_D ** -0.5, lam_init=lam_init),
        grid=(b, s // tq),
        in_specs=[pl.BlockSpec((1, tq, w), lambda i, j: (i, j, 0)),
                  pl.BlockSpec((1, s, w), lambda i, j: (i, 0, 0)),
                  pl.BlockSpec((1, s, w), lambda i, j: (i, 0, cv)),
                  pl.BlockSpec((4, DIFF_D), lambda i, j: (0, 0)),
                  pl.BlockSpec((1, HEAD_DIM), lambda i, j: (0, 0))],
        out_specs=pl.BlockSpec((1, tq, w), lambda i, j: (i, j, 0)),
        out_shape=jax.ShapeDtypeStruct((b, s, w), BF16),
        compiler_params=_params("parallel", "arbitrary"),
        name="diff_attn",
    )(qr, kr, p3, lamv, g_diff)


def _sgu_kernel(uv_ref, lg_ref, lb_ref, ws_ref, bs_ref, o_ref, *, tm):
    z = uv_ref[...].astype(F32)
    z = 0.5 * z * (1.0 + jnp.tanh(math.sqrt(2.0 / math.pi) * (z + 0.044715 * (z * z * z))))
    u = z[:, 0:GROUP_W]
    v = z[:, GROUP_W:2 * GROUP_W]
    xc = v - jnp.mean(v, axis=-1, keepdims=True)
    vn = xc * lax.rsqrt(jnp.mean(xc * xc, axis=-1, keepdims=True) + EPS) * lg_ref[...] + lb_ref[...]
    vn = vn.astype(BF16)
    t = SGU_CHUNK
    tril = (lax.broadcasted_iota(jnp.int32, (t, t), 1) <= lax.broadcasted_iota(jnp.int32, (t, t), 0))
    for g in range(GROUP_HEADS):
        w = jnp.where(tril, ws_ref[g], 0.0).astype(BF16)
        bias = bs_ref[:, g:g + 1]
        for c in range(tm // t):
            rows = slice(c * t, (c + 1) * t)
            cols = slice(g * HEAD_DIM, (g + 1) * HEAD_DIM)
            mixed = _dot(w, vn[rows, cols]) + bias
            o_ref[rows, cols] = (u[rows, cols] * mixed).astype(o_ref.dtype)


def _sgu(p, ln_g, ln_b, w_s, b_st, tm=512):
    t = p.shape[0]
    c0 = P_SGU // (2 * GROUP_W)
    return pl.pallas_call(
        functools.partial(_sgu_kernel, tm=tm),
        grid=(t // tm,),
        in_specs=[pl.BlockSpec((tm, 2 * GROUP_W), lambda i: (i, c0)),
                  pl.BlockSpec((1, GROUP_W), lambda i: (0, 0)),
                  pl.BlockSpec((1, GROUP_W), lambda i: (0, 0)),
                  pl.BlockSpec((GROUP_HEADS, SGU_CHUNK, SGU_CHUNK), lambda i: (0, 0, 0)),
                  pl.BlockSpec((SGU_CHUNK, GROUP_HEADS), lambda i: (0, 0))],
        out_specs=pl.BlockSpec((tm, GROUP_W), lambda i: (i, 0)),
        out_shape=jax.ShapeDtypeStruct((t, GROUP_W), BF16),
        compiler_params=_params("parallel"),
        name="sgu",
    )(p, ln_g, ln_b, w_s, b_st)


def _out_proj_kernel(x_ref, ya_ref, yb_ref, yc_ref, yd_ref, w_ref, o_ref):
    acc = x_ref[...]
    for m, y_ref in enumerate((ya_ref, yb_ref, yc_ref, yd_ref)):
        acc = acc + _dot(y_ref[...], w_ref[m * GROUP_W:(m + 1) * GROUP_W, :])
    o_ref[...] = acc


def _out_proj(x, ys, w, tm=1024, tn=1024):
    t, d = x.shape
    yspec = pl.BlockSpec((tm, GROUP_W), lambda i, j: (i, 0))
    return pl.pallas_call(
        _out_proj_kernel,
        grid=(t // tm, d // tn),
        in_specs=[pl.BlockSpec((tm, tn), lambda i, j: (i, j)), yspec, yspec, yspec, yspec,
                  pl.BlockSpec((w.shape[0], tn), lambda i, j: (0, j))],
        out_specs=pl.BlockSpec((tm, tn), lambda i, j: (i, j)),
        out_shape=jax.ShapeDtypeStruct((t, d), F32),
        compiler_params=_params("parallel", "parallel"),
        name="out_proj",
    )(x, *ys, w)


def _cross_kernel(x_ref, g_ref, wq_ref, kv_ref, wo_ref, o_ref, oh_ref, *, scale):
    x = x_ref[...]
    q = _dot(_rms(x, g_ref[...]).astype(BF16), wq_ref[...]).astype(BF16)
    for h in range(CROSS_HEADS):
        cols = slice(h * CROSS_HEAD_DIM, (h + 1) * CROSS_HEAD_DIM)
        k = kv_ref[0, :, cols]
        v = kv_ref[0, :, CROSS_W + h * CROSS_HEAD_DIM:CROSS_W + (h + 1) * CROSS_HEAD_DIM]
        s = _dot_nt(q[:, cols], k) * scale
        p = jnp.exp(s - jnp.max(s, axis=-1, keepdims=True))
        l = jnp.sum(p, axis=-1, keepdims=True)
        oh_ref[:, cols] = (_dot(p.astype(BF16), v) / l).astype(BF16)
    o_ref[...] = x + _dot(oh_ref[...], wo_ref[...])


def _cross(x, g, wq, kv, wo, seq, tm=512):
    t, d = x.shape
    per_b = seq // tm
    m = kv.shape[1]
    return pl.pallas_call(
        functools.partial(_cross_kernel, scale=CROSS_HEAD_DIM ** -0.5),
        grid=(t // tm,),
        in_specs=[pl.BlockSpec((tm, d), lambda i: (i, 0)),
                  pl.BlockSpec((1, d), lambda i: (0, 0)),
                  pl.BlockSpec((d, CROSS_W), lambda i: (0, 0)),
                  pl.BlockSpec((1, m, 2 * CROSS_W), lambda i: (i // per_b, 0, 0)),
                  pl.BlockSpec((CROSS_W, d), lambda i: (0, 0))],
        out_specs=pl.BlockSpec((tm, d), lambda i: (i, 0)),
        out_shape=jax.ShapeDtypeStruct((t, d), F32),
        scratch_shapes=[pltpu.VMEM((tm, CROSS_W), BF16)],
        compiler_params=_params("parallel"),
        name="cross_attn",
    )(x, g, wq, kv, wo)


def _ffn_kernel(x_ref, g_ref, wg_ref, wu_ref, wd_ref, gf_ref, o_ref, xn_ref, acc_ref, *, final):
    j = pl.program_id(1)

    @pl.when(j == 0)
    def _():
        xn_ref[...] = _rms(x_ref[...], g_ref[...]).astype(BF16)
        acc_ref[...] = jnp.zeros_like(acc_ref)

    xn = xn_ref[...]
    gate = _dot(xn, wg_ref[...])
    up = _dot(xn, wu_ref[...])
    hid = (gate * jax.nn.sigmoid(gate) * up).astype(BF16)
    acc_ref[...] += _dot(hid, wd_ref[...])

    @pl.when(j == pl.num_programs(1) - 1)
    def _():
        y = x_ref[...] + acc_ref[...]
        o_ref[...] = _rms(y, gf_ref[...]) if final else y


def _ffn(x, g, wg, wu, wd, g_final, final, tm=512, th=512):
    t, d = x.shape
    hdim = wg.shape[1]
    return pl.pallas_call(
        functools.partial(_ffn_kernel, final=final),
        grid=(t // tm, hdim // th),
        in_specs=[pl.BlockSpec((tm, d), lambda i, j: (i, 0)),
                  pl.BlockSpec((1, d), lambda i, j: (0, 0)),
                  pl.BlockSpec((d, th), lambda i, j: (0, j)),
                  pl.BlockSpec((d, th), lambda i, j: (0, j)),
                  pl.BlockSpec((th, d), lambda i, j: (j, 0)),
                  pl.BlockSpec((1, d), lambda i, j: (0, 0))],
        out_specs=pl.BlockSpec((tm, d), lambda i, j: (i, 0)),
        out_shape=jax.ShapeDtypeStruct((t, d), F32),
        scratch_shapes=[pltpu.VMEM((tm, d), BF16), pltpu.VMEM((tm, d), F32)],
        compiler_params=_params("parallel", "arbitrary"),
        name="swiglu",
    )(x, g, wg, wu, wd, g_final)


def _rope_tables(positions):
    pos = positions.astype(F32)[..., None]
    b, s = positions.shape

    def table(rot_dim, period):
        freqs = ROPE_THETA ** (-jnp.arange(0, rot_dim, 2, dtype=F32) / rot_dim)
        ang = pos * freqs
        ones = jnp.ones((b, s, period - rot_dim), F32)
        cos = jnp.concatenate([jnp.cos(ang), jnp.cos(ang), ones], axis=-1)
        sin = jnp.concatenate([jnp.sin(ang), jnp.sin(ang), 0.0 * ones], axis=-1)
        reps = LANES // period
        return jnp.tile(cos, (1, 1, reps)), jnp.tile(sin, (1, 1, reps))

    return table(MLA_ROPE, LANES), table(DIFF_ROT, DIFF_D)


def _split_w_in(w):
    sizes = (GROUP_W, GROUP_W, GROUP_W, GROUP_HEADS, MLA_Q_RANK, MLA_KV_RANK, MLA_ROPE,
             2 * GROUP_W, GROUP_W, GROUP_W, GROUP_W)
    offs = [0]
    for sz in sizes:
        offs.append(offs[-1] + sz)
    fq, fk, fv, ff, cq, ckv, kr, uv, dq, dk, dv = (w[:, offs[i]:offs[i + 1]] for i in range(11))
    d = w.shape[0]
    pad = jnp.zeros((d, P_W - P_MLA - MLA_Q_RANK - MLA_KV_RANK - MLA_ROPE), w.dtype)
    main = jnp.concatenate([fq, fk, fv, dq, dk, dv, uv, cq, ckv, kr, pad], axis=1).astype(BF16)
    wf = jnp.concatenate([ff, jnp.zeros((d, LANES - GROUP_HEADS), w.dtype)], axis=1).astype(BF16)
    return main, wf


def _pad_heads(w, used, width):
    k = w.shape[0]
    w = w.reshape(k, GROUP_HEADS, used)
    w = jnp.pad(w, ((0, 0), (0, 0), (0, width - used)))
    return w.reshape(k, GROUP_HEADS * width)


def kernel(x, mem, positions, g_mix, w_in, b_f, g_cq, g_ckv, w_uq, w_ukv, sgu_ln_g, sgu_ln_b, w_s,
           b_s, lam_q1, lam_k1, lam_q2, lam_k2, g_diff, w_o, g_mem, g_cross, w_cq, w_ck, w_cv, w_co,
           g_ffn, w_gate, w_up, w_down, g_final):
    b, s, d = x.shape
    depth = w_in.shape[0]
    t = b * s
    (cos_m, sin_m), (cos_d, sin_d) = _rope_tables(positions)
    xf = x.reshape(t, d)
    memf = mem.reshape(b * mem.shape[1], d)
    row = lambda a: a.reshape(1, -1)

    for l in range(depth):
        lam_init = 0.8 - 0.6 * math.exp(-0.3 * l)
        w_main, w_f = _split_w_in(w_in[l])
        p, fb = _in_proj(xf, row(g_mix[l]), w_main, w_f)
        p3 = p.reshape(b, s, P_W)

        bf_pad = jnp.pad(b_f[l], (0, LANES - GROUP_HEADS)).reshape(1, LANES)
        fcol, frow = _fox_gate(fb.reshape(b, s, LANES), bf_pad)
        y_a = _fox_attn(p3, fcol, frow)

        qm, km, vm = _mla_prep(p3, row(g_cq[l]), row(g_ckv[l]),
                               _pad_heads(w_uq[l], MLA_NOPE + MLA_ROPE, MLA_QK_PAD).astype(BF16),
                               w_ukv[l].astype(BF16), cos_m, sin_m)
        y_b = _mla_attn(qm, km, vm)

        y_c = _sgu(p, row(sgu_ln_g[l]), row(sgu_ln_b[l]), w_s[l], b_s[l].T)

        qr, kr = _diff_prep(p3, cos_d, sin_d)
        lamv = jnp.stack([lam_q1[l], lam_k1[l], lam_q2[l], lam_k2[l]])
        y_d = _diff_attn(qr, kr, p3, lamv, row(g_diff[l]), lam_init)

        ys = [y.reshape(t, GROUP_W) for y in (y_a, y_b, y_c, y_d)]
        xf = _out_proj(xf, ys, w_o[l].astype(BF16))

        w_kv = jnp.concatenate([w_ck[l], w_cv[l]], axis=1).astype(BF16)
        kv = _norm_proj(memf, row(g_mem), w_kv).reshape(b, mem.shape[1], 2 * CROSS_W)
        xf = _cross(xf, row(g_cross[l]), w_cq[l].astype(BF16), kv, w_co[l].astype(BF16), s)

        xf = _ffn(xf, row(g_ffn[l]), w_gate[l].astype(BF16), w_up[l].astype(BF16),
                  w_down[l].astype(BF16), row(g_final), final=(l == depth - 1))
    return xf.reshape(b, s, d)
```

```python
import functools
import math

import jax
import jax.numpy as jnp
from jax import lax
from jax.experimental import pallas as pl
from jax.experimental.pallas import tpu as pltpu

F32 = jnp.float32
BF16 = jnp.bfloat16

HEAD_DIM = 128
GROUP_HEADS = 4
GROUP_W = GROUP_HEADS * HEAD_DIM
MLA_Q_RANK = 384
MLA_KV_RANK = 256
MLA_NOPE = 128
MLA_ROPE = 64
MLA_V = 128
MLA_QK_PAD = 256
SGU_CHUNK = 128
DIFF_D = HEAD_DIM // 2
DIFF_ROT = DIFF_D // 4
CROSS_HEADS = 4
CROSS_HEAD_DIM = 128
CROSS_W = CROSS_HEADS * CROSS_HEAD_DIM
ROPE_THETA = 500000.0
EPS = 1e-6
LOG2E = math.log2(math.e)
LANES = 128

P_FOX = 0
P_DIFF = 3 * GROUP_W
P_SGU = 6 * GROUP_W
P_MLA = 8 * GROUP_W
P_W = 10 * GROUP_W
MLA_BLOCK_W = 2 * GROUP_W

VMEM_LIMIT = 52 * 2**20


def _params(*sem):
    return pltpu.CompilerParams(dimension_semantics=sem, vmem_limit_bytes=VMEM_LIMIT)


def _rms(x, g):
    return x * lax.rsqrt(jnp.mean(x * x, axis=-1, keepdims=True) + EPS) * g


def _dot(a, b):
    return jnp.dot(a, b, preferred_element_type=F32)


def _dot_nt(a, b):
    return lax.dot_general(a, b, (((1,), (1,)), ((), ())), preferred_element_type=F32)


def _cast_kernel(w_ref, o_ref):
    o_ref[...] = w_ref[...].astype(o_ref.dtype)


def _cast_bf16(w, block_bytes=4 * 2**20):
    nl, k, n = w.shape
    tk = k
    while tk * n * 4 > block_bytes and tk % 16 == 0:
        tk //= 2
    return pl.pallas_call(
        _cast_kernel,
        grid=(nl, k // tk),
        in_specs=[pl.BlockSpec((1, tk, n), lambda l, i: (l, i, 0))],
        out_specs=pl.BlockSpec((1, tk, n), lambda l, i: (l, i, 0)),
        out_shape=jax.ShapeDtypeStruct(w.shape, BF16),
        compiler_params=_params("parallel", "parallel"),
        name="cast_bf16",
    )(w)


def _in_proj_kernel(x_ref, g_ref, w_ref, wf_ref, cs_ref, p_ref, f_ref, xn_ref):
    @pl.when(pl.program_id(1) == 0)
    def _():
        xb = _rms(x_ref[...], g_ref[...]).astype(BF16)
        xn_ref[...] = xb
        f_ref[...] = _dot(xb, wf_ref[...])

    p_ref[...] = (_dot(xn_ref[...], w_ref[...]) * cs_ref[...]).astype(p_ref.dtype)


def _in_proj(x, g, w, wf, col_scale, l, tm=1024, tn=1024):
    t, d = x.shape
    n = w.shape[2]
    return pl.pallas_call(
        _in_proj_kernel,
        grid=(t // tm, n // tn),
        in_specs=[pl.BlockSpec((tm, d), lambda i, j: (i, 0)),
                  pl.BlockSpec((1, d), lambda i, j: (0, 0)),
                  pl.BlockSpec((None, d, tn), lambda i, j: (l, 0, j)),
                  pl.BlockSpec((None, d, LANES), lambda i, j: (l, 0, 0)),
                  pl.BlockSpec((1, tn), lambda i, j: (0, j))],
        out_specs=[pl.BlockSpec((tm, tn), lambda i, j: (i, j)),
                   pl.BlockSpec((tm, LANES), lambda i, j: (i, 0))],
        out_shape=[jax.ShapeDtypeStruct((t, n), BF16),
                   jax.ShapeDtypeStruct((t, LANES), F32)],
        scratch_shapes=[pltpu.VMEM((tm, d), BF16)],
        compiler_params=_params("parallel", "arbitrary"),
        name="in_proj",
    )(x, g, w, wf, col_scale)


def _mem_kv_kernel(x_ref, g_ref, wk_ref, wv_ref, o_ref):
    xb = _rms(x_ref[...], g_ref[...]).astype(BF16)
    o_ref[:, 0:CROSS_W] = _dot(xb, wk_ref[...]).astype(o_ref.dtype)
    o_ref[:, CROSS_W:2 * CROSS_W] = _dot(xb, wv_ref[...]).astype(o_ref.dtype)


def _mem_kv(x, g, wk, wv, l, tm=512):
    t, d = x.shape
    wspec = pl.BlockSpec((None, d, CROSS_W), lambda i: (l, 0, 0))
    return pl.pallas_call(
        _mem_kv_kernel,
        grid=(t // tm,),
        in_specs=[pl.BlockSpec((tm, d), lambda i: (i, 0)),
                  pl.BlockSpec((1, d), lambda i: (0, 0)), wspec, wspec],
        out_specs=pl.BlockSpec((tm, 2 * CROSS_W), lambda i: (i, 0)),
        out_shape=jax.ShapeDtypeStruct((t, 2 * CROSS_W), BF16),
        compiler_params=_params("parallel"),
        name="mem_kv_proj",
    )(x, g, wk, wv)


def _fox_gate_kernel(f_ref, b_ref, col_ref, row_ref):
    x = f_ref[0] + b_ref[...]
    c = jnp.minimum(x, 0.0) - jnp.log1p(jnp.exp(-jnp.abs(x)))
    s = x.shape[0]
    ridx = lax.broadcasted_iota(jnp.int32, x.shape, 0)
    shift = 1
    while shift < s:
        c = c + jnp.where(ridx >= shift, pltpu.roll(c, shift, 0), 0.0)
        shift *= 2
    c = c * LOG2E
    col_ref[0] = c
    row_ref[0] = c.T[0:8, :]


def _fox_gate(fb, b_f):
    b, s, _ = fb.shape
    return pl.pallas_call(
        _fox_gate_kernel,
        grid=(b,),
        in_specs=[pl.BlockSpec((1, s, LANES), lambda i: (i, 0, 0)),
                  pl.BlockSpec((1, LANES), lambda i: (0, 0))],
        out_specs=[pl.BlockSpec((1, s, LANES), lambda i: (i, 0, 0)),
                   pl.BlockSpec((1, 8, s), lambda i: (i, 0, 0))],
        out_shape=[jax.ShapeDtypeStruct((b, s, LANES), F32),
                   jax.ShapeDtypeStruct((b, 8, s), F32)],
        compiler_params=_params("parallel"),
        name="fox_gate",
    )(fb, b_f)


def _flash(qs, k_ref, v_ref, dk, dv, qi, tq, fqs=None, frow_ref=None):
    nh = len(qs)
    r = qs[0].shape[0]
    assert dv == LANES and tq % LANES == 0
    groups = tq // LANES
    row = lax.broadcasted_iota(jnp.int32, (r, tq), 0) & (tq - 1)
    col = lax.broadcasted_iota(jnp.int32, (r, tq), 1)
    causal = col <= row
    if fqs is not None:
        fqs = [jnp.broadcast_to(f, (r, LANES)) for f in fqs]

    def block(j, carry, diag):
        start = pl.multiple_of(j * tq, tq)
        out = []
        for h in range(nh):
            m, lp, acc = carry[h]
            k = k_ref[0, pl.ds(start, tq), h * dk:(h + 1) * dk]
            s = _dot_nt(qs[h], k)
            if frow_ref is not None:
                s = s - frow_ref[0, h:h + 1, pl.ds(start, tq)]
            if diag:
                s = jnp.where(causal, s, -jnp.inf)
            mz = jnp.broadcast_to(jnp.max(s, axis=-1, keepdims=True), (r, LANES))
            if fqs is not None:
                mz = mz + fqs[h]
            m_new = jnp.maximum(m, mz)
            alpha = jnp.exp2(m - m_new)
            shift = (fqs[h] - m_new) if fqs is not None else -m_new
            p = jnp.exp2(s + jnp.concatenate([shift] * groups, axis=1))
            psum = p[:, 0:LANES]
            for g in range(1, groups):
                psum = psum + p[:, g * LANES:(g + 1) * LANES]
            lp = alpha * lp + psum
            v = v_ref[0, pl.ds(start, tq), h * dv:(h + 1) * dv]
            acc = alpha * acc + _dot(p.astype(BF16), v)
            out.append((m_new, lp, acc))
        return tuple(out)

    init = tuple((jnp.full((r, LANES), -jnp.inf, F32), jnp.zeros((r, LANES), F32),
                  jnp.zeros((r, dv), F32)) for _ in range(nh))
    carry = lax.fori_loop(0, qi, lambda j, c: block(j, c, False), init)
    carry = block(qi, carry, True)
    return [(acc, jnp.sum(lp, axis=-1, keepdims=True)) for (_, lp, acc) in carry]


def _fox_attn_kernel(q_ref, k_ref, v_ref, fcol_ref, frow_ref, o_ref, *, tq):
    qi = pl.program_id(1)
    qs = [q_ref[0, :, h * HEAD_DIM:(h + 1) * HEAD_DIM] for h in range(GROUP_HEADS)]
    fqs = [fcol_ref[0, :, h:h + 1] for h in range(GROUP_HEADS)]
    res = _flash(qs, k_ref, v_ref, HEAD_DIM, HEAD_DIM, qi, tq, fqs, frow_ref)
    for h, (acc, l) in enumerate(res):
        o_ref[0, :, h * HEAD_DIM:(h + 1) * HEAD_DIM] = (acc / l).astype(o_ref.dtype)


def _fox_attn(p3, fcol, frow, tq=256):
    b, s, _ = p3.shape
    w = GROUP_W
    c0 = P_FOX // w
    return pl.pallas_call(
        functools.partial(_fox_attn_kernel, tq=tq),
        grid=(b, s // tq),
        in_specs=[pl.BlockSpec((1, tq, w), lambda i, j: (i, j, c0)),
                  pl.BlockSpec((1, s, w), lambda i, j: (i, 0, c0 + 1)),
                  pl.BlockSpec((1, s, w), lambda i, j: (i, 0, c0 + 2)),
                  pl.BlockSpec((1, tq, LANES), lambda i, j: (i, j, 0)),
                  pl.BlockSpec((1, 8, s), lambda i, j: (i, 0, 0))],
        out_specs=pl.BlockSpec((1, tq, w), lambda i, j: (i, j, 0)),
        out_shape=jax.ShapeDtypeStruct((b, s, w), BF16),
        compiler_params=_params("parallel", "arbitrary"),
        name="fox_attn",
    )(p3, p3, p3, fcol, frow)


def _rope_pairs(x, cos, sin, half, period):
    lane = lax.broadcasted_iota(jnp.int32, x.shape, x.ndim - 1)
    first = (lane & (period - 1)) < half
    rot = jnp.where(first, -pltpu.roll(x, LANES - half, x.ndim - 1), pltpu.roll(x, half, x.ndim - 1))
    return x * cos + rot * sin


def _mla_prep_kernel(p_ref, gq_ref, gkv_ref, wq_ref, wkv_ref, cos_ref, sin_ref,
                     q_ref, k_ref, v_ref, *, qscale):
    blk = p_ref[0].astype(F32)
    cos = cos_ref[0]
    sin = sin_ref[0]
    cq = blk[:, 0:MLA_Q_RANK]
    ckv = blk[:, MLA_Q_RANK:MLA_Q_RANK + MLA_KV_RANK]
    kr = blk[:, MLA_Q_RANK + MLA_KV_RANK:MLA_Q_RANK + MLA_KV_RANK + LANES]
    q = _dot(_rms(cq, gq_ref[...]).astype(BF16), wq_ref[...])
    kv = _dot(_rms(ckv, gkv_ref[...]).astype(BF16), wkv_ref[...])
    kr = _rope_pairs(kr, cos, sin, MLA_ROPE // 2, LANES).astype(BF16)
    for h in range(GROUP_HEADS):
        o = h * MLA_QK_PAD
        q_ref[0, :, o:o + LANES] = (q[:, o:o + LANES] * qscale).astype(BF16)
        q_ref[0, :, o + LANES:o + 2 * LANES] = (_rope_pairs(
            q[:, o + LANES:o + 2 * LANES], cos, sin, MLA_ROPE // 2, LANES) * qscale).astype(BF16)
        k_ref[0, :, o:o + LANES] = kv[:, o:o + LANES].astype(BF16)
        k_ref[0, :, o + LANES:o + 2 * LANES] = kr
        v_ref[0, :, h * MLA_V:(h + 1) * MLA_V] = kv[:, o + LANES:o + 2 * LANES].astype(BF16)


def _mla_prep(p3, g_cq, g_ckv, w_uq, w_ukv, cos, sin, l, tm=512):
    b, s, _ = p3.shape
    c0 = P_MLA // MLA_BLOCK_W
    hq = GROUP_HEADS * MLA_QK_PAD
    qscale = (MLA_NOPE + MLA_ROPE) ** -0.5 * LOG2E
    return pl.pallas_call(
        functools.partial(_mla_prep_kernel, qscale=qscale),
        grid=(b, s // tm),
        in_specs=[pl.BlockSpec((1, tm, MLA_BLOCK_W), lambda i, j: (i, j, c0)),
                  pl.BlockSpec((1, MLA_Q_RANK), lambda i, j: (0, 0)),
                  pl.BlockSpec((1, MLA_KV_RANK), lambda i, j: (0, 0)),
                  pl.BlockSpec((None, MLA_Q_RANK, hq), lambda i, j: (l, 0, 0)),
                  pl.BlockSpec((None, MLA_KV_RANK, hq), lambda i, j: (l, 0, 0)),
                  pl.BlockSpec((1, tm, LANES), lambda i, j: (i, j, 0)),
                  pl.BlockSpec((1, tm, LANES), lambda i, j: (i, j, 0))],
        out_specs=[pl.BlockSpec((1, tm, hq), lambda i, j: (i, j, 0)),
                   pl.BlockSpec((1, tm, hq), lambda i, j: (i, j, 0)),
                   pl.BlockSpec((1, tm, GROUP_W), lambda i, j: (i, j, 0))],
        out_shape=[jax.ShapeDtypeStruct((b, s, hq), BF16),
                   jax.ShapeDtypeStruct((b, s, hq), BF16),
                   jax.ShapeDtypeStruct((b, s, GROUP_W), BF16)],
        compiler_params=_params("parallel", "parallel"),
        name="mla_prep",
    )(p3, g_cq, g_ckv, w_uq, w_ukv, cos, sin)


def _mla_attn_kernel(q_ref, k_ref, v_ref, o_ref, *, tq):
    qi = pl.program_id(1)
    qs = [q_ref[0, :, h * MLA_QK_PAD:(h + 1) * MLA_QK_PAD] for h in range(GROUP_HEADS)]
    res = _flash(qs, k_ref, v_ref, MLA_QK_PAD, MLA_V, qi, tq)
    for h, (acc, l) in enumerate(res):
        o_ref[0, :, h * MLA_V:(h + 1) * MLA_V] = (acc / l).astype(o_ref.dtype)


def _mla_attn(qm, km, vm, tq=256):
    b, s, hq = qm.shape
    return pl.pallas_call(
        functools.partial(_mla_attn_kernel, tq=tq),
        grid=(b, s // tq),
        in_specs=[pl.BlockSpec((1, tq, hq), lambda i, j: (i, j, 0)),
                  pl.BlockSpec((1, s, hq), lambda i, j: (i, 0, 0)),
                  pl.BlockSpec((1, s, GROUP_W), lambda i, j: (i, 0, 0))],
        out_specs=pl.BlockSpec((1, tq, GROUP_W), lambda i, j: (i, j, 0)),
        out_shape=jax.ShapeDtypeStruct((b, s, GROUP_W), BF16),
        compiler_params=_params("parallel", "arbitrary"),
        name="mla_attn",
    )(qm, km, vm)


def _diff_prep_kernel(q_ref, k_ref, cos_ref, sin_ref, qo_ref, ko_ref, *, qscale):
    cos = cos_ref[0]
    sin = sin_ref[0]
    for src, dst, scale in ((q_ref, qo_ref, qscale), (k_ref, ko_ref, None)):
        for c in range(GROUP_W // LANES):
            x = src[0, :, c * LANES:(c + 1) * LANES].astype(F32)
            y = _rope_pairs(x, cos, sin, DIFF_ROT // 2, DIFF_D)
            if scale is not None:
                y = y * scale
            dst[0, :, c * LANES:(c + 1) * LANES] = y.astype(BF16)


def _diff_prep(p3, cos, sin, tm=1024):
    b, s, _ = p3.shape
    c0 = P_DIFF // GROUP_W
    return pl.pallas_call(
        functools.partial(_diff_prep_kernel, qscale=DIFF_D ** -0.5 * LOG2E),
        grid=(b, s // tm),
        in_specs=[pl.BlockSpec((1, tm, GROUP_W), lambda i, j: (i, j, c0)),
                  pl.BlockSpec((1, tm, GROUP_W), lambda i, j: (i, j, c0 + 1)),
                  pl.BlockSpec((1, tm, LANES), lambda i, j: (i, j, 0)),
                  pl.BlockSpec((1, tm, LANES), lambda i, j: (i, j, 0))],
        out_specs=[pl.BlockSpec((1, tm, GROUP_W), lambda i, j: (i, j, 0)),
                   pl.BlockSpec((1, tm, GROUP_W), lambda i, j: (i, j, 0))],
        out_shape=[jax.ShapeDtypeStruct((b, s, GROUP_W), BF16),
                   jax.ShapeDtypeStruct((b, s, GROUP_W), BF16)],
        compiler_params=_params("parallel", "parallel"),
        name="diff_prep",
    )(p3, p3, cos, sin)


def _diff_attn_kernel(q_ref, k_ref, v_ref, lam_ref, g_ref, o_ref, *, tq, lam_init):
    qi = pl.program_id(1)
    lv = lam_ref[...]
    lam = (jnp.exp(jnp.sum(lv[0:1] * lv[1:2], axis=-1, keepdims=True))
           - jnp.exp(jnp.sum(lv[2:3] * lv[3:4], axis=-1, keepdims=True)) + lam_init)
    lane = lax.broadcasted_iota(jnp.int32, (tq, HEAD_DIM), 1)
    qs = []
    for h in range(GROUP_HEADS):
        q = q_ref[0, :, h * HEAD_DIM:(h + 1) * HEAD_DIM]
        zero = jnp.zeros_like(q)
        qs.append(jnp.concatenate([jnp.where(lane < DIFF_D, q, zero),
                                   jnp.where(lane >= DIFF_D, q, zero)], axis=0))
    res = _flash(qs, k_ref, v_ref, HEAD_DIM, HEAD_DIM, qi, tq)
    for h, (acc, l) in enumerate(res):
        o = acc / l
        o = o[0:tq] - lam * o[tq:2 * tq]
        o = _rms(o, g_ref[...]) * (1.0 - lam_init)
        o_ref[0, :, h * HEAD_DIM:(h + 1) * HEAD_DIM] = o.astype(o_ref.dtype)


def _diff_attn(qr, kr, p3, lamv, g_diff, lam_init, tq=256):
    b, s, w = qr.shape
    cv = P_DIFF // w + 2
    return pl.pallas_call(
        functools.partial(_diff_attn_kernel, tq=tq, lam_init=lam_init),
        grid=(b, s // tq),
        in_specs=[pl.BlockSpec((1, tq, w), lambda i, j: (i, j, 0)),
                  pl.BlockSpec((1, s, w), lambda i, j: (i, 0, 0)),
                  pl.BlockSpec((1, s, w), lambda i, j: (i, 0, cv)),
                  pl.BlockSpec((4, DIFF_D), lambda i, j: (0, 0)),
                  pl.BlockSpec((1, HEAD_DIM), lambda i, j: (0, 0))],
        out_specs=pl.BlockSpec((1, tq, w), lambda i, j: (i, j, 0)),
        out_shape=jax.ShapeDtypeStruct((b, s, w), BF16),
        compiler_params=_params("parallel", "arbitrary"),
        name="diff_attn",
    )(qr, kr, p3, lamv, g_diff)


def _sgu_kernel(uv_ref, lg_ref, lb_ref, ws_ref, bs_ref, o_ref, *, tm):
    z = uv_ref[...].astype(F32)
    z = 0.5 * z * (1.0 + jnp.tanh(math.sqrt(2.0 / math.pi) * (z + 0.044715 * (z * z * z))))
    u = z[:, 0:GROUP_W]
    v = z[:, GROUP_W:2 * GROUP_W]
    xc = v - jnp.mean(v, axis=-1, keepdims=True)
    vn = xc * lax.rsqrt(jnp.mean(xc * xc, axis=-1, keepdims=True) + EPS) * lg_ref[...] + lb_ref[...]
    vn = vn.astype(BF16)
    t = SGU_CHUNK
    tril = (lax.broadcasted_iota(jnp.int32, (t, t), 1) <= lax.broadcasted_iota(jnp.int32, (t, t), 0))
    for g in range(GROUP_HEADS):
        w = jnp.where(tril, ws_ref[g], 0.0).astype(BF16)
        bias = bs_ref[:, g:g + 1]
        for c in range(tm // t):
            rows = slice(c * t, (c + 1) * t)
            cols = slice(g * HEAD_DIM, (g + 1) * HEAD_DIM)
            mixed = _dot(w, vn[rows, cols]) + bias
            o_ref[rows, cols] = (u[rows, cols] * mixed).astype(o_ref.dtype)


def _sgu(p, ln_g, ln_b, w_s, b_st, l, tm=512):
    t = p.shape[0]
    c0 = P_SGU // (2 * GROUP_W)
    return pl.pallas_call(
        functools.partial(_sgu_kernel, tm=tm),
        grid=(t // tm,),
        in_specs=[pl.BlockSpec((tm, 2 * GROUP_W), lambda i: (i, c0)),
                  pl.BlockSpec((1, GROUP_W), lambda i: (0, 0)),
                  pl.BlockSpec((1, GROUP_W), lambda i: (0, 0)),
                  pl.BlockSpec((None, GROUP_HEADS, SGU_CHUNK, SGU_CHUNK), lambda i: (l, 0, 0, 0)),
                  pl.BlockSpec((None, SGU_CHUNK, GROUP_HEADS), lambda i: (l, 0, 0))],
        out_specs=pl.BlockSpec((tm, GROUP_W), lambda i: (i, 0)),
        out_shape=jax.ShapeDtypeStruct((t, GROUP_W), BF16),
        compiler_params=_params("parallel"),
        name="sgu",
    )(p, ln_g, ln_b, w_s, b_st)


def _out_proj_kernel(x_ref, ya_ref, yb_ref, yc_ref, yd_ref, w_ref, o_ref):
    acc = x_ref[...]
    for m, y_ref in enumerate((ya_ref, yb_ref, yc_ref, yd_ref)):
        acc = acc + _dot(y_ref[...], w_ref[m * GROUP_W:(m + 1) * GROUP_W, :])
    o_ref[...] = acc


def _out_proj(x, ys, w, l, tm=1024, tn=1024):
    t, d = x.shape
    yspec = pl.BlockSpec((tm, GROUP_W), lambda i, j: (i, 0))
    return pl.pallas_call(
        _out_proj_kernel,
        grid=(t // tm, d // tn),
        in_specs=[pl.BlockSpec((tm, tn), lambda i, j: (i, j)), yspec, yspec, yspec, yspec,
                  pl.BlockSpec((None, w.shape[1], tn), lambda i, j: (l, 0, j))],
        out_specs=pl.BlockSpec((tm, tn), lambda i, j: (i, j)),
        out_shape=jax.ShapeDtypeStruct((t, d), F32),
        compiler_params=_params("parallel", "parallel"),
        name="out_proj",
    )(x, *ys, w)


def _cross_kernel(x_ref, g_ref, wq_ref, kv_ref, wo_ref, o_ref, oh_ref, *, scale):
    x = x_ref[...]
    q = _dot(_rms(x, g_ref[...]).astype(BF16), wq_ref[...]).astype(BF16)
    for h in range(CROSS_HEADS):
        cols = slice(h * CROSS_HEAD_DIM, (h + 1) * CROSS_HEAD_DIM)
        k = kv_ref[0, :, cols]
        v = kv_ref[0, :, CROSS_W + h * CROSS_HEAD_DIM:CROSS_W + (h + 1) * CROSS_HEAD_DIM]
        s = _dot_nt(q[:, cols], k) * scale
        p = jnp.exp(s - jnp.max(s, axis=-1, keepdims=True))
        l = jnp.sum(p, axis=-1, keepdims=True)
        oh_ref[:, cols] = (_dot(p.astype(BF16), v) / l).astype(BF16)
    o_ref[...] = x + _dot(oh_ref[...], wo_ref[...])


def _cross(x, g, wq, kv, wo, seq, l, tm=512):
    t, d = x.shape
    per_b = seq // tm
    m = kv.shape[1]
    return pl.pallas_call(
        functools.partial(_cross_kernel, scale=CROSS_HEAD_DIM ** -0.5),
        grid=(t // tm,),
        in_specs=[pl.BlockSpec((tm, d), lambda i: (i, 0)),
                  pl.BlockSpec((1, d), lambda i: (0, 0)),
                  pl.BlockSpec((None, d, CROSS_W), lambda i: (l, 0, 0)),
                  pl.BlockSpec((1, m, 2 * CROSS_W), lambda i: (i // per_b, 0, 0)),
                  pl.BlockSpec((None, CROSS_W, d), lambda i: (l, 0, 0))],
        out_specs=pl.BlockSpec((tm, d), lambda i: (i, 0)),
        out_shape=jax.ShapeDtypeStruct((t, d), F32),
        scratch_shapes=[pltpu.VMEM((tm, CROSS_W), BF16)],
        compiler_params=_params("parallel"),
        name="cross_attn",
    )(x, g, wq, kv, wo)


def _ffn_kernel(x_ref, g_ref, wg_ref, wu_ref, wd_ref, gf_ref, o_ref, xn_ref, acc_ref, *, final):
    j = pl.program_id(1)

    @pl.when(j == 0)
    def _():
        xn_ref[...] = _rms(x_ref[...], g_ref[...]).astype(BF16)
        acc_ref[...] = jnp.zeros_like(acc_ref)

    xn = xn_ref[...]
    gate = _dot(xn, wg_ref[...])
    up = _dot(xn, wu_ref[...])
    hid = (gate * jax.nn.sigmoid(gate) * up).astype(BF16)
    acc_ref[...] += _dot(hid, wd_ref[...])

    @pl.when(j == pl.num_programs(1) - 1)
    def _():
        y = x_ref[...] + acc_ref[...]
        o_ref[...] = _rms(y, gf_ref[...]) if final else y


def _ffn(x, g, wg, wu, wd, g_final, final, l, tm=512, th=512):
    t, d = x.shape
    hdim = wg.shape[2]
    return pl.pallas_call(
        functools.partial(_ffn_kernel, final=final),
        grid=(t // tm, hdim // th),
        in_specs=[pl.BlockSpec((tm, d), lambda i, j: (i, 0)),
                  pl.BlockSpec((1, d), lambda i, j: (0, 0)),
                  pl.BlockSpec((None, d, th), lambda i, j: (l, 0, j)),
                  pl.BlockSpec((None, d, th), lambda i, j: (l, 0, j)),
                  pl.BlockSpec((None, th, d), lambda i, j: (l, j, 0)),
                  pl.BlockSpec((1, d), lambda i, j: (0, 0))],
        out_specs=pl.BlockSpec((tm, d), lambda i, j: (i, 0)),
        out_shape=jax.ShapeDtypeStruct((t, d), F32),
        scratch_shapes=[pltpu.VMEM((tm, d), BF16), pltpu.VMEM((tm, d), F32)],
        compiler_params=_params("parallel", "arbitrary"),
        name="swiglu",
    )(x, g, wg, wu, wd, g_final)


def _rope_tables(positions):
    pos = positions.astype(F32)[..., None]
    b, s = positions.shape

    def table(rot_dim, period):
        freqs = ROPE_THETA ** (-jnp.arange(0, rot_dim, 2, dtype=F32) / rot_dim)
        ang = pos * freqs
        ones = jnp.ones((b, s, period - rot_dim), F32)
        cos = jnp.concatenate([jnp.cos(ang), jnp.cos(ang), ones], axis=-1)
        sin = jnp.concatenate([jnp.sin(ang), jnp.sin(ang), 0.0 * ones], axis=-1)
        reps = LANES // period
        return jnp.tile(cos, (1, 1, reps)), jnp.tile(sin, (1, 1, reps))

    return table(MLA_ROPE, LANES), table(DIFF_ROT, DIFF_D)


def _split_w_in(w):
    sizes = (GROUP_W, GROUP_W, GROUP_W, GROUP_HEADS, MLA_Q_RANK, MLA_KV_RANK, MLA_ROPE,
             2 * GROUP_W, GROUP_W, GROUP_W, GROUP_W)
    offs = [0]
    for sz in sizes:
        offs.append(offs[-1] + sz)
    fq, fk, fv, ff, cq, ckv, kr, uv, dq, dk, dv = (w[..., offs[i]:offs[i + 1]] for i in range(11))
    lead = w.shape[:-1]
    pad = jnp.zeros(lead + (P_W - P_MLA - MLA_Q_RANK - MLA_KV_RANK - MLA_ROPE,), w.dtype)
    main = jnp.concatenate([fq, fk, fv, dq, dk, dv, uv, cq, ckv, kr, pad], axis=-1).astype(BF16)
    wf = jnp.concatenate([ff, jnp.zeros(lead + (LANES - GROUP_HEADS,), w.dtype)], axis=-1).astype(BF16)
    return main, wf


def _pad_heads(w, used, width):
    nl, k, _ = w.shape
    w = w.reshape(nl, k, GROUP_HEADS, used)
    w = jnp.pad(w, ((0, 0), (0, 0), (0, 0), (0, width - used)))
    return w.reshape(nl, k, GROUP_HEADS * width)


def kernel(x, mem, positions, g_mix, w_in, b_f, g_cq, g_ckv, w_uq, w_ukv, sgu_ln_g, sgu_ln_b, w_s,
           b_s, lam_q1, lam_k1, lam_q2, lam_k2, g_diff, w_o, g_mem, g_cross, w_cq, w_ck, w_cv, w_co,
           g_ffn, w_gate, w_up, w_down, g_final):
    b, s, d = x.shape
    depth = w_in.shape[0]
    t = b * s
    (cos_m, sin_m), (cos_d, sin_d) = _rope_tables(positions)
    xf = x.reshape(t, d)
    memf = mem.reshape(b * mem.shape[1], d)
    row = lambda a: a.reshape(1, -1)

    w_main, w_f = _split_w_in(w_in)
    w_uq_p = _pad_heads(w_uq, MLA_NOPE + MLA_ROPE, MLA_QK_PAD).astype(BF16)
    w_ukv_b = w_ukv.astype(BF16)
    w_o_b, w_cq_b, w_ck_b, w_cv_b, w_co_b, w_gate_b, w_up_b, w_down_b = (
        _cast_bf16(w) for w in (w_o, w_cq, w_ck, w_cv, w_co, w_gate, w_up, w_down))
    b_st = jnp.swapaxes(b_s, 1, 2)
    col_scale = jnp.ones((1, P_W), F32).at[:, P_FOX:P_FOX + GROUP_W].set(HEAD_DIM ** -0.5 * LOG2E)

    for l in range(depth):
        lam_init = 0.8 - 0.6 * math.exp(-0.3 * l)
        p, fb = _in_proj(xf, row(g_mix[l]), w_main, w_f, col_scale, l)
        p3 = p.reshape(b, s, P_W)

        bf_pad = jnp.pad(b_f[l], (0, LANES - GROUP_HEADS)).reshape(1, LANES)
        fcol, frow = _fox_gate(fb.reshape(b, s, LANES), bf_pad)
        y_a = _fox_attn(p3, fcol, frow)

        qm, km, vm = _mla_prep(p3, row(g_cq[l]), row(g_ckv[l]), w_uq_p, w_ukv_b, cos_m, sin_m, l)
        y_b = _mla_attn(qm, km, vm)

        y_c = _sgu(p, row(sgu_ln_g[l]), row(sgu_ln_b[l]), w_s, b_st, l)

        qr, kr = _diff_prep(p3, cos_d, sin_d)
        lamv = jnp.stack([lam_q1[l], lam_k1[l], lam_q2[l], lam_k2[l]])
        y_d = _diff_attn(qr, kr, p3, lamv, row(g_diff[l]), lam_init)

        ys = [y.reshape(t, GROUP_W) for y in (y_a, y_b, y_c, y_d)]
        xf = _out_proj(xf, ys, w_o_b, l)

        kv = _mem_kv(memf, row(g_mem), w_ck_b, w_cv_b, l).reshape(b, mem.shape[1], 2 * CROSS_W)
        xf = _cross(xf, row(g_cross[l]), w_cq_b, kv, w_co_b, s, l)

        xf = _ffn(xf, row(g_ffn[l]), w_gate_b, w_up_b, w_down_b, row(g_final),
                  final=(l == depth - 1), l=l)
    return xf.reshape(b, s, d)
```

```python
import functools
import math

import jax
import jax.numpy as jnp
from jax import lax
from jax.experimental import pallas as pl
from jax.experimental.pallas import tpu as pltpu

F32 = jnp.float32
BF16 = jnp.bfloat16

HEAD_DIM = 128
GROUP_HEADS = 4
GROUP_W = GROUP_HEADS * HEAD_DIM
MLA_Q_RANK = 384
MLA_KV_RANK = 256
MLA_NOPE = 128
MLA_ROPE = 64
MLA_V = 128
MLA_QK_PAD = 256
SGU_CHUNK = 128
DIFF_D = HEAD_DIM // 2
DIFF_ROT = DIFF_D // 4
CROSS_HEADS = 4
CROSS_HEAD_DIM = 128
CROSS_W = CROSS_HEADS * CROSS_HEAD_DIM
ROPE_THETA = 500000.0
EPS = 1e-6
LOG2E = math.log2(math.e)
LANES = 128

P_FOX = 0
P_DIFF = 3 * GROUP_W
P_SGU = 6 * GROUP_W
P_MLA = 8 * GROUP_W
P_W = 10 * GROUP_W
MLA_BLOCK_W = 2 * GROUP_W

VMEM_LIMIT = 52 * 2**20


def _params(*sem):
    return pltpu.CompilerParams(dimension_semantics=sem, vmem_limit_bytes=VMEM_LIMIT)


def _rms(x, g):
    return x * lax.rsqrt(jnp.mean(x * x, axis=-1, keepdims=True) + EPS) * g


def _dot(a, b):
    return jnp.dot(a, b, preferred_element_type=F32)


def _dot_nt(a, b):
    return lax.dot_general(a, b, (((1,), (1,)), ((), ())), preferred_element_type=F32)


def _cast_kernel(w_ref, o_ref):
    o_ref[...] = w_ref[...].astype(o_ref.dtype)


def _cast_bf16(w, block_bytes=4 * 2**20):
    nl, k, n = w.shape
    tk = k
    while tk * n * 4 > block_bytes and tk % 16 == 0:
        tk //= 2
    return pl.pallas_call(
        _cast_kernel,
        grid=(nl, k // tk),
        in_specs=[pl.BlockSpec((1, tk, n), lambda l, i: (l, i, 0))],
        out_specs=pl.BlockSpec((1, tk, n), lambda l, i: (l, i, 0)),
        out_shape=jax.ShapeDtypeStruct(w.shape, BF16),
        compiler_params=_params("parallel", "parallel"),
        name="cast_bf16",
    )(w)


def _in_proj_kernel(x_ref, g_ref, w_ref, wf_ref, cs_ref, p_ref, f_ref, xn_ref):
    @pl.when(pl.program_id(1) == 0)
    def _():
        xb = _rms(x_ref[...], g_ref[...]).astype(BF16)
        xn_ref[...] = xb
        f_ref[...] = _dot(xb, wf_ref[...])

    p_ref[...] = (_dot(xn_ref[...], w_ref[...]) * cs_ref[...]).astype(p_ref.dtype)


def _in_proj(x, g, w, wf, col_scale, l, tm=1024, tn=1024):
    t, d = x.shape
    n = w.shape[2]
    return pl.pallas_call(
        _in_proj_kernel,
        grid=(t // tm, n // tn),
        in_specs=[pl.BlockSpec((tm, d), lambda i, j: (i, 0)),
                  pl.BlockSpec((1, d), lambda i, j: (0, 0)),
                  pl.BlockSpec((None, d, tn), lambda i, j: (l, 0, j)),
                  pl.BlockSpec((None, d, LANES), lambda i, j: (l, 0, 0)),
                  pl.BlockSpec((1, tn), lambda i, j: (0, j))],
        out_specs=[pl.BlockSpec((tm, tn), lambda i, j: (i, j)),
                   pl.BlockSpec((tm, LANES), lambda i, j: (i, 0))],
        out_shape=[jax.ShapeDtypeStruct((t, n), BF16),
                   jax.ShapeDtypeStruct((t, LANES), F32)],
        scratch_shapes=[pltpu.VMEM((tm, d), BF16)],
        compiler_params=_params("parallel", "arbitrary"),
        name="in_proj",
    )(x, g, w, wf, col_scale)


def _mem_kv_kernel(x_ref, g_ref, wk_ref, wv_ref, o_ref):
    xb = _rms(x_ref[...], g_ref[...]).astype(BF16)
    o_ref[:, 0:CROSS_W] = _dot(xb, wk_ref[...]).astype(o_ref.dtype)
    o_ref[:, CROSS_W:2 * CROSS_W] = _dot(xb, wv_ref[...]).astype(o_ref.dtype)


def _mem_kv(x, g, wk, wv, l, tm=512):
    t, d = x.shape
    wspec = pl.BlockSpec((None, d, CROSS_W), lambda i: (l, 0, 0))
    return pl.pallas_call(
        _mem_kv_kernel,
        grid=(t // tm,),
        in_specs=[pl.BlockSpec((tm, d), lambda i: (i, 0)),
                  pl.BlockSpec((1, d), lambda i: (0, 0)), wspec, wspec],
        out_specs=pl.BlockSpec((tm, 2 * CROSS_W), lambda i: (i, 0)),
        out_shape=jax.ShapeDtypeStruct((t, 2 * CROSS_W), BF16),
        compiler_params=_params("parallel"),
        name="mem_kv_proj",
    )(x, g, wk, wv)


def _fox_gate_kernel(f_ref, b_ref, col_ref, row_ref):
    x = f_ref[0] + b_ref[...]
    c = jnp.minimum(x, 0.0) - jnp.log1p(jnp.exp(-jnp.abs(x)))
    s = x.shape[0]
    ridx = lax.broadcasted_iota(jnp.int32, x.shape, 0)
    shift = 1
    while shift < s:
        c = c + jnp.where(ridx >= shift, pltpu.roll(c, shift, 0), 0.0)
        shift *= 2
    c = c * LOG2E
    col_ref[0] = c
    row_ref[0] = c.T[0:8, :]


def _fox_gate(fb, b_f):
    b, s, _ = fb.shape
    return pl.pallas_call(
        _fox_gate_kernel,
        grid=(b,),
        in_specs=[pl.BlockSpec((1, s, LANES), lambda i: (i, 0, 0)),
                  pl.BlockSpec((1, LANES), lambda i: (0, 0))],
        out_specs=[pl.BlockSpec((1, s, LANES), lambda i: (i, 0, 0)),
                   pl.BlockSpec((1, 8, s), lambda i: (i, 0, 0))],
        out_shape=[jax.ShapeDtypeStruct((b, s, LANES), F32),
                   jax.ShapeDtypeStruct((b, 8, s), F32)],
        compiler_params=_params("parallel"),
        name="fox_gate",
    )(fb, b_f)


def _flash(qs, k_ref, v_ref, dk, dv, qi, tq, fqs=None, frow_ref=None):
    nh = len(qs)
    r = qs[0].shape[0]
    assert dv == LANES and tq % LANES == 0
    groups = tq // LANES
    row = lax.broadcasted_iota(jnp.int32, (r, tq), 0) & (tq - 1)
    col = lax.broadcasted_iota(jnp.int32, (r, tq), 1)
    causal = col <= row
    if fqs is not None:
        fqs = [jnp.broadcast_to(f, (r, LANES)) for f in fqs]

    def block(j, carry, diag):
        start = pl.multiple_of(j * tq, tq)
        out = []
        for h in range(nh):
            m, lp, acc = carry[h]
            k = k_ref[0, pl.ds(start, tq), h * dk:(h + 1) * dk]
            s = _dot_nt(qs[h], k)
            if frow_ref is not None:
                s = s - frow_ref[0, h:h + 1, pl.ds(start, tq)]
            if diag:
                s = jnp.where(causal, s, -jnp.inf)
            mz = jnp.broadcast_to(jnp.max(s, axis=-1, keepdims=True), (r, LANES))
            if fqs is not None:
                mz = mz + fqs[h]
            m_new = jnp.maximum(m, mz)
            alpha = jnp.exp2(m - m_new)
            shift = (fqs[h] - m_new) if fqs is not None else -m_new
            p = jnp.exp2(s + jnp.concatenate([shift] * groups, axis=1))
            psum = p[:, 0:LANES]
            for g in range(1, groups):
                psum = psum + p[:, g * LANES:(g + 1) * LANES]
            lp = alpha * lp + psum
            v = v_ref[0, pl.ds(start, tq), h * dv:(h + 1) * dv]
            acc = alpha * acc + _dot(p.astype(BF16), v)
            out.append((m_new, lp, acc))
        return tuple(out)

    init = tuple((jnp.full((r, LANES), -jnp.inf, F32), jnp.zeros((r, LANES), F32),
                  jnp.zeros((r, dv), F32)) for _ in range(nh))
    carry = lax.fori_loop(0, qi, lambda j, c: block(j, c, False), init)
    carry = block(qi, carry, True)
    return [(acc, jnp.sum(lp, axis=-1, keepdims=True)) for (_, lp, acc) in carry]


def _fox_attn_kernel(q_ref, k_ref, v_ref, fcol_ref, frow_ref, o_ref, *, tq):
    qi = pl.program_id(1)
    qs = [q_ref[0, :, h * HEAD_DIM:(h + 1) * HEAD_DIM] for h in range(GROUP_HEADS)]
    fqs = [fcol_ref[0, :, h:h + 1] for h in range(GROUP_HEADS)]
    res = _flash(qs, k_ref, v_ref, HEAD_DIM, HEAD_DIM, qi, tq, fqs, frow_ref)
    for h, (acc, l) in enumerate(res):
        o_ref[0, :, h * HEAD_DIM:(h + 1) * HEAD_DIM] = (acc / l).astype(o_ref.dtype)


def _fox_attn(p3, fcol, frow, tq=512):
    b, s, _ = p3.shape
    w = GROUP_W
    c0 = P_FOX // w
    return pl.pallas_call(
        functools.partial(_fox_attn_kernel, tq=tq),
        grid=(b, s // tq),
        in_specs=[pl.BlockSpec((1, tq, w), lambda i, j: (i, j, c0)),
                  pl.BlockSpec((1, s, w), lambda i, j: (i, 0, c0 + 1)),
                  pl.BlockSpec((1, s, w), lambda i, j: (i, 0, c0 + 2)),
                  pl.BlockSpec((1, tq, LANES), lambda i, j: (i, j, 0)),
                  pl.BlockSpec((1, 8, s), lambda i, j: (i, 0, 0))],
        out_specs=pl.BlockSpec((1, tq, w), lambda i, j: (i, j, 0)),
        out_shape=jax.ShapeDtypeStruct((b, s, w), BF16),
        compiler_params=_params("parallel", "arbitrary"),
        name="fox_attn",
    )(p3, p3, p3, fcol, frow)


def _rope_pairs(x, cos, sin, half, period):
    lane = lax.broadcasted_iota(jnp.int32, x.shape, x.ndim - 1)
    first = (lane & (period - 1)) < half
    rot = jnp.where(first, -pltpu.roll(x, LANES - half, x.ndim - 1), pltpu.roll(x, half, x.ndim - 1))
    return x * cos + rot * sin


def _mla_prep_kernel(p_ref, gq_ref, gkv_ref, wq_ref, wkv_ref, cos_ref, sin_ref,
                     q_ref, k_ref, v_ref, *, qscale):
    blk = p_ref[0].astype(F32)
    cos = cos_ref[0]
    sin = sin_ref[0]
    cq = blk[:, 0:MLA_Q_RANK]
    ckv = blk[:, MLA_Q_RANK:MLA_Q_RANK + MLA_KV_RANK]
    kr = blk[:, MLA_Q_RANK + MLA_KV_RANK:MLA_Q_RANK + MLA_KV_RANK + LANES]
    q = _dot(_rms(cq, gq_ref[...]).astype(BF16), wq_ref[...])
    kv = _dot(_rms(ckv, gkv_ref[...]).astype(BF16), wkv_ref[...])
    kr = _rope_pairs(kr, cos, sin, MLA_ROPE // 2, LANES).astype(BF16)
    for h in range(GROUP_HEADS):
        o = h * MLA_QK_PAD
        q_ref[0, :, o:o + LANES] = (q[:, o:o + LANES] * qscale).astype(BF16)
        q_ref[0, :, o + LANES:o + 2 * LANES] = (_rope_pairs(
            q[:, o + LANES:o + 2 * LANES], cos, sin, MLA_ROPE // 2, LANES) * qscale).astype(BF16)
        k_ref[0, :, o:o + LANES] = kv[:, o:o + LANES].astype(BF16)
        k_ref[0, :, o + LANES:o + 2 * LANES] = kr
        v_ref[0, :, h * MLA_V:(h + 1) * MLA_V] = kv[:, o + LANES:o + 2 * LANES].astype(BF16)


def _mla_prep(p3, g_cq, g_ckv, w_uq, w_ukv, cos, sin, l, tm=512):
    b, s, _ = p3.shape
    c0 = P_MLA // MLA_BLOCK_W
    hq = GROUP_HEADS * MLA_QK_PAD
    qscale = (MLA_NOPE + MLA_ROPE) ** -0.5 * LOG2E
    return pl.pallas_call(
        functools.partial(_mla_prep_kernel, qscale=qscale),
        grid=(b, s // tm),
        in_specs=[pl.BlockSpec((1, tm, MLA_BLOCK_W), lambda i, j: (i, j, c0)),
                  pl.BlockSpec((1, MLA_Q_RANK), lambda i, j: (0, 0)),
                  pl.BlockSpec((1, MLA_KV_RANK), lambda i, j: (0, 0)),
                  pl.BlockSpec((None, MLA_Q_RANK, hq), lambda i, j: (l, 0, 0)),
                  pl.BlockSpec((None, MLA_KV_RANK, hq), lambda i, j: (l, 0, 0)),
                  pl.BlockSpec((1, tm, LANES), lambda i, j: (i, j, 0)),
                  pl.BlockSpec((1, tm, LANES), lambda i, j: (i, j, 0))],
        out_specs=[pl.BlockSpec((1, tm, hq), lambda i, j: (i, j, 0)),
                   pl.BlockSpec((1, tm, hq), lambda i, j: (i, j, 0)),
                   pl.BlockSpec((1, tm, GROUP_W), lambda i, j: (i, j, 0))],
        out_shape=[jax.ShapeDtypeStruct((b, s, hq), BF16),
                   jax.ShapeDtypeStruct((b, s, hq), BF16),
                   jax.ShapeDtypeStruct((b, s, GROUP_W), BF16)],
        compiler_params=_params("parallel", "parallel"),
        name="mla_prep",
    )(p3, g_cq, g_ckv, w_uq, w_ukv, cos, sin)


def _mla_attn_kernel(q_ref, k_ref, v_ref, o_ref, *, tq):
    qi = pl.program_id(1)
    qs = [q_ref[0, :, h * MLA_QK_PAD:(h + 1) * MLA_QK_PAD] for h in range(GROUP_HEADS)]
    res = _flash(qs, k_ref, v_ref, MLA_QK_PAD, MLA_V, qi, tq)
    for h, (acc, l) in enumerate(res):
        o_ref[0, :, h * MLA_V:(h + 1) * MLA_V] = (acc / l).astype(o_ref.dtype)


def _mla_attn(qm, km, vm, tq=512):
    b, s, hq = qm.shape
    return pl.pallas_call(
        functools.partial(_mla_attn_kernel, tq=tq),
        grid=(b, s // tq),
        in_specs=[pl.BlockSpec((1, tq, hq), lambda i, j: (i, j, 0)),
                  pl.BlockSpec((1, s, hq), lambda i, j: (i, 0, 0)),
                  pl.BlockSpec((1, s, GROUP_W), lambda i, j: (i, 0, 0))],
        out_specs=pl.BlockSpec((1, tq, GROUP_W), lambda i, j: (i, j, 0)),
        out_shape=jax.ShapeDtypeStruct((b, s, GROUP_W), BF16),
        compiler_params=_params("parallel", "arbitrary"),
        name="mla_attn",
    )(qm, km, vm)


def _diff_prep_kernel(q_ref, k_ref, cos_ref, sin_ref, qo_ref, ko_ref, *, qscale):
    cos = cos_ref[0]
    sin = sin_ref[0]
    for src, dst, scale in ((q_ref, qo_ref, qscale), (k_ref, ko_ref, None)):
        for c in range(GROUP_W // LANES):
            x = src[0, :, c * LANES:(c + 1) * LANES].astype(F32)
            y = _rope_pairs(x, cos, sin, DIFF_ROT // 2, DIFF_D)
            if scale is not None:
                y = y * scale
            dst[0, :, c * LANES:(c + 1) * LANES] = y.astype(BF16)


def _diff_prep(p3, cos, sin, tm=1024):
    b, s, _ = p3.shape
    c0 = P_DIFF // GROUP_W
    return pl.pallas_call(
        functools.partial(_diff_prep_kernel, qscale=DIFF_D ** -0.5 * LOG2E),
        grid=(b, s // tm),
        in_specs=[pl.BlockSpec((1, tm, GROUP_W), lambda i, j: (i, j, c0)),
                  pl.BlockSpec((1, tm, GROUP_W), lambda i, j: (i, j, c0 + 1)),
                  pl.BlockSpec((1, tm, LANES), lambda i, j: (i, j, 0)),
                  pl.BlockSpec((1, tm, LANES), lambda i, j: (i, j, 0))],
        out_specs=[pl.BlockSpec((1, tm, GROUP_W), lambda i, j: (i, j, 0)),
                   pl.BlockSpec((1, tm, GROUP_W), lambda i, j: (i, j, 0))],
        out_shape=[jax.ShapeDtypeStruct((b, s, GROUP_W), BF16),
                   jax.ShapeDtypeStruct((b, s, GROUP_W), BF16)],
        compiler_params=_params("parallel", "parallel"),
        name="diff_prep",
    )(p3, p3, cos, sin)


def _diff_attn_kernel(q_ref, k_ref, v_ref, lam_ref, g_ref, o_ref, *, tq, lam_init):
    qi = pl.program_id(1)
    lv = lam_ref[...]
    lam = (jnp.exp(jnp.sum(lv[0:1] * lv[1:2], axis=-1, keepdims=True))
           - jnp.exp(jnp.sum(lv[2:3] * lv[3:4], axis=-1, keepdims=True)) + lam_init)
    lane = lax.broadcasted_iota(jnp.int32, (tq, HEAD_DIM), 1)
    qs = []
    for h in range(GROUP_HEADS):
        q = q_ref[0, :, h * HEAD_DIM:(h + 1) * HEAD_DIM]
        zero = jnp.zeros_like(q)
        qs.append(jnp.concatenate([jnp.where(lane < DIFF_D, q, zero),
                                   jnp.where(lane >= DIFF_D, q, zero)], axis=0))
    res = _flash(qs, k_ref, v_ref, HEAD_DIM, HEAD_DIM, qi, tq)
    for h, (acc, l) in enumerate(res):
        o = acc / l
        o = o[0:tq] - lam * o[tq:2 * tq]
        o = _rms(o, g_ref[...]) * (1.0 - lam_init)
        o_ref[0, :, h * HEAD_DIM:(h + 1) * HEAD_DIM] = o.astype(o_ref.dtype)


def _diff_attn(qr, kr, p3, lamv, g_diff, lam_init, tq=512):
    b, s, w = qr.shape
    cv = P_DIFF // w + 2
    return pl.pallas_call(
        functools.partial(_diff_attn_kernel, tq=tq, lam_init=lam_init),
        grid=(b, s // tq),
        in_specs=[pl.BlockSpec((1, tq, w), lambda i, j: (i, j, 0)),
                  pl.BlockSpec((1, s, w), lambda i, j: (i, 0, 0)),
                  pl.BlockSpec((1, s, w), lambda i, j: (i, 0, cv)),
                  pl.BlockSpec((4, DIFF_D), lambda i, j: (0, 0)),
                  pl.BlockSpec((1, HEAD_DIM), lambda i, j: (0, 0))],
        out_specs=pl.BlockSpec((1, tq, w), lambda i, j: (i, j, 0)),
        out_shape=jax.ShapeDtypeStruct((b, s, w), BF16),
        compiler_params=_params("parallel", "arbitrary"),
        name="diff_attn",
    )(qr, kr, p3, lamv, g_diff)


def _sgu_kernel(uv_ref, lg_ref, lb_ref, ws_ref, bs_ref, o_ref, *, tm):
    z = uv_ref[...].astype(F32)
    z = 0.5 * z * (1.0 + jnp.tanh(math.sqrt(2.0 / math.pi) * (z + 0.044715 * (z * z * z))))
    u = z[:, 0:GROUP_W]
    v = z[:, GROUP_W:2 * GROUP_W]
    xc = v - jnp.mean(v, axis=-1, keepdims=True)
    vn = xc * lax.rsqrt(jnp.mean(xc * xc, axis=-1, keepdims=True) + EPS) * lg_ref[...] + lb_ref[...]
    vn = vn.astype(BF16)
    t = SGU_CHUNK
    tril = (lax.broadcasted_iota(jnp.int32, (t, t), 1) <= lax.broadcasted_iota(jnp.int32, (t, t), 0))
    for g in range(GROUP_HEADS):
        w = jnp.where(tril, ws_ref[g], 0.0).astype(BF16)
        bias = bs_ref[:, g:g + 1]
        for c in range(tm // t):
            rows = slice(c * t, (c + 1) * t)
            cols = slice(g * HEAD_DIM, (g + 1) * HEAD_DIM)
            mixed = _dot(w, vn[rows, cols]) + bias
            o_ref[rows, cols] = (u[rows, cols] * mixed).astype(o_ref.dtype)


def _sgu(p, ln_g, ln_b, w_s, b_st, l, tm=512):
    t = p.shape[0]
    c0 = P_SGU // (2 * GROUP_W)
    return pl.pallas_call(
        functools.partial(_sgu_kernel, tm=tm),
        grid=(t // tm,),
        in_specs=[pl.BlockSpec((tm, 2 * GROUP_W), lambda i: (i, c0)),
                  pl.BlockSpec((1, GROUP_W), lambda i: (0, 0)),
                  pl.BlockSpec((1, GROUP_W), lambda i: (0, 0)),
                  pl.BlockSpec((None, GROUP_HEADS, SGU_CHUNK, SGU_CHUNK), lambda i: (l, 0, 0, 0)),
                  pl.BlockSpec((None, SGU_CHUNK, GROUP_HEADS), lambda i: (l, 0, 0))],
        out_specs=pl.BlockSpec((tm, GROUP_W), lambda i: (i, 0)),
        out_shape=jax.ShapeDtypeStruct((t, GROUP_W), BF16),
        compiler_params=_params("parallel"),
        name="sgu",
    )(p, ln_g, ln_b, w_s, b_st)


def _out_proj_kernel(x_ref, ya_ref, yb_ref, yc_ref, yd_ref, w_ref, o_ref):
    acc = x_ref[...]
    for m, y_ref in enumerate((ya_ref, yb_ref, yc_ref, yd_ref)):
        acc = acc + _dot(y_ref[...], w_ref[m * GROUP_W:(m + 1) * GROUP_W, :])
    o_ref[...] = acc


def _out_proj(x, ys, w, l, tm=1024, tn=1024):
    t, d = x.shape
    yspec = pl.BlockSpec((tm, GROUP_W), lambda i, j: (i, 0))
    return pl.pallas_call(
        _out_proj_kernel,
        grid=(t // tm, d // tn),
        in_specs=[pl.BlockSpec((tm, tn), lambda i, j: (i, j)), yspec, yspec, yspec, yspec,
                  pl.BlockSpec((None, w.shape[1], tn), lambda i, j: (l, 0, j))],
        out_specs=pl.BlockSpec((tm, tn), lambda i, j: (i, j)),
        out_shape=jax.ShapeDtypeStruct((t, d), F32),
        compiler_params=_params("parallel", "parallel"),
        name="out_proj",
    )(x, *ys, w)


def _cross_kernel(x_ref, g_ref, wq_ref, kv_ref, wo_ref, o_ref, oh_ref, *, scale):
    x = x_ref[...]
    q = _dot(_rms(x, g_ref[...]).astype(BF16), wq_ref[...]).astype(BF16)
    for h in range(CROSS_HEADS):
        cols = slice(h * CROSS_HEAD_DIM, (h + 1) * CROSS_HEAD_DIM)
        k = kv_ref[0, :, cols]
        v = kv_ref[0, :, CROSS_W + h * CROSS_HEAD_DIM:CROSS_W + (h + 1) * CROSS_HEAD_DIM]
        s = _dot_nt(q[:, cols], k) * scale
        p = jnp.exp(s - jnp.max(s, axis=-1, keepdims=True))
        l = jnp.sum(p, axis=-1, keepdims=True)
        oh_ref[:, cols] = (_dot(p.astype(BF16), v) / l).astype(BF16)
    o_ref[...] = x + _dot(oh_ref[...], wo_ref[...])


def _cross(x, g, wq, kv, wo, seq, l, tm=512):
    t, d = x.shape
    per_b = seq // tm
    m = kv.shape[1]
    return pl.pallas_call(
        functools.partial(_cross_kernel, scale=CROSS_HEAD_DIM ** -0.5),
        grid=(t // tm,),
        in_specs=[pl.BlockSpec((tm, d), lambda i: (i, 0)),
                  pl.BlockSpec((1, d), lambda i: (0, 0)),
                  pl.BlockSpec((None, d, CROSS_W), lambda i: (l, 0, 0)),
                  pl.BlockSpec((1, m, 2 * CROSS_W), lambda i: (i // per_b, 0, 0)),
                  pl.BlockSpec((None, CROSS_W, d), lambda i: (l, 0, 0))],
        out_specs=pl.BlockSpec((tm, d), lambda i: (i, 0)),
        out_shape=jax.ShapeDtypeStruct((t, d), F32),
        scratch_shapes=[pltpu.VMEM((tm, CROSS_W), BF16)],
        compiler_params=_params("parallel"),
        name="cross_attn",
    )(x, g, wq, kv, wo)


def _ffn_kernel(x_ref, g_ref, wg_ref, wu_ref, wd_ref, gf_ref, o_ref, xn_ref, acc_ref, *, final):
    j = pl.program_id(1)

    @pl.when(j == 0)
    def _():
        xn_ref[...] = _rms(x_ref[...], g_ref[...]).astype(BF16)
        acc_ref[...] = jnp.zeros_like(acc_ref)

    xn = xn_ref[...]
    gate = _dot(xn, wg_ref[...])
    up = _dot(xn, wu_ref[...])
    hid = (gate * jax.nn.sigmoid(gate) * up).astype(BF16)
    acc_ref[...] += _dot(hid, wd_ref[...])

    @pl.when(j == pl.num_programs(1) - 1)
    def _():
        y = x_ref[...] + acc_ref[...]
        o_ref[...] = _rms(y, gf_ref[...]) if final else y


def _ffn(x, g, wg, wu, wd, g_final, final, l, tm=512, th=512):
    t, d = x.shape
    hdim = wg.shape[2]
    return pl.pallas_call(
        functools.partial(_ffn_kernel, final=final),
        grid=(t // tm, hdim // th),
        in_specs=[pl.BlockSpec((tm, d), lambda i, j: (i, 0)),
                  pl.BlockSpec((1, d), lambda i, j: (0, 0)),
                  pl.BlockSpec((None, d, th), lambda i, j: (l, 0, j)),
                  pl.BlockSpec((None, d, th), lambda i, j: (l, 0, j)),
                  pl.BlockSpec((None, th, d), lambda i, j: (l, j, 0)),
                  pl.BlockSpec((1, d), lambda i, j: (0, 0))],
        out_specs=pl.BlockSpec((tm, d), lambda i, j: (i, 0)),
        out_shape=jax.ShapeDtypeStruct((t, d), F32),
        scratch_shapes=[pltpu.VMEM((tm, d), BF16), pltpu.VMEM((tm, d), F32)],
        compiler_params=_params("parallel", "arbitrary"),
        name="swiglu",
    )(x, g, wg, wu, wd, g_final)


def _rope_tables(positions):
    pos = positions.astype(F32)[..., None]
    b, s = positions.shape

    def table(rot_dim, period):
        freqs = ROPE_THETA ** (-jnp.arange(0, rot_dim, 2, dtype=F32) / rot_dim)
        ang = pos * freqs
        ones = jnp.ones((b, s, period - rot_dim), F32)
        cos = jnp.concatenate([jnp.cos(ang), jnp.cos(ang), ones], axis=-1)
        sin = jnp.concatenate([jnp.sin(ang), jnp.sin(ang), 0.0 * ones], axis=-1)
        reps = LANES // period
        return jnp.tile(cos, (1, 1, reps)), jnp.tile(sin, (1, 1, reps))

    return table(MLA_ROPE, LANES), table(DIFF_ROT, DIFF_D)


def _split_w_in(w):
    sizes = (GROUP_W, GROUP_W, GROUP_W, GROUP_HEADS, MLA_Q_RANK, MLA_KV_RANK, MLA_ROPE,
             2 * GROUP_W, GROUP_W, GROUP_W, GROUP_W)
    offs = [0]
    for sz in sizes:
        offs.append(offs[-1] + sz)
    fq, fk, fv, ff, cq, ckv, kr, uv, dq, dk, dv = (w[..., offs[i]:offs[i + 1]] for i in range(11))
    lead = w.shape[:-1]
    pad = jnp.zeros(lead + (P_W - P_MLA - MLA_Q_RANK - MLA_KV_RANK - MLA_ROPE,), w.dtype)
    main = jnp.concatenate([fq, fk, fv, dq, dk, dv, uv, cq, ckv, kr, pad], axis=-1).astype(BF16)
    wf = jnp.concatenate([ff, jnp.zeros(lead + (LANES - GROUP_HEADS,), w.dtype)], axis=-1).astype(BF16)
    return main, wf


def _pad_heads(w, used, width):
    nl, k, _ = w.shape
    w = w.reshape(nl, k, GROUP_HEADS, used)
    w = jnp.pad(w, ((0, 0), (0, 0), (0, 0), (0, width - used)))
    return w.reshape(nl, k, GROUP_HEADS * width)


def kernel(x, mem, positions, g_mix, w_in, b_f, g_cq, g_ckv, w_uq, w_ukv, sgu_ln_g, sgu_ln_b, w_s,
           b_s, lam_q1, lam_k1, lam_q2, lam_k2, g_diff, w_o, g_mem, g_cross, w_cq, w_ck, w_cv, w_co,
           g_ffn, w_gate, w_up, w_down, g_final):
    b, s, d = x.shape
    depth = w_in.shape[0]
    t = b * s
    (cos_m, sin_m), (cos_d, sin_d) = _rope_tables(positions)
    xf = x.reshape(t, d)
    memf = mem.reshape(b * mem.shape[1], d)
    row = lambda a: a.reshape(1, -1)

    w_main, w_f = _split_w_in(w_in)
    w_uq_p = _pad_heads(w_uq, MLA_NOPE + MLA_ROPE, MLA_QK_PAD).astype(BF16)
    w_ukv_b = w_ukv.astype(BF16)
    w_o_b, w_cq_b, w_ck_b, w_cv_b, w_co_b, w_gate_b, w_up_b, w_down_b = (
        _cast_bf16(w) for w in (w_o, w_cq, w_ck, w_cv, w_co, w_gate, w_up, w_down))
    b_st = jnp.swapaxes(b_s, 1, 2)
    col_scale = jnp.ones((1, P_W), F32).at[:, P_FOX:P_FOX + GROUP_W].set(HEAD_DIM ** -0.5 * LOG2E)

    for l in range(depth):
        lam_init = 0.8 - 0.6 * math.exp(-0.3 * l)
        p, fb = _in_proj(xf, row(g_mix[l]), w_main, w_f, col_scale, l)
        p3 = p.reshape(b, s, P_W)

        bf_pad = jnp.pad(b_f[l], (0, LANES - GROUP_HEADS)).reshape(1, LANES)
        fcol, frow = _fox_gate(fb.reshape(b, s, LANES), bf_pad)
        y_a = _fox_attn(p3, fcol, frow)

        qm, km, vm = _mla_prep(p3, row(g_cq[l]), row(g_ckv[l]), w_uq_p, w_ukv_b, cos_m, sin_m, l)
        y_b = _mla_attn(qm, km, vm)

        y_c = _sgu(p, row(sgu_ln_g[l]), row(sgu_ln_b[l]), w_s, b_st, l)

        qr, kr = _diff_prep(p3, cos_d, sin_d)
        lamv = jnp.stack([lam_q1[l], lam_k1[l], lam_q2[l], lam_k2[l]])
        y_d = _diff_attn(qr, kr, p3, lamv, row(g_diff[l]), lam_init)

        ys = [y.reshape(t, GROUP_W) for y in (y_a, y_b, y_c, y_d)]
        xf = _out_proj(xf, ys, w_o_b, l)

        kv = _mem_kv(memf, row(g_mem), w_ck_b, w_cv_b, l).reshape(b, mem.shape[1], 2 * CROSS_W)
        xf = _cross(xf, row(g_cross[l]), w_cq_b, kv, w_co_b, s, l)

        xf = _ffn(xf, row(g_ffn[l]), w_gate_b, w_up_b, w_down_b, row(g_final),
                  final=(l == depth - 1), l=l)
    return xf.reshape(b, s, d)
```

```python
import functools
import math

import jax
import jax.numpy as jnp
from jax import lax
from jax.experimental import pallas as pl
from jax.experimental.pallas import tpu as pltpu

F32 = jnp.float32
BF16 = jnp.bfloat16

HEAD_DIM = 128
GROUP_HEADS = 4
GROUP_W = GROUP_HEADS * HEAD_DIM
MLA_Q_RANK = 384
MLA_KV_RANK = 256
MLA_NOPE = 128
MLA_ROPE = 64
MLA_V = 128
MLA_QK_PAD = 256
SGU_CHUNK = 128
DIFF_D = HEAD_DIM // 2
DIFF_ROT = DIFF_D // 4
CROSS_HEADS = 4
CROSS_HEAD_DIM = 128
CROSS_W = CROSS_HEADS * CROSS_HEAD_DIM
ROPE_THETA = 500000.0
EPS = 1e-6
LOG2E = math.log2(math.e)
LANES = 128

P_FOX = 0
P_DIFF = 3 * GROUP_W
P_SGU = 6 * GROUP_W
P_MLA = 8 * GROUP_W
P_W = 10 * GROUP_W
MLA_BLOCK_W = 2 * GROUP_W

VMEM_LIMIT = 52 * 2**20


def _params(*sem):
    return pltpu.CompilerParams(dimension_semantics=sem, vmem_limit_bytes=VMEM_LIMIT)


def _rms(x, g):
    return x * lax.rsqrt(jnp.mean(x * x, axis=-1, keepdims=True) + EPS) * g


def _dot(a, b):
    return jnp.dot(a, b, preferred_element_type=F32)


def _dot_nt(a, b):
    return lax.dot_general(a, b, (((1,), (1,)), ((), ())), preferred_element_type=F32)


def _cast_kernel(w_ref, o_ref):
    o_ref[...] = w_ref[...].astype(o_ref.dtype)


def _cast_bf16(w, block_bytes=4 * 2**20):
    nl, k, n = w.shape
    tk = k
    while tk * n * 4 > block_bytes and tk % 16 == 0:
        tk //= 2
    return pl.pallas_call(
        _cast_kernel,
        grid=(nl, k // tk),
        in_specs=[pl.BlockSpec((1, tk, n), lambda l, i: (l, i, 0))],
        out_specs=pl.BlockSpec((1, tk, n), lambda l, i: (l, i, 0)),
        out_shape=jax.ShapeDtypeStruct(w.shape, BF16),
        compiler_params=_params("parallel", "parallel"),
        name="cast_bf16",
    )(w)


def _in_proj_kernel(x_ref, g_ref, w_ref, wf_ref, cs_ref, p_ref, f_ref, xn_ref):
    @pl.when(pl.program_id(1) == 0)
    def _():
        xb = _rms(x_ref[...], g_ref[...]).astype(BF16)
        xn_ref[...] = xb
        f_ref[...] = _dot(xb, wf_ref[...])

    p_ref[...] = (_dot(xn_ref[...], w_ref[...]) * cs_ref[...]).astype(p_ref.dtype)


def _in_proj(x, g, w, wf, col_scale, l, tm=1024, tn=1024):
    t, d = x.shape
    n = w.shape[2]
    return pl.pallas_call(
        _in_proj_kernel,
        grid=(t // tm, n // tn),
        in_specs=[pl.BlockSpec((tm, d), lambda i, j: (i, 0)),
                  pl.BlockSpec((1, d), lambda i, j: (0, 0)),
                  pl.BlockSpec((None, d, tn), lambda i, j: (l, 0, j)),
                  pl.BlockSpec((None, d, LANES), lambda i, j: (l, 0, 0)),
                  pl.BlockSpec((1, tn), lambda i, j: (0, j))],
        out_specs=[pl.BlockSpec((tm, tn), lambda i, j: (i, j)),
                   pl.BlockSpec((tm, LANES), lambda i, j: (i, 0))],
        out_shape=[jax.ShapeDtypeStruct((t, n), BF16),
                   jax.ShapeDtypeStruct((t, LANES), F32)],
        scratch_shapes=[pltpu.VMEM((tm, d), BF16)],
        compiler_params=_params("parallel", "arbitrary"),
        name="in_proj",
    )(x, g, w, wf, col_scale)


def _mem_kv_kernel(x_ref, g_ref, wk_ref, wv_ref, o_ref):
    xb = _rms(x_ref[...], g_ref[...]).astype(BF16)
    o_ref[:, 0:CROSS_W] = _dot(xb, wk_ref[...]).astype(o_ref.dtype)
    o_ref[:, CROSS_W:2 * CROSS_W] = _dot(xb, wv_ref[...]).astype(o_ref.dtype)


def _mem_kv(x, g, wk, wv, l, tm=512):
    t, d = x.shape
    wspec = pl.BlockSpec((None, d, CROSS_W), lambda i: (l, 0, 0))
    return pl.pallas_call(
        _mem_kv_kernel,
        grid=(t // tm,),
        in_specs=[pl.BlockSpec((tm, d), lambda i: (i, 0)),
                  pl.BlockSpec((1, d), lambda i: (0, 0)), wspec, wspec],
        out_specs=pl.BlockSpec((tm, 2 * CROSS_W), lambda i: (i, 0)),
        out_shape=jax.ShapeDtypeStruct((t, 2 * CROSS_W), BF16),
        compiler_params=_params("parallel"),
        name="mem_kv_proj",
    )(x, g, wk, wv)


def _fox_gate_kernel(f_ref, b_ref, col_ref, row_ref):
    x = f_ref[0] + b_ref[...]
    c = jnp.minimum(x, 0.0) - jnp.log1p(jnp.exp(-jnp.abs(x)))
    s = x.shape[0]
    ridx = lax.broadcasted_iota(jnp.int32, x.shape, 0)
    shift = 1
    while shift < s:
        c = c + jnp.where(ridx >= shift, pltpu.roll(c, shift, 0), 0.0)
        shift *= 2
    c = c * LOG2E
    col_ref[0] = c
    row_ref[0] = c.T[0:8, :]


def _fox_gate(fb, b_f):
    b, s, _ = fb.shape
    return pl.pallas_call(
        _fox_gate_kernel,
        grid=(b,),
        in_specs=[pl.BlockSpec((1, s, LANES), lambda i: (i, 0, 0)),
                  pl.BlockSpec((1, LANES), lambda i: (0, 0))],
        out_specs=[pl.BlockSpec((1, s, LANES), lambda i: (i, 0, 0)),
                   pl.BlockSpec((1, 8, s), lambda i: (i, 0, 0))],
        out_shape=[jax.ShapeDtypeStruct((b, s, LANES), F32),
                   jax.ShapeDtypeStruct((b, 8, s), F32)],
        compiler_params=_params("parallel"),
        name="fox_gate",
    )(fb, b_f)


def _flash(qs, k_ref, v_ref, st_ref, dk, dv, qi, tq, fqs=None, frow_ref=None):
    nh = len(qs)
    r = qs[0].shape[0]
    assert dv == LANES and tq % LANES == 0
    groups = tq // LANES
    if fqs is not None:
        fqs = [jnp.broadcast_to(f, (r, LANES)) for f in fqs]

    def block(j, first):
        start = pl.multiple_of(j * tq, tq)
        scores = []
        for h in range(nh):
            k = k_ref[0, pl.ds(start, tq), h * dk:(h + 1) * dk]
            s = _dot_nt(qs[h], k)
            if frow_ref is not None:
                s = s - frow_ref[0, h:h + 1, pl.ds(start, tq)]
            if first:
                row = lax.broadcasted_iota(jnp.int32, (r, tq), 0) & (tq - 1)
                col = lax.broadcasted_iota(jnp.int32, (r, tq), 1)
                s = jnp.where(col <= row, s, -jnp.inf)
            scores.append(s)
        for h in range(nh):
            mz = jnp.broadcast_to(jnp.max(scores[h], axis=-1, keepdims=True), (r, LANES))
            if fqs is not None:
                mz = mz + fqs[h]
            if first:
                m_new = mz
            else:
                m_old = st_ref[0, h]
                m_new = jnp.maximum(m_old, mz)
                alpha = jnp.exp2(m_old - m_new)
            shift = (fqs[h] - m_new) if fqs is not None else -m_new
            p = jnp.exp2(scores[h] + jnp.concatenate([shift] * groups, axis=1))
            psum = p[:, 0:LANES]
            for g in range(1, groups):
                psum = psum + p[:, g * LANES:(g + 1) * LANES]
            pv = _dot(p.astype(BF16), v_ref[0, pl.ds(start, tq), h * dv:(h + 1) * dv])
            st_ref[0, h] = m_new
            if first:
                st_ref[1, h] = psum
                st_ref[2, h] = pv
            else:
                st_ref[1, h] = alpha * st_ref[1, h] + psum
                st_ref[2, h] = alpha * st_ref[2, h] + pv

    block(qi, True)

    def body(j, c):
        block(j, False)
        return c

    lax.fori_loop(0, qi, body, 0)
    return [(st_ref[2, h], jnp.sum(st_ref[1, h], axis=-1, keepdims=True)) for h in range(nh)]


def _flash_state(rows):
    return pltpu.VMEM((3, GROUP_HEADS, rows, LANES), F32)


def _fox_attn_kernel(q_ref, k_ref, v_ref, fcol_ref, frow_ref, o_ref, st_ref, *, tq):
    qi = pl.program_id(1)
    qs = [q_ref[0, :, h * HEAD_DIM:(h + 1) * HEAD_DIM] for h in range(GROUP_HEADS)]
    fqs = [fcol_ref[0, :, h:h + 1] for h in range(GROUP_HEADS)]
    res = _flash(qs, k_ref, v_ref, st_ref, HEAD_DIM, HEAD_DIM, qi, tq, fqs, frow_ref)
    for h, (acc, l) in enumerate(res):
        o_ref[0, :, h * HEAD_DIM:(h + 1) * HEAD_DIM] = (acc / l).astype(o_ref.dtype)


def _fox_attn(p3, fcol, frow, tq=512):
    b, s, _ = p3.shape
    w = GROUP_W
    c0 = P_FOX // w
    return pl.pallas_call(
        functools.partial(_fox_attn_kernel, tq=tq),
        grid=(b, s // tq),
        in_specs=[pl.BlockSpec((1, tq, w), lambda i, j: (i, j, c0)),
                  pl.BlockSpec((1, s, w), lambda i, j: (i, 0, c0 + 1)),
                  pl.BlockSpec((1, s, w), lambda i, j: (i, 0, c0 + 2)),
                  pl.BlockSpec((1, tq, LANES), lambda i, j: (i, j, 0)),
                  pl.BlockSpec((1, 8, s), lambda i, j: (i, 0, 0))],
        out_specs=pl.BlockSpec((1, tq, w), lambda i, j: (i, j, 0)),
        out_shape=jax.ShapeDtypeStruct((b, s, w), BF16),
        scratch_shapes=[_flash_state(tq)],
        compiler_params=_params("parallel", "arbitrary"),
        name="fox_attn",
    )(p3, p3, p3, fcol, frow)


def _rope_pairs(x, cos, sin, half, period):
    lane = lax.broadcasted_iota(jnp.int32, x.shape, x.ndim - 1)
    first = (lane & (period - 1)) < half
    rot = jnp.where(first, -pltpu.roll(x, LANES - half, x.ndim - 1), pltpu.roll(x, half, x.ndim - 1))
    return x * cos + rot * sin


def _mla_prep_kernel(p_ref, gq_ref, gkv_ref, wq_ref, wkv_ref, cos_ref, sin_ref,
                     q_ref, k_ref, v_ref, *, qscale):
    blk = p_ref[0].astype(F32)
    cos = cos_ref[0]
    sin = sin_ref[0]
    cq = blk[:, 0:MLA_Q_RANK]
    ckv = blk[:, MLA_Q_RANK:MLA_Q_RANK + MLA_KV_RANK]
    kr = blk[:, MLA_Q_RANK + MLA_KV_RANK:MLA_Q_RANK + MLA_KV_RANK + LANES]
    q = _dot(_rms(cq, gq_ref[...]).astype(BF16), wq_ref[...])
    kv = _dot(_rms(ckv, gkv_ref[...]).astype(BF16), wkv_ref[...])
    kr = _rope_pairs(kr, cos, sin, MLA_ROPE // 2, LANES).astype(BF16)
    for h in range(GROUP_HEADS):
        o = h * MLA_QK_PAD
        q_ref[0, :, o:o + LANES] = (q[:, o:o + LANES] * qscale).astype(BF16)
        q_ref[0, :, o + LANES:o + 2 * LANES] = (_rope_pairs(
            q[:, o + LANES:o + 2 * LANES], cos, sin, MLA_ROPE // 2, LANES) * qscale).astype(BF16)
        k_ref[0, :, o:o + LANES] = kv[:, o:o + LANES].astype(BF16)
        k_ref[0, :, o + LANES:o + 2 * LANES] = kr
        v_ref[0, :, h * MLA_V:(h + 1) * MLA_V] = kv[:, o + LANES:o + 2 * LANES].astype(BF16)


def _mla_prep(p3, g_cq, g_ckv, w_uq, w_ukv, cos, sin, l, tm=512):
    b, s, _ = p3.shape
    c0 = P_MLA // MLA_BLOCK_W
    hq = GROUP_HEADS * MLA_QK_PAD
    qscale = (MLA_NOPE + MLA_ROPE) ** -0.5 * LOG2E
    return pl.pallas_call(
        functools.partial(_mla_prep_kernel, qscale=qscale),
        grid=(b, s // tm),
        in_specs=[pl.BlockSpec((1, tm, MLA_BLOCK_W), lambda i, j: (i, j, c0)),
                  pl.BlockSpec((1, MLA_Q_RANK), lambda i, j: (0, 0)),
                  pl.BlockSpec((1, MLA_KV_RANK), lambda i, j: (0, 0)),
                  pl.BlockSpec((None, MLA_Q_RANK, hq), lambda i, j: (l, 0, 0)),
                  pl.BlockSpec((None, MLA_KV_RANK, hq), lambda i, j: (l, 0, 0)),
                  pl.BlockSpec((1, tm, LANES), lambda i, j: (i, j, 0)),
                  pl.BlockSpec((1, tm, LANES), lambda i, j: (i, j, 0))],
        out_specs=[pl.BlockSpec((1, tm, hq), lambda i, j: (i, j, 0)),
                   pl.BlockSpec((1, tm, hq), lambda i, j: (i, j, 0)),
                   pl.BlockSpec((1, tm, GROUP_W), lambda i, j: (i, j, 0))],
        out_shape=[jax.ShapeDtypeStruct((b, s, hq), BF16),
                   jax.ShapeDtypeStruct((b, s, hq), BF16),
                   jax.ShapeDtypeStruct((b, s, GROUP_W), BF16)],
        compiler_params=_params("parallel", "parallel"),
        name="mla_prep",
    )(p3, g_cq, g_ckv, w_uq, w_ukv, cos, sin)


def _mla_attn_kernel(q_ref, k_ref, v_ref, o_ref, st_ref, *, tq):
    qi = pl.program_id(1)
    qs = [q_ref[0, :, h * MLA_QK_PAD:(h + 1) * MLA_QK_PAD] for h in range(GROUP_HEADS)]
    res = _flash(qs, k_ref, v_ref, st_ref, MLA_QK_PAD, MLA_V, qi, tq)
    for h, (acc, l) in enumerate(res):
        o_ref[0, :, h * MLA_V:(h + 1) * MLA_V] = (acc / l).astype(o_ref.dtype)


def _mla_attn(qm, km, vm, tq=512):
    b, s, hq = qm.shape
    return pl.pallas_call(
        functools.partial(_mla_attn_kernel, tq=tq),
        grid=(b, s // tq),
        in_specs=[pl.BlockSpec((1, tq, hq), lambda i, j: (i, j, 0)),
                  pl.BlockSpec((1, s, hq), lambda i, j: (i, 0, 0)),
                  pl.BlockSpec((1, s, GROUP_W), lambda i, j: (i, 0, 0))],
        out_specs=pl.BlockSpec((1, tq, GROUP_W), lambda i, j: (i, j, 0)),
        out_shape=jax.ShapeDtypeStruct((b, s, GROUP_W), BF16),
        scratch_shapes=[_flash_state(tq)],
        compiler_params=_params("parallel", "arbitrary"),
        name="mla_attn",
    )(qm, km, vm)


def _diff_prep_kernel(q_ref, k_ref, cos_ref, sin_ref, qo_ref, ko_ref, *, qscale):
    cos = cos_ref[0]
    sin = sin_ref[0]
    for src, dst, scale in ((q_ref, qo_ref, qscale), (k_ref, ko_ref, None)):
        for c in range(GROUP_W // LANES):
            x = src[0, :, c * LANES:(c + 1) * LANES].astype(F32)
            y = _rope_pairs(x, cos, sin, DIFF_ROT // 2, DIFF_D)
            if scale is not None:
                y = y * scale
            dst[0, :, c * LANES:(c + 1) * LANES] = y.astype(BF16)


def _diff_prep(p3, cos, sin, tm=1024):
    b, s, _ = p3.shape
    c0 = P_DIFF // GROUP_W
    return pl.pallas_call(
        functools.partial(_diff_prep_kernel, qscale=DIFF_D ** -0.5 * LOG2E),
        grid=(b, s // tm),
        in_specs=[pl.BlockSpec((1, tm, GROUP_W), lambda i, j: (i, j, c0)),
                  pl.BlockSpec((1, tm, GROUP_W), lambda i, j: (i, j, c0 + 1)),
                  pl.BlockSpec((1, tm, LANES), lambda i, j: (i, j, 0)),
                  pl.BlockSpec((1, tm, LANES), lambda i, j: (i, j, 0))],
        out_specs=[pl.BlockSpec((1, tm, GROUP_W), lambda i, j: (i, j, 0)),
                   pl.BlockSpec((1, tm, GROUP_W), lambda i, j: (i, j, 0))],
        out_shape=[jax.ShapeDtypeStruct((b, s, GROUP_W), BF16),
                   jax.ShapeDtypeStruct((b, s, GROUP_W), BF16)],
        compiler_params=_params("parallel", "parallel"),
        name="diff_prep",
    )(p3, p3, cos, sin)


def _diff_attn_kernel(q_ref, k_ref, v_ref, lam_ref, g_ref, o_ref, st_ref, *, tq, lam_init):
    qi = pl.program_id(1)
    lv = lam_ref[...]
    lam = (jnp.exp(jnp.sum(lv[0:1] * lv[1:2], axis=-1, keepdims=True))
           - jnp.exp(jnp.sum(lv[2:3] * lv[3:4], axis=-1, keepdims=True)) + lam_init)
    lane = lax.broadcasted_iota(jnp.int32, (tq, HEAD_DIM), 1)
    qs = []
    for h in range(GROUP_HEADS):
        q = q_ref[0, :, h * HEAD_DIM:(h + 1) * HEAD_DIM]
        zero = jnp.zeros_like(q)
        qs.append(jnp.concatenate([jnp.where(lane < DIFF_D, q, zero),
                                   jnp.where(lane >= DIFF_D, q, zero)], axis=0))
    res = _flash(qs, k_ref, v_ref, st_ref, HEAD_DIM, HEAD_DIM, qi, tq)
    for h, (acc, l) in enumerate(res):
        o = acc / l
        o = o[0:tq] - lam * o[tq:2 * tq]
        o = _rms(o, g_ref[...]) * (1.0 - lam_init)
        o_ref[0, :, h * HEAD_DIM:(h + 1) * HEAD_DIM] = o.astype(o_ref.dtype)


def _diff_attn(qr, kr, p3, lamv, g_diff, lam_init, tq=512):
    b, s, w = qr.shape
    cv = P_DIFF // w + 2
    return pl.pallas_call(
        functools.partial(_diff_attn_kernel, tq=tq, lam_init=lam_init),
        grid=(b, s // tq),
        in_specs=[pl.BlockSpec((1, tq, w), lambda i, j: (i, j, 0)),
                  pl.BlockSpec((1, s, w), lambda i, j: (i, 0, 0)),
                  pl.BlockSpec((1, s, w), lambda i, j: (i, 0, cv)),
                  pl.BlockSpec((4, DIFF_D), lambda i, j: (0, 0)),
                  pl.BlockSpec((1, HEAD_DIM), lambda i, j: (0, 0))],
        out_specs=pl.BlockSpec((1, tq, w), lambda i, j: (i, j, 0)),
        out_shape=jax.ShapeDtypeStruct((b, s, w), BF16),
        scratch_shapes=[_flash_state(2 * tq)],
        compiler_params=_params("parallel", "arbitrary"),
        name="diff_attn",
    )(qr, kr, p3, lamv, g_diff)


def _sgu_kernel(uv_ref, lg_ref, lb_ref, ws_ref, bs_ref, o_ref, *, tm):
    z = uv_ref[...].astype(F32)
    z = 0.5 * z * (1.0 + jnp.tanh(math.sqrt(2.0 / math.pi) * (z + 0.044715 * (z * z * z))))
    u = z[:, 0:GROUP_W]
    v = z[:, GROUP_W:2 * GROUP_W]
    xc = v - jnp.mean(v, axis=-1, keepdims=True)
    vn = xc * lax.rsqrt(jnp.mean(xc * xc, axis=-1, keepdims=True) + EPS) * lg_ref[...] + lb_ref[...]
    vn = vn.astype(BF16)
    t = SGU_CHUNK
    tril = (lax.broadcasted_iota(jnp.int32, (t, t), 1) <= lax.broadcasted_iota(jnp.int32, (t, t), 0))
    for g in range(GROUP_HEADS):
        w = jnp.where(tril, ws_ref[g], 0.0).astype(BF16)
        bias = bs_ref[:, g:g + 1]
        for c in range(tm // t):
            rows = slice(c * t, (c + 1) * t)
            cols = slice(g * HEAD_DIM, (g + 1) * HEAD_DIM)
            mixed = _dot(w, vn[rows, cols]) + bias
            o_ref[rows, cols] = (u[rows, cols] * mixed).astype(o_ref.dtype)


def _sgu(p, ln_g, ln_b, w_s, b_st, l, tm=512):
    t = p.shape[0]
    c0 = P_SGU // (2 * GROUP_W)
    return pl.pallas_call(
        functools.partial(_sgu_kernel, tm=tm),
        grid=(t // tm,),
        in_specs=[pl.BlockSpec((tm, 2 * GROUP_W), lambda i: (i, c0)),
                  pl.BlockSpec((1, GROUP_W), lambda i: (0, 0)),
                  pl.BlockSpec((1, GROUP_W), lambda i: (0, 0)),
                  pl.BlockSpec((None, GROUP_HEADS, SGU_CHUNK, SGU_CHUNK), lambda i: (l, 0, 0, 0)),
                  pl.BlockSpec((None, SGU_CHUNK, GROUP_HEADS), lambda i: (l, 0, 0))],
        out_specs=pl.BlockSpec((tm, GROUP_W), lambda i: (i, 0)),
        out_shape=jax.ShapeDtypeStruct((t, GROUP_W), BF16),
        compiler_params=_params("parallel"),
        name="sgu",
    )(p, ln_g, ln_b, w_s, b_st)


def _out_proj_kernel(x_ref, ya_ref, yb_ref, yc_ref, yd_ref, w_ref, o_ref):
    acc = x_ref[...]
    for m, y_ref in enumerate((ya_ref, yb_ref, yc_ref, yd_ref)):
        acc = acc + _dot(y_ref[...], w_ref[m * GROUP_W:(m + 1) * GROUP_W, :])
    o_ref[...] = acc


def _out_proj(x, ys, w, l, tm=1024, tn=1024):
    t, d = x.shape
    yspec = pl.BlockSpec((tm, GROUP_W), lambda i, j: (i, 0))
    return pl.pallas_call(
        _out_proj_kernel,
        grid=(t // tm, d // tn),
        in_specs=[pl.BlockSpec((tm, tn), lambda i, j: (i, j)), yspec, yspec, yspec, yspec,
                  pl.BlockSpec((None, w.shape[1], tn), lambda i, j: (l, 0, j))],
        out_specs=pl.BlockSpec((tm, tn), lambda i, j: (i, j)),
        out_shape=jax.ShapeDtypeStruct((t, d), F32),
        compiler_params=_params("parallel", "parallel"),
        name="out_proj",
    )(x, *ys, w)


def _cross_kernel(x_ref, g_ref, wq_ref, kv_ref, wo_ref, o_ref, oh_ref, *, scale):
    x = x_ref[...]
    q = _dot(_rms(x, g_ref[...]).astype(BF16), wq_ref[...]).astype(BF16)
    for h in range(CROSS_HEADS):
        cols = slice(h * CROSS_HEAD_DIM, (h + 1) * CROSS_HEAD_DIM)
        k = kv_ref[0, :, cols]
        v = kv_ref[0, :, CROSS_W + h * CROSS_HEAD_DIM:CROSS_W + (h + 1) * CROSS_HEAD_DIM]
        s = _dot_nt(q[:, cols], k) * scale
        p = jnp.exp(s - jnp.max(s, axis=-1, keepdims=True))
        l = jnp.sum(p, axis=-1, keepdims=True)
        oh_ref[:, cols] = (_dot(p.astype(BF16), v) / l).astype(BF16)
    o_ref[...] = x + _dot(oh_ref[...], wo_ref[...])


def _cross(x, g, wq, kv, wo, seq, l, tm=512):
    t, d = x.shape
    per_b = seq // tm
    m = kv.shape[1]
    return pl.pallas_call(
        functools.partial(_cross_kernel, scale=CROSS_HEAD_DIM ** -0.5),
        grid=(t // tm,),
        in_specs=[pl.BlockSpec((tm, d), lambda i: (i, 0)),
                  pl.BlockSpec((1, d), lambda i: (0, 0)),
                  pl.BlockSpec((None, d, CROSS_W), lambda i: (l, 0, 0)),
                  pl.BlockSpec((1, m, 2 * CROSS_W), lambda i: (i // per_b, 0, 0)),
                  pl.BlockSpec((None, CROSS_W, d), lambda i: (l, 0, 0))],
        out_specs=pl.BlockSpec((tm, d), lambda i: (i, 0)),
        out_shape=jax.ShapeDtypeStruct((t, d), F32),
        scratch_shapes=[pltpu.VMEM((tm, CROSS_W), BF16)],
        compiler_params=_params("parallel"),
        name="cross_attn",
    )(x, g, wq, kv, wo)


def _ffn_kernel(x_ref, g_ref, wg_ref, wu_ref, wd_ref, gf_ref, o_ref, xn_ref, acc_ref, *, final):
    j = pl.program_id(1)

    @pl.when(j == 0)
    def _():
        xn_ref[...] = _rms(x_ref[...], g_ref[...]).astype(BF16)
        acc_ref[...] = jnp.zeros_like(acc_ref)

    xn = xn_ref[...]
    gate = _dot(xn, wg_ref[...])
    up = _dot(xn, wu_ref[...])
    hid = (gate * jax.nn.sigmoid(gate) * up).astype(BF16)
    acc_ref[...] += _dot(hid, wd_ref[...])

    @pl.when(j == pl.num_programs(1) - 1)
    def _():
        y = x_ref[...] + acc_ref[...]
        o_ref[...] = _rms(y, gf_ref[...]) if final else y


def _ffn(x, g, wg, wu, wd, g_final, final, l, tm=512, th=512):
    t, d = x.shape
    hdim = wg.shape[2]
    return pl.pallas_call(
        functools.partial(_ffn_kernel, final=final),
        grid=(t // tm, hdim // th),
        in_specs=[pl.BlockSpec((tm, d), lambda i, j: (i, 0)),
                  pl.BlockSpec((1, d), lambda i, j: (0, 0)),
                  pl.BlockSpec((None, d, th), lambda i, j: (l, 0, j)),
                  pl.BlockSpec((None, d, th), lambda i, j: (l, 0, j)),
                  pl.BlockSpec((None, th, d), lambda i, j: (l, j, 0)),
                  pl.BlockSpec((1, d), lambda i, j: (0, 0))],
        out_specs=pl.BlockSpec((tm, d), lambda i, j: (i, 0)),
        out_shape=jax.ShapeDtypeStruct((t, d), F32),
        scratch_shapes=[pltpu.VMEM((tm, d), BF16), pltpu.VMEM((tm, d), F32)],
        compiler_params=_params("parallel", "arbitrary"),
        name="swiglu",
    )(x, g, wg, wu, wd, g_final)


def _rope_tables(positions):
    pos = positions.astype(F32)[..., None]
    b, s = positions.shape

    def table(rot_dim, period):
        freqs = ROPE_THETA ** (-jnp.arange(0, rot_dim, 2, dtype=F32) / rot_dim)
        ang = pos * freqs
        ones = jnp.ones((b, s, period - rot_dim), F32)
        cos = jnp.concatenate([jnp.cos(ang), jnp.cos(ang), ones], axis=-1)
        sin = jnp.concatenate([jnp.sin(ang), jnp.sin(ang), 0.0 * ones], axis=-1)
        reps = LANES // period
        return jnp.tile(cos, (1, 1, reps)), jnp.tile(sin, (1, 1, reps))

    return table(MLA_ROPE, LANES), table(DIFF_ROT, DIFF_D)


def _split_w_in(w):
    sizes = (GROUP_W, GROUP_W, GROUP_W, GROUP_HEADS, MLA_Q_RANK, MLA_KV_RANK, MLA_ROPE,
             2 * GROUP_W, GROUP_W, GROUP_W, GROUP_W)
    offs = [0]
    for sz in sizes:
        offs.append(offs[-1] + sz)
    fq, fk, fv, ff, cq, ckv, kr, uv, dq, dk, dv = (w[..., offs[i]:offs[i + 1]] for i in range(11))
    lead = w.shape[:-1]
    pad = jnp.zeros(lead + (P_W - P_MLA - MLA_Q_RANK - MLA_KV_RANK - MLA_ROPE,), w.dtype)
    main = jnp.concatenate([fq, fk, fv, dq, dk, dv, uv, cq, ckv, kr, pad], axis=-1).astype(BF16)
    wf = jnp.concatenate([ff, jnp.zeros(lead + (LANES - GROUP_HEADS,), w.dtype)], axis=-1).astype(BF16)
    return main, wf


def _pad_heads(w, used, width):
    nl, k, _ = w.shape
    w = w.reshape(nl, k, GROUP_HEADS, used)
    w = jnp.pad(w, ((0, 0), (0, 0), (0, 0), (0, width - used)))
    return w.reshape(nl, k, GROUP_HEADS * width)


def kernel(x, mem, positions, g_mix, w_in, b_f, g_cq, g_ckv, w_uq, w_ukv, sgu_ln_g, sgu_ln_b, w_s,
           b_s, lam_q1, lam_k1, lam_q2, lam_k2, g_diff, w_o, g_mem, g_cross, w_cq, w_ck, w_cv, w_co,
           g_ffn, w_gate, w_up, w_down, g_final):
    b, s, d = x.shape
    depth = w_in.shape[0]
    t = b * s
    (cos_m, sin_m), (cos_d, sin_d) = _rope_tables(positions)
    xf = x.reshape(t, d)
    memf = mem.reshape(b * mem.shape[1], d)
    row = lambda a: a.reshape(1, -1)

    w_main, w_f = _split_w_in(w_in)
    w_uq_p = _pad_heads(w_uq, MLA_NOPE + MLA_ROPE, MLA_QK_PAD).astype(BF16)
    w_ukv_b = w_ukv.astype(BF16)
    w_o_b, w_cq_b, w_ck_b, w_cv_b, w_co_b, w_gate_b, w_up_b, w_down_b = (
        _cast_bf16(w) for w in (w_o, w_cq, w_ck, w_cv, w_co, w_gate, w_up, w_down))
    b_st = jnp.swapaxes(b_s, 1, 2)
    col_scale = jnp.ones((1, P_W), F32).at[:, P_FOX:P_FOX + GROUP_W].set(HEAD_DIM ** -0.5 * LOG2E)

    for l in range(depth):
        lam_init = 0.8 - 0.6 * math.exp(-0.3 * l)
        p, fb = _in_proj(xf, row(g_mix[l]), w_main, w_f, col_scale, l)
        p3 = p.reshape(b, s, P_W)

        bf_pad = jnp.pad(b_f[l], (0, LANES - GROUP_HEADS)).reshape(1, LANES)
        fcol, frow = _fox_gate(fb.reshape(b, s, LANES), bf_pad)
        y_a = _fox_attn(p3, fcol, frow)

        qm, km, vm = _mla_prep(p3, row(g_cq[l]), row(g_ckv[l]), w_uq_p, w_ukv_b, cos_m, sin_m, l)
        y_b = _mla_attn(qm, km, vm)

        y_c = _sgu(p, row(sgu_ln_g[l]), row(sgu_ln_b[l]), w_s, b_st, l)

        qr, kr = _diff_prep(p3, cos_d, sin_d)
        lamv = jnp.stack([lam_q1[l], lam_k1[l], lam_q2[l], lam_k2[l]])
        y_d = _diff_attn(qr, kr, p3, lamv, row(g_diff[l]), lam_init)

        ys = [y.reshape(t, GROUP_W) for y in (y_a, y_b, y_c, y_d)]
        xf = _out_proj(xf, ys, w_o_b, l)

        kv = _mem_kv(memf, row(g_mem), w_ck_b, w_cv_b, l).reshape(b, mem.shape[1], 2 * CROSS_W)
        xf = _cross(xf, row(g_cross[l]), w_cq_b, kv, w_co_b, s, l)

        xf = _ffn(xf, row(g_ffn[l]), w_gate_b, w_up_b, w_down_b, row(g_final),
                  final=(l == depth - 1), l=l)
    return xf.reshape(b, s, d)
```

```python
import functools
import math

import jax
import jax.numpy as jnp
from jax import lax
from jax.experimental import pallas as pl
from jax.experimental.pallas import tpu as pltpu

F32 = jnp.float32
BF16 = jnp.bfloat16

HEAD_DIM = 128
GROUP_HEADS = 4
GROUP_W = GROUP_HEADS * HEAD_DIM
MLA_Q_RANK = 384
MLA_KV_RANK = 256
MLA_NOPE = 128
MLA_ROPE = 64
MLA_V = 128
MLA_QK_PAD = 256
SGU_CHUNK = 128
DIFF_D = HEAD_DIM // 2
DIFF_ROT = DIFF_D // 4
CROSS_HEADS = 4
CROSS_HEAD_DIM = 128
CROSS_W = CROSS_HEADS * CROSS_HEAD_DIM
ROPE_THETA = 500000.0
EPS = 1e-6
LOG2E = math.log2(math.e)
LANES = 128

P_FOX = 0
P_DIFF = 3 * GROUP_W
P_SGU = 6 * GROUP_W
P_MLA = 8 * GROUP_W
P_W = 10 * GROUP_W
MLA_BLOCK_W = 2 * GROUP_W

VMEM_LIMIT = 52 * 2**20


def _params(*sem):
    return pltpu.CompilerParams(dimension_semantics=sem, vmem_limit_bytes=VMEM_LIMIT)


def _rms(x, g):
    return x * lax.rsqrt(jnp.mean(x * x, axis=-1, keepdims=True) + EPS) * g


def _dot(a, b):
    return jnp.dot(a, b, preferred_element_type=F32)


def _dot_nt(a, b):
    return lax.dot_general(a, b, (((1,), (1,)), ((), ())), preferred_element_type=F32)


def _cast_kernel(w_ref, o_ref):
    o_ref[...] = w_ref[...].astype(o_ref.dtype)


def _cast_bf16(w, block_bytes=4 * 2**20):
    nl, k, n = w.shape
    tk = k
    while tk * n * 4 > block_bytes and tk % 16 == 0:
        tk //= 2
    return pl.pallas_call(
        _cast_kernel,
        grid=(nl, k // tk),
        in_specs=[pl.BlockSpec((1, tk, n), lambda l, i: (l, i, 0))],
        out_specs=pl.BlockSpec((1, tk, n), lambda l, i: (l, i, 0)),
        out_shape=jax.ShapeDtypeStruct(w.shape, BF16),
        compiler_params=_params("parallel", "parallel"),
        name="cast_bf16",
    )(w)


def _in_proj_kernel(x_ref, g_ref, w_ref, wf_ref, cs_ref, p_ref, f_ref, xn_ref):
    @pl.when(pl.program_id(1) == 0)
    def _():
        xb = _rms(x_ref[...], g_ref[...]).astype(BF16)
        xn_ref[...] = xb
        f_ref[...] = _dot(xb, wf_ref[...])

    p_ref[...] = (_dot(xn_ref[...], w_ref[...]) * cs_ref[...]).astype(p_ref.dtype)


def _in_proj(x, g, w, wf, col_scale, l, tm=1024, tn=1024):
    t, d = x.shape
    n = w.shape[2]
    return pl.pallas_call(
        _in_proj_kernel,
        grid=(t // tm, n // tn),
        in_specs=[pl.BlockSpec((tm, d), lambda i, j: (i, 0)),
                  pl.BlockSpec((1, d), lambda i, j: (0, 0)),
                  pl.BlockSpec((None, d, tn), lambda i, j: (l, 0, j)),
                  pl.BlockSpec((None, d, LANES), lambda i, j: (l, 0, 0)),
                  pl.BlockSpec((1, tn), lambda i, j: (0, j))],
        out_specs=[pl.BlockSpec((tm, tn), lambda i, j: (i, j)),
                   pl.BlockSpec((tm, LANES), lambda i, j: (i, 0))],
        out_shape=[jax.ShapeDtypeStruct((t, n), BF16),
                   jax.ShapeDtypeStruct((t, LANES), F32)],
        scratch_shapes=[pltpu.VMEM((tm, d), BF16)],
        compiler_params=_params("parallel", "arbitrary"),
        name="in_proj",
    )(x, g, w, wf, col_scale)


def _mem_kv_kernel(x_ref, g_ref, wk_ref, wv_ref, o_ref):
    xb = _rms(x_ref[...], g_ref[...]).astype(BF16)
    o_ref[:, 0:CROSS_W] = _dot(xb, wk_ref[...]).astype(o_ref.dtype)
    o_ref[:, CROSS_W:2 * CROSS_W] = _dot(xb, wv_ref[...]).astype(o_ref.dtype)


def _mem_kv(x, g, wk, wv, l, tm=512):
    t, d = x.shape
    wspec = pl.BlockSpec((None, d, CROSS_W), lambda i: (l, 0, 0))
    return pl.pallas_call(
        _mem_kv_kernel,
        grid=(t // tm,),
        in_specs=[pl.BlockSpec((tm, d), lambda i: (i, 0)),
                  pl.BlockSpec((1, d), lambda i: (0, 0)), wspec, wspec],
        out_specs=pl.BlockSpec((tm, 2 * CROSS_W), lambda i: (i, 0)),
        out_shape=jax.ShapeDtypeStruct((t, 2 * CROSS_W), BF16),
        compiler_params=_params("parallel"),
        name="mem_kv_proj",
    )(x, g, wk, wv)


def _fox_gate_kernel(f_ref, b_ref, col_ref, row_ref):
    x = f_ref[0] + b_ref[...]
    c = jnp.minimum(x, 0.0) - jnp.log1p(jnp.exp(-jnp.abs(x)))
    s = x.shape[0]
    ridx = lax.broadcasted_iota(jnp.int32, x.shape, 0)
    shift = 1
    while shift < s:
        c = c + jnp.where(ridx >= shift, pltpu.roll(c, shift, 0), 0.0)
        shift *= 2
    c = c * LOG2E
    col_ref[0] = c
    row_ref[0] = c.T[0:8, :]


def _fox_gate(fb, b_f):
    b, s, _ = fb.shape
    return pl.pallas_call(
        _fox_gate_kernel,
        grid=(b,),
        in_specs=[pl.BlockSpec((1, s, LANES), lambda i: (i, 0, 0)),
                  pl.BlockSpec((1, LANES), lambda i: (0, 0))],
        out_specs=[pl.BlockSpec((1, s, LANES), lambda i: (i, 0, 0)),
                   pl.BlockSpec((1, 8, s), lambda i: (i, 0, 0))],
        out_shape=[jax.ShapeDtypeStruct((b, s, LANES), F32),
                   jax.ShapeDtypeStruct((b, 8, s), F32)],
        compiler_params=_params("parallel"),
        name="fox_gate",
    )(fb, b_f)


def _flash(qs, k_ref, v_ref, st_ref, dk, dv, qi, tq, fqs=None, frow_ref=None):
    nh = len(qs)
    r = qs[0].shape[0]
    assert dv == LANES and tq % LANES == 0
    groups = tq // LANES
    if fqs is not None:
        fqs = [jnp.broadcast_to(f, (r, LANES)) for f in fqs]

    def block(j, first):
        start = pl.multiple_of(j * tq, tq)
        scores = []
        for h in range(nh):
            k = k_ref[0, pl.ds(start, tq), h * dk:(h + 1) * dk]
            s = _dot_nt(qs[h], k)
            if frow_ref is not None:
                s = s - frow_ref[0, h:h + 1, pl.ds(start, tq)]
            if first:
                row = lax.broadcasted_iota(jnp.int32, (r, tq), 0) & (tq - 1)
                col = lax.broadcasted_iota(jnp.int32, (r, tq), 1)
                s = jnp.where(col <= row, s, -jnp.inf)
            scores.append(s)
        for h in range(nh):
            mz = jnp.broadcast_to(jnp.max(scores[h], axis=-1, keepdims=True), (r, LANES))
            if fqs is not None:
                mz = mz + fqs[h]
            if first:
                m_new = mz
            else:
                m_old = st_ref[0, h]
                m_new = jnp.maximum(m_old, mz)
                alpha = jnp.exp2(m_old - m_new)
            shift = (fqs[h] - m_new) if fqs is not None else -m_new
            p = jnp.exp2(scores[h] + jnp.concatenate([shift] * groups, axis=1))
            psum = p[:, 0:LANES]
            for g in range(1, groups):
                psum = psum + p[:, g * LANES:(g + 1) * LANES]
            pv = _dot(p.astype(BF16), v_ref[0, pl.ds(start, tq), h * dv:(h + 1) * dv])
            st_ref[0, h] = m_new
            if first:
                st_ref[1, h] = psum
                st_ref[2, h] = pv
            else:
                st_ref[1, h] = alpha * st_ref[1, h] + psum
                st_ref[2, h] = alpha * st_ref[2, h] + pv

    block(qi, True)

    def body(j, c):
        block(j, False)
        return c

    lax.fori_loop(0, qi, body, 0)
    return [(st_ref[2, h], jnp.sum(st_ref[1, h], axis=-1, keepdims=True)) for h in range(nh)]


def _flash_state(rows):
    return pltpu.VMEM((3, GROUP_HEADS, rows, LANES), F32)


def _fox_attn_kernel(q_ref, k_ref, v_ref, fcol_ref, frow_ref, o_ref, st_ref, *, tq):
    qi = pl.program_id(1)
    qs = [q_ref[0, :, h * HEAD_DIM:(h + 1) * HEAD_DIM] for h in range(GROUP_HEADS)]
    fqs = [fcol_ref[0, :, h:h + 1] for h in range(GROUP_HEADS)]
    res = _flash(qs, k_ref, v_ref, st_ref, HEAD_DIM, HEAD_DIM, qi, tq, fqs, frow_ref)
    for h, (acc, l) in enumerate(res):
        o_ref[0, :, h * HEAD_DIM:(h + 1) * HEAD_DIM] = (acc / l).astype(o_ref.dtype)


def _fox_attn(p3, fcol, frow, tq=512):
    b, s, _ = p3.shape
    w = GROUP_W
    c0 = P_FOX // w
    return pl.pallas_call(
        functools.partial(_fox_attn_kernel, tq=tq),
        grid=(b, s // tq),
        in_specs=[pl.BlockSpec((1, tq, w), lambda i, j: (i, j, c0)),
                  pl.BlockSpec((1, s, w), lambda i, j: (i, 0, c0 + 1)),
                  pl.BlockSpec((1, s, w), lambda i, j: (i, 0, c0 + 2)),
                  pl.BlockSpec((1, tq, LANES), lambda i, j: (i, j, 0)),
                  pl.BlockSpec((1, 8, s), lambda i, j: (i, 0, 0))],
        out_specs=pl.BlockSpec((1, tq, w), lambda i, j: (i, j, 0)),
        out_shape=jax.ShapeDtypeStruct((b, s, w), BF16),
        scratch_shapes=[_flash_state(tq)],
        compiler_params=_params("parallel", "arbitrary"),
        name="fox_attn",
    )(p3, p3, p3, fcol, frow)


def _rope_pairs(x, cos, sin, half, period):
    lane = lax.broadcasted_iota(jnp.int32, x.shape, x.ndim - 1)
    first = (lane & (period - 1)) < half
    rot = jnp.where(first, -pltpu.roll(x, LANES - half, x.ndim - 1), pltpu.roll(x, half, x.ndim - 1))
    return x * cos + rot * sin


def _mla_prep_kernel(p_ref, gq_ref, gkv_ref, wq_ref, wkv_ref, cos_ref, sin_ref,
                     q_ref, k_ref, v_ref, *, qscale):
    blk = p_ref[0].astype(F32)
    cos = cos_ref[0]
    sin = sin_ref[0]
    cq = blk[:, 0:MLA_Q_RANK]
    ckv = blk[:, MLA_Q_RANK:MLA_Q_RANK + MLA_KV_RANK]
    kr = blk[:, MLA_Q_RANK + MLA_KV_RANK:MLA_Q_RANK + MLA_KV_RANK + LANES]
    q = _dot(_rms(cq, gq_ref[...]).astype(BF16), wq_ref[...])
    kv = _dot(_rms(ckv, gkv_ref[...]).astype(BF16), wkv_ref[...])
    kr = _rope_pairs(kr, cos, sin, MLA_ROPE // 2, LANES).astype(BF16)
    for h in range(GROUP_HEADS):
        o = h * MLA_QK_PAD
        q_ref[0, :, o:o + LANES] = (q[:, o:o + LANES] * qscale).astype(BF16)
        q_ref[0, :, o + LANES:o + 2 * LANES] = (_rope_pairs(
            q[:, o + LANES:o + 2 * LANES], cos, sin, MLA_ROPE // 2, LANES) * qscale).astype(BF16)
        k_ref[0, :, o:o + LANES] = kv[:, o:o + LANES].astype(BF16)
        k_ref[0, :, o + LANES:o + 2 * LANES] = kr
        v_ref[0, :, h * MLA_V:(h + 1) * MLA_V] = kv[:, o + LANES:o + 2 * LANES].astype(BF16)


def _mla_prep(p3, g_cq, g_ckv, w_uq, w_ukv, cos, sin, l, tm=512):
    b, s, _ = p3.shape
    c0 = P_MLA // MLA_BLOCK_W
    hq = GROUP_HEADS * MLA_QK_PAD
    qscale = (MLA_NOPE + MLA_ROPE) ** -0.5 * LOG2E
    return pl.pallas_call(
        functools.partial(_mla_prep_kernel, qscale=qscale),
        grid=(b, s // tm),
        in_specs=[pl.BlockSpec((1, tm, MLA_BLOCK_W), lambda i, j: (i, j, c0)),
                  pl.BlockSpec((1, MLA_Q_RANK), lambda i, j: (0, 0)),
                  pl.BlockSpec((1, MLA_KV_RANK), lambda i, j: (0, 0)),
                  pl.BlockSpec((None, MLA_Q_RANK, hq), lambda i, j: (l, 0, 0)),
                  pl.BlockSpec((None, MLA_KV_RANK, hq), lambda i, j: (l, 0, 0)),
                  pl.BlockSpec((1, tm, LANES), lambda i, j: (i, j, 0)),
                  pl.BlockSpec((1, tm, LANES), lambda i, j: (i, j, 0))],
        out_specs=[pl.BlockSpec((1, tm, hq), lambda i, j: (i, j, 0)),
                   pl.BlockSpec((1, tm, hq), lambda i, j: (i, j, 0)),
                   pl.BlockSpec((1, tm, GROUP_W), lambda i, j: (i, j, 0))],
        out_shape=[jax.ShapeDtypeStruct((b, s, hq), BF16),
                   jax.ShapeDtypeStruct((b, s, hq), BF16),
                   jax.ShapeDtypeStruct((b, s, GROUP_W), BF16)],
        compiler_params=_params("parallel", "parallel"),
        name="mla_prep",
    )(p3, g_cq, g_ckv, w_uq, w_ukv, cos, sin)


def _mla_attn_kernel(q_ref, k_ref, v_ref, o_ref, st_ref, *, tq):
    qi = pl.program_id(1)
    qs = [q_ref[0, :, h * MLA_QK_PAD:(h + 1) * MLA_QK_PAD] for h in range(GROUP_HEADS)]
    res = _flash(qs, k_ref, v_ref, st_ref, MLA_QK_PAD, MLA_V, qi, tq)
    for h, (acc, l) in enumerate(res):
        o_ref[0, :, h * MLA_V:(h + 1) * MLA_V] = (acc / l).astype(o_ref.dtype)


def _mla_attn(qm, km, vm, tq=512):
    b, s, hq = qm.shape
    return pl.pallas_call(
        functools.partial(_mla_attn_kernel, tq=tq),
        grid=(b, s // tq),
        in_specs=[pl.BlockSpec((1, tq, hq), lambda i, j: (i, j, 0)),
                  pl.BlockSpec((1, s, hq), lambda i, j: (i, 0, 0)),
                  pl.BlockSpec((1, s, GROUP_W), lambda i, j: (i, 0, 0))],
        out_specs=pl.BlockSpec((1, tq, GROUP_W), lambda i, j: (i, j, 0)),
        out_shape=jax.ShapeDtypeStruct((b, s, GROUP_W), BF16),
        scratch_shapes=[_flash_state(tq)],
        compiler_params=_params("parallel", "arbitrary"),
        name="mla_attn",
    )(qm, km, vm)


def _diff_prep_kernel(q_ref, k_ref, cos_ref, sin_ref, qo_ref, ko_ref, *, qscale):
    cos = cos_ref[0]
    sin = sin_ref[0]
    for src, dst, scale in ((q_ref, qo_ref, qscale), (k_ref, ko_ref, None)):
        for c in range(GROUP_W // LANES):
            x = src[0, :, c * LANES:(c + 1) * LANES].astype(F32)
            y = _rope_pairs(x, cos, sin, DIFF_ROT // 2, DIFF_D)
            if scale is not None:
                y = y * scale
            dst[0, :, c * LANES:(c + 1) * LANES] = y.astype(BF16)


def _diff_prep(p3, cos, sin, tm=1024):
    b, s, _ = p3.shape
    c0 = P_DIFF // GROUP_W
    return pl.pallas_call(
        functools.partial(_diff_prep_kernel, qscale=DIFF_D ** -0.5 * LOG2E),
        grid=(b, s // tm),
        in_specs=[pl.BlockSpec((1, tm, GROUP_W), lambda i, j: (i, j, c0)),
                  pl.BlockSpec((1, tm, GROUP_W), lambda i, j: (i, j, c0 + 1)),
                  pl.BlockSpec((1, tm, LANES), lambda i, j: (i, j, 0)),
                  pl.BlockSpec((1, tm, LANES), lambda i, j: (i, j, 0))],
        out_specs=[pl.BlockSpec((1, tm, GROUP_W), lambda i, j: (i, j, 0)),
                   pl.BlockSpec((1, tm, GROUP_W), lambda i, j: (i, j, 0))],
        out_shape=[jax.ShapeDtypeStruct((b, s, GROUP_W), BF16),
                   jax.ShapeDtypeStruct((b, s, GROUP_W), BF16)],
        compiler_params=_params("parallel", "parallel"),
        name="diff_prep",
    )(p3, p3, cos, sin)


def _diff_attn_kernel(q_ref, k_ref, v_ref, lam_ref, g_ref, o_ref, st_ref, *, tq, lam_init):
    qi = pl.program_id(1)
    lv = lam_ref[...]
    lam = (jnp.exp(jnp.sum(lv[0:1] * lv[1:2], axis=-1, keepdims=True))
           - jnp.exp(jnp.sum(lv[2:3] * lv[3:4], axis=-1, keepdims=True)) + lam_init)
    lane = lax.broadcasted_iota(jnp.int32, (tq, HEAD_DIM), 1)
    qs = []
    for h in range(GROUP_HEADS):
        q = q_ref[0, :, h * HEAD_DIM:(h + 1) * HEAD_DIM]
        zero = jnp.zeros_like(q)
        qs.append(jnp.concatenate([jnp.where(lane < DIFF_D, q, zero),
                                   jnp.where(lane >= DIFF_D, q, zero)], axis=0))
    res = _flash(qs, k_ref, v_ref, st_ref, HEAD_DIM, HEAD_DIM, qi, tq)
    for h, (acc, l) in enumerate(res):
        o = acc / l
        o = o[0:tq] - lam * o[tq:2 * tq]
        o = _rms(o, g_ref[...]) * (1.0 - lam_init)
        o_ref[0, :, h * HEAD_DIM:(h + 1) * HEAD_DIM] = o.astype(o_ref.dtype)


def _diff_attn(qr, kr, p3, lamv, g_diff, lam_init, tq=512):
    b, s, w = qr.shape
    cv = P_DIFF // w + 2
    return pl.pallas_call(
        functools.partial(_diff_attn_kernel, tq=tq, lam_init=lam_init),
        grid=(b, s // tq),
        in_specs=[pl.BlockSpec((1, tq, w), lambda i, j: (i, j, 0)),
                  pl.BlockSpec((1, s, w), lambda i, j: (i, 0, 0)),
                  pl.BlockSpec((1, s, w), lambda i, j: (i, 0, cv)),
                  pl.BlockSpec((4, DIFF_D), lambda i, j: (0, 0)),
                  pl.BlockSpec((1, HEAD_DIM), lambda i, j: (0, 0))],
        out_specs=pl.BlockSpec((1, tq, w), lambda i, j: (i, j, 0)),
        out_shape=jax.ShapeDtypeStruct((b, s, w), BF16),
        scratch_shapes=[_flash_state(2 * tq)],
        compiler_params=_params("parallel", "arbitrary"),
        name="diff_attn",
    )(qr, kr, p3, lamv, g_diff)


def _sgu_kernel(uv_ref, lg_ref, lb_ref, ws_ref, bs_ref, o_ref, *, tm):
    z = uv_ref[...].astype(F32)
    z = 0.5 * z * (1.0 + jnp.tanh(math.sqrt(2.0 / math.pi) * (z + 0.044715 * (z * z * z))))
    u = z[:, 0:GROUP_W]
    v = z[:, GROUP_W:2 * GROUP_W]
    xc = v - jnp.mean(v, axis=-1, keepdims=True)
    vn = xc * lax.rsqrt(jnp.mean(xc * xc, axis=-1, keepdims=True) + EPS) * lg_ref[...] + lb_ref[...]
    vn = vn.astype(BF16)
    t = SGU_CHUNK
    tril = (lax.broadcasted_iota(jnp.int32, (t, t), 1) <= lax.broadcasted_iota(jnp.int32, (t, t), 0))
    for g in range(GROUP_HEADS):
        w = jnp.where(tril, ws_ref[g], 0.0).astype(BF16)
        bias = bs_ref[:, g:g + 1]
        for c in range(tm // t):
            rows = slice(c * t, (c + 1) * t)
            cols = slice(g * HEAD_DIM, (g + 1) * HEAD_DIM)
            mixed = _dot(w, vn[rows, cols]) + bias
            o_ref[rows, cols] = (u[rows, cols] * mixed).astype(o_ref.dtype)


def _sgu(p, ln_g, ln_b, w_s, b_st, l, tm=512):
    t = p.shape[0]
    c0 = P_SGU // (2 * GROUP_W)
    return pl.pallas_call(
        functools.partial(_sgu_kernel, tm=tm),
        grid=(t // tm,),
        in_specs=[pl.BlockSpec((tm, 2 * GROUP_W), lambda i: (i, c0)),
                  pl.BlockSpec((1, GROUP_W), lambda i: (0, 0)),
                  pl.BlockSpec((1, GROUP_W), lambda i: (0, 0)),
                  pl.BlockSpec((None, GROUP_HEADS, SGU_CHUNK, SGU_CHUNK), lambda i: (l, 0, 0, 0)),
                  pl.BlockSpec((None, SGU_CHUNK, GROUP_HEADS), lambda i: (l, 0, 0))],
        out_specs=pl.BlockSpec((tm, GROUP_W), lambda i: (i, 0)),
        out_shape=jax.ShapeDtypeStruct((t, GROUP_W), BF16),
        compiler_params=_params("parallel"),
        name="sgu",
    )(p, ln_g, ln_b, w_s, b_st)


def _mix_cross_kernel(x_ref, ya_ref, yb_ref, yc_ref, yd_ref, wo_ref, g_ref, wq_ref, kv_ref,
                      wco_ref, o_ref, oh_ref, *, scale):
    acc = x_ref[...]
    for m, y_ref in enumerate((ya_ref, yb_ref, yc_ref, yd_ref)):
        acc = acc + _dot(y_ref[...], wo_ref[m * GROUP_W:(m + 1) * GROUP_W, :])
    o_ref[...] = acc
    q = _dot(_rms(o_ref[...], g_ref[...]).astype(BF16), wq_ref[...]).astype(BF16)
    for h in range(CROSS_HEADS):
        cols = slice(h * CROSS_HEAD_DIM, (h + 1) * CROSS_HEAD_DIM)
        k = kv_ref[0, :, cols]
        v = kv_ref[0, :, CROSS_W + h * CROSS_HEAD_DIM:CROSS_W + (h + 1) * CROSS_HEAD_DIM]
        s = _dot_nt(q[:, cols], k) * scale
        p = jnp.exp(s - jnp.max(s, axis=-1, keepdims=True))
        l = jnp.sum(p, axis=-1, keepdims=True)
        oh_ref[:, cols] = (_dot(p.astype(BF16), v) / l).astype(BF16)
    o_ref[...] += _dot(oh_ref[...], wco_ref[...])


def _mix_cross(x, ys, w_o, g, wq, kv, wco, seq, l, tm=512):
    t, d = x.shape
    per_b = seq // tm
    m = kv.shape[1]
    once = pl.Buffered(1)
    yspec = pl.BlockSpec((tm, GROUP_W), lambda i: (i, 0))
    return pl.pallas_call(
        functools.partial(_mix_cross_kernel, scale=CROSS_HEAD_DIM ** -0.5),
        grid=(t // tm,),
        in_specs=[pl.BlockSpec((tm, d), lambda i: (i, 0)), yspec, yspec, yspec, yspec,
                  pl.BlockSpec((None, w_o.shape[1], d), lambda i: (l, 0, 0), pipeline_mode=once),
                  pl.BlockSpec((1, d), lambda i: (0, 0)),
                  pl.BlockSpec((None, d, CROSS_W), lambda i: (l, 0, 0), pipeline_mode=once),
                  pl.BlockSpec((1, m, 2 * CROSS_W), lambda i: (i // per_b, 0, 0)),
                  pl.BlockSpec((None, CROSS_W, d), lambda i: (l, 0, 0), pipeline_mode=once)],
        out_specs=pl.BlockSpec((tm, d), lambda i: (i, 0)),
        out_shape=jax.ShapeDtypeStruct((t, d), F32),
        scratch_shapes=[pltpu.VMEM((tm, CROSS_W), BF16)],
        compiler_params=_params("parallel"),
        name="mix_cross",
    )(x, *ys, w_o, g, wq, kv, wco)


def _ffn_kernel(x_ref, g_ref, wg_ref, wu_ref, wd_ref, gf_ref, o_ref, xn_ref, acc_ref, *, final):
    j = pl.program_id(1)

    @pl.when(j == 0)
    def _():
        xn_ref[...] = _rms(x_ref[...], g_ref[...]).astype(BF16)
        acc_ref[...] = jnp.zeros_like(acc_ref)

    xn = xn_ref[...]
    gate = _dot(xn, wg_ref[...])
    up = _dot(xn, wu_ref[...])
    hid = (gate * jax.nn.sigmoid(gate) * up).astype(BF16)
    acc_ref[...] += _dot(hid, wd_ref[...])

    @pl.when(j == pl.num_programs(1) - 1)
    def _():
        y = x_ref[...] + acc_ref[...]
        o_ref[...] = _rms(y, gf_ref[...]) if final else y


def _ffn(x, g, wg, wu, wd, g_final, final, l, tm=512, th=512):
    t, d = x.shape
    hdim = wg.shape[2]
    return pl.pallas_call(
        functools.partial(_ffn_kernel, final=final),
        grid=(t // tm, hdim // th),
        in_specs=[pl.BlockSpec((tm, d), lambda i, j: (i, 0)),
                  pl.BlockSpec((1, d), lambda i, j: (0, 0)),
                  pl.BlockSpec((None, d, th), lambda i, j: (l, 0, j)),
                  pl.BlockSpec((None, d, th), lambda i, j: (l, 0, j)),
                  pl.BlockSpec((None, th, d), lambda i, j: (l, j, 0)),
                  pl.BlockSpec((1, d), lambda i, j: (0, 0))],
        out_specs=pl.BlockSpec((tm, d), lambda i, j: (i, 0)),
        out_shape=jax.ShapeDtypeStruct((t, d), F32),
        scratch_shapes=[pltpu.VMEM((tm, d), BF16), pltpu.VMEM((tm, d), F32)],
        compiler_params=_params("parallel", "arbitrary"),
        name="swiglu",
    )(x, g, wg, wu, wd, g_final)


def _rope_tables(positions):
    pos = positions.astype(F32)[..., None]
    b, s = positions.shape

    def table(rot_dim, period):
        freqs = ROPE_THETA ** (-jnp.arange(0, rot_dim, 2, dtype=F32) / rot_dim)
        ang = pos * freqs
        ones = jnp.ones((b, s, period - rot_dim), F32)
        cos = jnp.concatenate([jnp.cos(ang), jnp.cos(ang), ones], axis=-1)
        sin = jnp.concatenate([jnp.sin(ang), jnp.sin(ang), 0.0 * ones], axis=-1)
        reps = LANES // period
        return jnp.tile(cos, (1, 1, reps)), jnp.tile(sin, (1, 1, reps))

    return table(MLA_ROPE, LANES), table(DIFF_ROT, DIFF_D)


_SRC_FORGET = 3 * GROUP_W
_SRC_MLA = _SRC_FORGET + GROUP_HEADS
_MLA_SRC_W = MLA_Q_RANK + MLA_KV_RANK + MLA_ROPE
_SRC_SGU = _SRC_MLA + _MLA_SRC_W
_SRC_DIFF = _SRC_SGU + 2 * GROUP_W


def _split_w_in_kernel(w_ref, main_ref, wf_ref):
    copies = ((P_FOX, 0, 3 * GROUP_W), (P_DIFF, _SRC_DIFF, 3 * GROUP_W),
              (P_SGU, _SRC_SGU, 2 * GROUP_W), (P_MLA, _SRC_MLA, _MLA_SRC_W))
    for dst, src, width in copies:
        main_ref[0, :, dst:dst + width] = w_ref[0, :, src:src + width].astype(BF16)
    rows = w_ref.shape[1]
    main_ref[0, :, P_MLA + _MLA_SRC_W:P_W] = jnp.zeros((rows, P_W - P_MLA - _MLA_SRC_W), BF16)
    f = w_ref[0, :, _SRC_FORGET:_SRC_FORGET + LANES]
    lane = lax.broadcasted_iota(jnp.int32, f.shape, 1)
    wf_ref[0] = jnp.where(lane < GROUP_HEADS, f, 0.0).astype(BF16)


def _split_w_in(w, tk=256):
    nl, k, n = w.shape
    return pl.pallas_call(
        _split_w_in_kernel,
        grid=(nl, k // tk),
        in_specs=[pl.BlockSpec((1, tk, n), lambda l, i: (l, i, 0))],
        out_specs=[pl.BlockSpec((1, tk, P_W), lambda l, i: (l, i, 0)),
                   pl.BlockSpec((1, tk, LANES), lambda l, i: (l, i, 0))],
        out_shape=[jax.ShapeDtypeStruct((nl, k, P_W), BF16),
                   jax.ShapeDtypeStruct((nl, k, LANES), BF16)],
        compiler_params=_params("parallel", "parallel"),
        name="split_w_in",
    )(w)


def _pad_heads(w, used, width):
    nl, k, _ = w.shape
    w = w.reshape(nl, k, GROUP_HEADS, used)
    w = jnp.pad(w, ((0, 0), (0, 0), (0, 0), (0, width - used)))
    return w.reshape(nl, k, GROUP_HEADS * width)


def kernel(x, mem, positions, g_mix, w_in, b_f, g_cq, g_ckv, w_uq, w_ukv, sgu_ln_g, sgu_ln_b, w_s,
           b_s, lam_q1, lam_k1, lam_q2, lam_k2, g_diff, w_o, g_mem, g_cross, w_cq, w_ck, w_cv, w_co,
           g_ffn, w_gate, w_up, w_down, g_final):
    b, s, d = x.shape
    depth = w_in.shape[0]
    t = b * s
    (cos_m, sin_m), (cos_d, sin_d) = _rope_tables(positions)
    xf = x.reshape(t, d)
    memf = mem.reshape(b * mem.shape[1], d)
    row = lambda a: a.reshape(1, -1)

    w_main, w_f = _split_w_in(w_in)
    w_uq_p = _pad_heads(w_uq, MLA_NOPE + MLA_ROPE, MLA_QK_PAD).astype(BF16)
    w_ukv_b = w_ukv.astype(BF16)
    w_o_b, w_cq_b, w_ck_b, w_cv_b, w_co_b, w_gate_b, w_up_b, w_down_b = (
        _cast_bf16(w) for w in (w_o, w_cq, w_ck, w_cv, w_co, w_gate, w_up, w_down))
    b_st = jnp.swapaxes(b_s, 1, 2)
    col_scale = jnp.ones((1, P_W), F32).at[:, P_FOX:P_FOX + GROUP_W].set(HEAD_DIM ** -0.5 * LOG2E)

    for l in range(depth):
        lam_init = 0.8 - 0.6 * math.exp(-0.3 * l)
        p, fb = _in_proj(xf, row(g_mix[l]), w_main, w_f, col_scale, l)
        p3 = p.reshape(b, s, P_W)

        bf_pad = jnp.pad(b_f[l], (0, LANES - GROUP_HEADS)).reshape(1, LANES)
        fcol, frow = _fox_gate(fb.reshape(b, s, LANES), bf_pad)
        y_a = _fox_attn(p3, fcol, frow)

        qm, km, vm = _mla_prep(p3, row(g_cq[l]), row(g_ckv[l]), w_uq_p, w_ukv_b, cos_m, sin_m, l)
        y_b = _mla_attn(qm, km, vm)

        y_c = _sgu(p, row(sgu_ln_g[l]), row(sgu_ln_b[l]), w_s, b_st, l)

        qr, kr = _diff_prep(p3, cos_d, sin_d)
        lamv = jnp.stack([lam_q1[l], lam_k1[l], lam_q2[l], lam_k2[l]])
        y_d = _diff_attn(qr, kr, p3, lamv, row(g_diff[l]), lam_init)

        ys = [y.reshape(t, GROUP_W) for y in (y_a, y_b, y_c, y_d)]
        kv = _mem_kv(memf, row(g_mem), w_ck_b, w_cv_b, l).reshape(b, mem.shape[1], 2 * CROSS_W)
        xf = _mix_cross(xf, ys, w_o_b, row(g_cross[l]), w_cq_b, kv, w_co_b, s, l)

        xf = _ffn(xf, row(g_ffn[l]), w_gate_b, w_up_b, w_down_b, row(g_final),
                  final=(l == depth - 1), l=l)
    return xf.reshape(b, s, d)
```

```python
import functools
import math

import jax
import jax.numpy as jnp
from jax import lax
from jax.experimental import pallas as pl
from jax.experimental.pallas import tpu as pltpu

F32 = jnp.float32
BF16 = jnp.bfloat16

HEAD_DIM = 128
GROUP_HEADS = 4
GROUP_W = GROUP_HEADS * HEAD_DIM
MLA_Q_RANK = 384
MLA_KV_RANK = 256
MLA_NOPE = 128
MLA_ROPE = 64
MLA_V = 128
MLA_QK_PAD = 256
SGU_CHUNK = 128
DIFF_D = HEAD_DIM // 2
DIFF_ROT = DIFF_D // 4
CROSS_HEADS = 4
CROSS_HEAD_DIM = 128
CROSS_W = CROSS_HEADS * CROSS_HEAD_DIM
ROPE_THETA = 500000.0
EPS = 1e-6
LOG2E = math.log2(math.e)
LANES = 128

P_FOX = 0
P_DIFF = 3 * GROUP_W
P_SGU = 6 * GROUP_W
P_MLA = 8 * GROUP_W
P_W = 10 * GROUP_W
MLA_BLOCK_W = 2 * GROUP_W

VMEM_LIMIT = 52 * 2**20


def _params(*sem):
    return pltpu.CompilerParams(dimension_semantics=sem, vmem_limit_bytes=VMEM_LIMIT)


def _rms(x, g):
    return x * lax.rsqrt(jnp.mean(x * x, axis=-1, keepdims=True) + EPS) * g


def _dot(a, b):
    return jnp.dot(a, b, preferred_element_type=F32)


def _dot_nt(a, b):
    return lax.dot_general(a, b, (((1,), (1,)), ((), ())), preferred_element_type=F32)


def _cast_kernel(w_ref, o_ref):
    o_ref[...] = w_ref[...].astype(o_ref.dtype)


def _cast_bf16(w, block_bytes=4 * 2**20):
    nl, k, n = w.shape
    tk = k
    while tk * n * 4 > block_bytes and tk % 16 == 0:
        tk //= 2
    return pl.pallas_call(
        _cast_kernel,
        grid=(nl, k // tk),
        in_specs=[pl.BlockSpec((1, tk, n), lambda l, i: (l, i, 0))],
        out_specs=pl.BlockSpec((1, tk, n), lambda l, i: (l, i, 0)),
        out_shape=jax.ShapeDtypeStruct(w.shape, BF16),
        compiler_params=_params("parallel", "parallel"),
        name="cast_bf16",
    )(w)


def _in_proj_kernel(x_ref, g_ref, w_ref, wf_ref, cs_ref, p_ref, f_ref, xn_ref):
    @pl.when(pl.program_id(1) == 0)
    def _():
        xb = _rms(x_ref[...], g_ref[...]).astype(BF16)
        xn_ref[...] = xb
        f_ref[...] = _dot(xb, wf_ref[...])

    p_ref[...] = (_dot(xn_ref[...], w_ref[...]) * cs_ref[...]).astype(p_ref.dtype)


def _in_proj(x, g, w, wf, col_scale, l, tm=1024, tn=1024):
    t, d = x.shape
    n = w.shape[2]
    return pl.pallas_call(
        _in_proj_kernel,
        grid=(t // tm, n // tn),
        in_specs=[pl.BlockSpec((tm, d), lambda i, j: (i, 0)),
                  pl.BlockSpec((1, d), lambda i, j: (0, 0)),
                  pl.BlockSpec((None, d, tn), lambda i, j: (l, 0, j)),
                  pl.BlockSpec((None, d, LANES), lambda i, j: (l, 0, 0)),
                  pl.BlockSpec((1, tn), lambda i, j: (0, j))],
        out_specs=[pl.BlockSpec((tm, tn), lambda i, j: (i, j)),
                   pl.BlockSpec((tm, LANES), lambda i, j: (i, 0))],
        out_shape=[jax.ShapeDtypeStruct((t, n), BF16),
                   jax.ShapeDtypeStruct((t, LANES), F32)],
        scratch_shapes=[pltpu.VMEM((tm, d), BF16)],
        compiler_params=_params("parallel", "arbitrary"),
        name="in_proj",
    )(x, g, w, wf, col_scale)


def _mem_kv_kernel(x_ref, g_ref, wk_ref, wv_ref, o_ref):
    xb = _rms(x_ref[...], g_ref[...]).astype(BF16)
    o_ref[:, 0:CROSS_W] = _dot(xb, wk_ref[...]).astype(o_ref.dtype)
    o_ref[:, CROSS_W:2 * CROSS_W] = _dot(xb, wv_ref[...]).astype(o_ref.dtype)


def _mem_kv(x, g, wk, wv, l, tm=512):
    t, d = x.shape
    wspec = pl.BlockSpec((None, d, CROSS_W), lambda i: (l, 0, 0))
    return pl.pallas_call(
        _mem_kv_kernel,
        grid=(t // tm,),
        in_specs=[pl.BlockSpec((tm, d), lambda i: (i, 0)),
                  pl.BlockSpec((1, d), lambda i: (0, 0)), wspec, wspec],
        out_specs=pl.BlockSpec((tm, 2 * CROSS_W), lambda i: (i, 0)),
        out_shape=jax.ShapeDtypeStruct((t, 2 * CROSS_W), BF16),
        compiler_params=_params("parallel"),
        name="mem_kv_proj",
    )(x, g, wk, wv)


def _fox_gate_kernel(f_ref, b_ref, col_ref, row_ref):
    x = f_ref[0] + b_ref[...]
    c = jnp.minimum(x, 0.0) - jnp.log1p(jnp.exp(-jnp.abs(x)))
    s = x.shape[0]
    ridx = lax.broadcasted_iota(jnp.int32, x.shape, 0)
    shift = 1
    while shift < s:
        c = c + jnp.where(ridx >= shift, pltpu.roll(c, shift, 0), 0.0)
        shift *= 2
    c = c * LOG2E
    col_ref[0] = c
    row_ref[0] = c.T[0:8, :]


def _fox_gate(fb, b_f):
    b, s, _ = fb.shape
    return pl.pallas_call(
        _fox_gate_kernel,
        grid=(b,),
        in_specs=[pl.BlockSpec((1, s, LANES), lambda i: (i, 0, 0)),
                  pl.BlockSpec((1, LANES), lambda i: (0, 0))],
        out_specs=[pl.BlockSpec((1, s, LANES), lambda i: (i, 0, 0)),
                   pl.BlockSpec((1, 8, s), lambda i: (i, 0, 0))],
        out_shape=[jax.ShapeDtypeStruct((b, s, LANES), F32),
                   jax.ShapeDtypeStruct((b, 8, s), F32)],
        compiler_params=_params("parallel"),
        name="fox_gate",
    )(fb, b_f)


def _flash(qs, k_ref, v_ref, st_ref, dk, dv, qi, tq, fqs=None, frow_ref=None):
    nh = len(qs)
    r = qs[0].shape[0]
    assert dv == LANES and tq % (2 * LANES) == 0
    if fqs is not None:
        fqs = [jnp.broadcast_to(f, (r, LANES)) for f in fqs]

    def scores(h, rows, start, nk, diag_row0=None):
        s = _dot_nt(qs[h][rows], k_ref[0, pl.ds(start, nk), h * dk:(h + 1) * dk])
        if frow_ref is not None:
            s = s - frow_ref[0, h:h + 1, pl.ds(start, nk)]
        if diag_row0 is not None:
            row = lax.broadcasted_iota(jnp.int32, s.shape, 0) + diag_row0
            col = lax.broadcasted_iota(jnp.int32, s.shape, 1)
            s = jnp.where(col <= row, s, -jnp.inf)
        return s

    def absorb(s, h, rows, start, old):
        nr, nk = s.shape
        mz = jnp.broadcast_to(jnp.max(s, axis=-1, keepdims=True), (nr, LANES))
        if fqs is not None:
            mz = mz + fqs[h][rows]
        if old is None:
            m_new = mz
        else:
            m_new = jnp.maximum(old[0], mz)
            alpha = jnp.exp2(old[0] - m_new)
        shift = (fqs[h][rows] - m_new) if fqs is not None else -m_new
        p = jnp.exp2(s + jnp.concatenate([shift] * (nk // LANES), axis=1))
        psum = p[:, 0:LANES]
        for g in range(1, nk // LANES):
            psum = psum + p[:, g * LANES:(g + 1) * LANES]
        pv = _dot(p.astype(BF16), v_ref[0, pl.ds(start, nk), h * dv:(h + 1) * dv])
        if old is None:
            return m_new, psum, pv
        return m_new, alpha * old[1] + psum, alpha * old[2] + pv

    half = tq // 2
    diag_start = pl.multiple_of(qi * tq, tq)
    pieces = []
    for rep in range(r // tq):
        pieces.append((slice(rep * tq, rep * tq + half), half, 0))
        pieces.append((slice(rep * tq + half, (rep + 1) * tq), tq, half))
    diag = [[scores(h, rows, diag_start, nk, row0) for rows, nk, row0 in pieces] for h in range(nh)]
    for h in range(nh):
        parts = [absorb(s, h, rows, diag_start, None) for s, (rows, _, _) in zip(diag[h], pieces)]
        for i in range(3):
            st_ref[i, h] = jnp.concatenate([part[i] for part in parts], axis=0)

    def body(j, c):
        start = pl.multiple_of(j * tq, tq)
        every = slice(0, r)
        blocks = [scores(h, every, start, tq) for h in range(nh)]
        for h in range(nh):
            new = absorb(blocks[h], h, every, start, (st_ref[0, h], st_ref[1, h], st_ref[2, h]))
            for i in range(3):
                st_ref[i, h] = new[i]
        return c

    lax.fori_loop(0, qi, body, 0)
    return [(st_ref[2, h], jnp.sum(st_ref[1, h], axis=-1, keepdims=True)) for h in range(nh)]


def _flash_state(rows):
    return pltpu.VMEM((3, GROUP_HEADS, rows, LANES), F32)


def _fox_attn_kernel(q_ref, k_ref, v_ref, fcol_ref, frow_ref, o_ref, st_ref, *, tq):
    qi = pl.program_id(1)
    qs = [q_ref[0, :, h * HEAD_DIM:(h + 1) * HEAD_DIM] for h in range(GROUP_HEADS)]
    fqs = [fcol_ref[0, :, h:h + 1] for h in range(GROUP_HEADS)]
    res = _flash(qs, k_ref, v_ref, st_ref, HEAD_DIM, HEAD_DIM, qi, tq, fqs, frow_ref)
    for h, (acc, l) in enumerate(res):
        o_ref[0, :, h * HEAD_DIM:(h + 1) * HEAD_DIM] = (acc / l).astype(o_ref.dtype)


def _fox_attn(p3, fcol, frow, tq=512):
    b, s, _ = p3.shape
    w = GROUP_W
    c0 = P_FOX // w
    return pl.pallas_call(
        functools.partial(_fox_attn_kernel, tq=tq),
        grid=(b, s // tq),
        in_specs=[pl.BlockSpec((1, tq, w), lambda i, j: (i, j, c0)),
                  pl.BlockSpec((1, s, w), lambda i, j: (i, 0, c0 + 1)),
                  pl.BlockSpec((1, s, w), lambda i, j: (i, 0, c0 + 2)),
                  pl.BlockSpec((1, tq, LANES), lambda i, j: (i, j, 0)),
                  pl.BlockSpec((1, 8, s), lambda i, j: (i, 0, 0))],
        out_specs=pl.BlockSpec((1, tq, w), lambda i, j: (i, j, 0)),
        out_shape=jax.ShapeDtypeStruct((b, s, w), BF16),
        scratch_shapes=[_flash_state(tq)],
        compiler_params=_params("parallel", "arbitrary"),
        name="fox_attn",
    )(p3, p3, p3, fcol, frow)


def _rope_pairs(x, cos, sin, half, period):
    lane = lax.broadcasted_iota(jnp.int32, x.shape, x.ndim - 1)
    first = (lane & (period - 1)) < half
    rot = jnp.where(first, -pltpu.roll(x, LANES - half, x.ndim - 1), pltpu.roll(x, half, x.ndim - 1))
    return x * cos + rot * sin


def _mla_prep_kernel(p_ref, gq_ref, gkv_ref, wq_ref, wkv_ref, cos_ref, sin_ref,
                     q_ref, k_ref, v_ref, *, qscale):
    blk = p_ref[0].astype(F32)
    cos = cos_ref[0]
    sin = sin_ref[0]
    cq = blk[:, 0:MLA_Q_RANK]
    ckv = blk[:, MLA_Q_RANK:MLA_Q_RANK + MLA_KV_RANK]
    kr = blk[:, MLA_Q_RANK + MLA_KV_RANK:MLA_Q_RANK + MLA_KV_RANK + LANES]
    q = _dot(_rms(cq, gq_ref[...]).astype(BF16), wq_ref[...])
    kv = _dot(_rms(ckv, gkv_ref[...]).astype(BF16), wkv_ref[...])
    kr = _rope_pairs(kr, cos, sin, MLA_ROPE // 2, LANES).astype(BF16)
    for h in range(GROUP_HEADS):
        o = h * MLA_QK_PAD
        q_ref[0, :, o:o + LANES] = (q[:, o:o + LANES] * qscale).astype(BF16)
        q_ref[0, :, o + LANES:o + 2 * LANES] = (_rope_pairs(
            q[:, o + LANES:o + 2 * LANES], cos, sin, MLA_ROPE // 2, LANES) * qscale).astype(BF16)
        k_ref[0, :, o:o + LANES] = kv[:, o:o + LANES].astype(BF16)
        k_ref[0, :, o + LANES:o + 2 * LANES] = kr
        v_ref[0, :, h * MLA_V:(h + 1) * MLA_V] = kv[:, o + LANES:o + 2 * LANES].astype(BF16)


def _mla_prep(p3, g_cq, g_ckv, w_uq, w_ukv, cos, sin, l, tm=512):
    b, s, _ = p3.shape
    c0 = P_MLA // MLA_BLOCK_W
    hq = GROUP_HEADS * MLA_QK_PAD
    qscale = (MLA_NOPE + MLA_ROPE) ** -0.5 * LOG2E
    return pl.pallas_call(
        functools.partial(_mla_prep_kernel, qscale=qscale),
        grid=(b, s // tm),
        in_specs=[pl.BlockSpec((1, tm, MLA_BLOCK_W), lambda i, j: (i, j, c0)),
                  pl.BlockSpec((1, MLA_Q_RANK), lambda i, j: (0, 0)),
                  pl.BlockSpec((1, MLA_KV_RANK), lambda i, j: (0, 0)),
                  pl.BlockSpec((None, MLA_Q_RANK, hq), lambda i, j: (l, 0, 0)),
                  pl.BlockSpec((None, MLA_KV_RANK, hq), lambda i, j: (l, 0, 0)),
                  pl.BlockSpec((1, tm, LANES), lambda i, j: (i, j, 0)),
                  pl.BlockSpec((1, tm, LANES), lambda i, j: (i, j, 0))],
        out_specs=[pl.BlockSpec((1, tm, hq), lambda i, j: (i, j, 0)),
                   pl.BlockSpec((1, tm, hq), lambda i, j: (i, j, 0)),
                   pl.BlockSpec((1, tm, GROUP_W), lambda i, j: (i, j, 0))],
        out_shape=[jax.ShapeDtypeStruct((b, s, hq), BF16),
                   jax.ShapeDtypeStruct((b, s, hq), BF16),
                   jax.ShapeDtypeStruct((b, s, GROUP_W), BF16)],
        compiler_params=_params("parallel", "parallel"),
        name="mla_prep",
    )(p3, g_cq, g_ckv, w_uq, w_ukv, cos, sin)


def _mla_attn_kernel(q_ref, k_ref, v_ref, o_ref, st_ref, *, tq):
    qi = pl.program_id(1)
    qs = [q_ref[0, :, h * MLA_QK_PAD:(h + 1) * MLA_QK_PAD] for h in range(GROUP_HEADS)]
    res = _flash(qs, k_ref, v_ref, st_ref, MLA_QK_PAD, MLA_V, qi, tq)
    for h, (acc, l) in enumerate(res):
        o_ref[0, :, h * MLA_V:(h + 1) * MLA_V] = (acc / l).astype(o_ref.dtype)


def _mla_attn(qm, km, vm, tq=512):
    b, s, hq = qm.shape
    return pl.pallas_call(
        functools.partial(_mla_attn_kernel, tq=tq),
        grid=(b, s // tq),
        in_specs=[pl.BlockSpec((1, tq, hq), lambda i, j: (i, j, 0)),
                  pl.BlockSpec((1, s, hq), lambda i, j: (i, 0, 0)),
                  pl.BlockSpec((1, s, GROUP_W), lambda i, j: (i, 0, 0))],
        out_specs=pl.BlockSpec((1, tq, GROUP_W), lambda i, j: (i, j, 0)),
        out_shape=jax.ShapeDtypeStruct((b, s, GROUP_W), BF16),
        scratch_shapes=[_flash_state(tq)],
        compiler_params=_params("parallel", "arbitrary"),
        name="mla_attn",
    )(qm, km, vm)


def _diff_prep_kernel(q_ref, k_ref, cos_ref, sin_ref, qo_ref, ko_ref, *, qscale):
    cos = cos_ref[0]
    sin = sin_ref[0]
    for src, dst, scale in ((q_ref, qo_ref, qscale), (k_ref, ko_ref, None)):
        for c in range(GROUP_W // LANES):
            x = src[0, :, c * LANES:(c + 1) * LANES].astype(F32)
            y = _rope_pairs(x, cos, sin, DIFF_ROT // 2, DIFF_D)
            if scale is not None:
                y = y * scale
            dst[0, :, c * LANES:(c + 1) * LANES] = y.astype(BF16)


def _diff_prep(p3, cos, sin, tm=1024):
    b, s, _ = p3.shape
    c0 = P_DIFF // GROUP_W
    return pl.pallas_call(
        functools.partial(_diff_prep_kernel, qscale=DIFF_D ** -0.5 * LOG2E),
        grid=(b, s // tm),
        in_specs=[pl.BlockSpec((1, tm, GROUP_W), lambda i, j: (i, j, c0)),
                  pl.BlockSpec((1, tm, GROUP_W), lambda i, j: (i, j, c0 + 1)),
                  pl.BlockSpec((1, tm, LANES), lambda i, j: (i, j, 0)),
                  pl.BlockSpec((1, tm, LANES), lambda i, j: (i, j, 0))],
        out_specs=[pl.BlockSpec((1, tm, GROUP_W), lambda i, j: (i, j, 0)),
                   pl.BlockSpec((1, tm, GROUP_W), lambda i, j: (i, j, 0))],
        out_shape=[jax.ShapeDtypeStruct((b, s, GROUP_W), BF16),
                   jax.ShapeDtypeStruct((b, s, GROUP_W), BF16)],
        compiler_params=_params("parallel", "parallel"),
        name="diff_prep",
    )(p3, p3, cos, sin)


def _diff_attn_kernel(q_ref, k_ref, v_ref, lam_ref, g_ref, o_ref, st_ref, *, tq, lam_init):
    qi = pl.program_id(1)
    lv = lam_ref[...]
    lam = (jnp.exp(jnp.sum(lv[0:1] * lv[1:2], axis=-1, keepdims=True))
           - jnp.exp(jnp.sum(lv[2:3] * lv[3:4], axis=-1, keepdims=True)) + lam_init)
    lane = lax.broadcasted_iota(jnp.int32, (tq, HEAD_DIM), 1)
    qs = []
    for h in range(GROUP_HEADS):
        q = q_ref[0, :, h * HEAD_DIM:(h + 1) * HEAD_DIM]
        zero = jnp.zeros_like(q)
        qs.append(jnp.concatenate([jnp.where(lane < DIFF_D, q, zero),
                                   jnp.where(lane >= DIFF_D, q, zero)], axis=0))
    res = _flash(qs, k_ref, v_ref, st_ref, HEAD_DIM, HEAD_DIM, qi, tq)
    for h, (acc, l) in enumerate(res):
        o = acc / l
        o = o[0:tq] - lam * o[tq:2 * tq]
        o = _rms(o, g_ref[...]) * (1.0 - lam_init)
        o_ref[0, :, h * HEAD_DIM:(h + 1) * HEAD_DIM] = o.astype(o_ref.dtype)


def _diff_attn(qr, kr, p3, lamv, g_diff, lam_init, tq=512):
    b, s, w = qr.shape
    cv = P_DIFF // w + 2
    return pl.pallas_call(
        functools.partial(_diff_attn_kernel, tq=tq, lam_init=lam_init),
        grid=(b, s // tq),
        in_specs=[pl.BlockSpec((1, tq, w), lambda i, j: (i, j, 0)),
                  pl.BlockSpec((1, s, w), lambda i, j: (i, 0, 0)),
                  pl.BlockSpec((1, s, w), lambda i, j: (i, 0, cv)),
                  pl.BlockSpec((4, DIFF_D), lambda i, j: (0, 0)),
                  pl.BlockSpec((1, HEAD_DIM), lambda i, j: (0, 0))],
        out_specs=pl.BlockSpec((1, tq, w), lambda i, j: (i, j, 0)),
        out_shape=jax.ShapeDtypeStruct((b, s, w), BF16),
        scratch_shapes=[_flash_state(2 * tq)],
        compiler_params=_params("parallel", "arbitrary"),
        name="diff_attn",
    )(qr, kr, p3, lamv, g_diff)


def _sgu_kernel(uv_ref, lg_ref, lb_ref, ws_ref, bs_ref, o_ref, *, tm):
    z = uv_ref[...].astype(F32)
    z = 0.5 * z * (1.0 + jnp.tanh(math.sqrt(2.0 / math.pi) * (z + 0.044715 * (z * z * z))))
    u = z[:, 0:GROUP_W]
    v = z[:, GROUP_W:2 * GROUP_W]
    xc = v - jnp.mean(v, axis=-1, keepdims=True)
    vn = xc * lax.rsqrt(jnp.mean(xc * xc, axis=-1, keepdims=True) + EPS) * lg_ref[...] + lb_ref[...]
    vn = vn.astype(BF16)
    t = SGU_CHUNK
    tril = (lax.broadcasted_iota(jnp.int32, (t, t), 1) <= lax.broadcasted_iota(jnp.int32, (t, t), 0))
    for g in range(GROUP_HEADS):
        w = jnp.where(tril, ws_ref[g], 0.0).astype(BF16)
        bias = bs_ref[:, g:g + 1]
        for c in range(tm // t):
            rows = slice(c * t, (c + 1) * t)
            cols = slice(g * HEAD_DIM, (g + 1) * HEAD_DIM)
            mixed = _dot(w, vn[rows, cols]) + bias
            o_ref[rows, cols] = (u[rows, cols] * mixed).astype(o_ref.dtype)


def _sgu(p, ln_g, ln_b, w_s, b_st, l, tm=512):
    t = p.shape[0]
    c0 = P_SGU // (2 * GROUP_W)
    return pl.pallas_call(
        functools.partial(_sgu_kernel, tm=tm),
        grid=(t // tm,),
        in_specs=[pl.BlockSpec((tm, 2 * GROUP_W), lambda i: (i, c0)),
                  pl.BlockSpec((1, GROUP_W), lambda i: (0, 0)),
                  pl.BlockSpec((1, GROUP_W), lambda i: (0, 0)),
                  pl.BlockSpec((None, GROUP_HEADS, SGU_CHUNK, SGU_CHUNK), lambda i: (l, 0, 0, 0)),
                  pl.BlockSpec((None, SGU_CHUNK, GROUP_HEADS), lambda i: (l, 0, 0))],
        out_specs=pl.BlockSpec((tm, GROUP_W), lambda i: (i, 0)),
        out_shape=jax.ShapeDtypeStruct((t, GROUP_W), BF16),
        compiler_params=_params("parallel"),
        name="sgu",
    )(p, ln_g, ln_b, w_s, b_st)


def _mix_cross_kernel(x_ref, ya_ref, yb_ref, yc_ref, yd_ref, wo_ref, g_ref, wq_ref, kv_ref,
                      wco_ref, o_ref, oh_ref, *, scale):
    acc = x_ref[...]
    for m, y_ref in enumerate((ya_ref, yb_ref, yc_ref, yd_ref)):
        acc = acc + _dot(y_ref[...], wo_ref[m * GROUP_W:(m + 1) * GROUP_W, :])
    o_ref[...] = acc
    q = _dot(_rms(o_ref[...], g_ref[...]).astype(BF16), wq_ref[...]).astype(BF16)
    for h in range(CROSS_HEADS):
        cols = slice(h * CROSS_HEAD_DIM, (h + 1) * CROSS_HEAD_DIM)
        k = kv_ref[0, :, cols]
        v = kv_ref[0, :, CROSS_W + h * CROSS_HEAD_DIM:CROSS_W + (h + 1) * CROSS_HEAD_DIM]
        s = _dot_nt(q[:, cols], k) * scale
        p = jnp.exp(s - jnp.max(s, axis=-1, keepdims=True))
        l = jnp.sum(p, axis=-1, keepdims=True)
        oh_ref[:, cols] = (_dot(p.astype(BF16), v) / l).astype(BF16)
    o_ref[...] += _dot(oh_ref[...], wco_ref[...])


def _mix_cross(x, ys, w_o, g, wq, kv, wco, seq, l, tm=512):
    t, d = x.shape
    per_b = seq // tm
    m = kv.shape[1]
    once = pl.Buffered(1)
    yspec = pl.BlockSpec((tm, GROUP_W), lambda i: (i, 0))
    return pl.pallas_call(
        functools.partial(_mix_cross_kernel, scale=CROSS_HEAD_DIM ** -0.5),
        grid=(t // tm,),
        in_specs=[pl.BlockSpec((tm, d), lambda i: (i, 0)), yspec, yspec, yspec, yspec,
                  pl.BlockSpec((None, w_o.shape[1], d), lambda i: (l, 0, 0), pipeline_mode=once),
                  pl.BlockSpec((1, d), lambda i: (0, 0)),
                  pl.BlockSpec((None, d, CROSS_W), lambda i: (l, 0, 0), pipeline_mode=once),
                  pl.BlockSpec((1, m, 2 * CROSS_W), lambda i: (i // per_b, 0, 0)),
                  pl.BlockSpec((None, CROSS_W, d), lambda i: (l, 0, 0), pipeline_mode=once)],
        out_specs=pl.BlockSpec((tm, d), lambda i: (i, 0)),
        out_shape=jax.ShapeDtypeStruct((t, d), F32),
        scratch_shapes=[pltpu.VMEM((tm, CROSS_W), BF16)],
        compiler_params=_params("parallel"),
        name="mix_cross",
    )(x, *ys, w_o, g, wq, kv, wco)


def _ffn_kernel(x_ref, g_ref, wg_ref, wu_ref, wd_ref, gf_ref, o_ref, xn_ref, acc_ref, *, final):
    j = pl.program_id(1)

    @pl.when(j == 0)
    def _():
        xn_ref[...] = _rms(x_ref[...], g_ref[...]).astype(BF16)
        acc_ref[...] = jnp.zeros_like(acc_ref)

    xn = xn_ref[...]
    gate = _dot(xn, wg_ref[...])
    up = _dot(xn, wu_ref[...])
    hid = (gate * jax.nn.sigmoid(gate) * up).astype(BF16)
    acc_ref[...] += _dot(hid, wd_ref[...])

    @pl.when(j == pl.num_programs(1) - 1)
    def _():
        y = x_ref[...] + acc_ref[...]
        o_ref[...] = _rms(y, gf_ref[...]) if final else y


def _ffn(x, g, wg, wu, wd, g_final, final, l, tm=512, th=512):
    t, d = x.shape
    hdim = wg.shape[2]
    return pl.pallas_call(
        functools.partial(_ffn_kernel, final=final),
        grid=(t // tm, hdim // th),
        in_specs=[pl.BlockSpec((tm, d), lambda i, j: (i, 0)),
                  pl.BlockSpec((1, d), lambda i, j: (0, 0)),
                  pl.BlockSpec((None, d, th), lambda i, j: (l, 0, j)),
                  pl.BlockSpec((None, d, th), lambda i, j: (l, 0, j)),
                  pl.BlockSpec((None, th, d), lambda i, j: (l, j, 0)),
                  pl.BlockSpec((1, d), lambda i, j: (0, 0))],
        out_specs=pl.BlockSpec((tm, d), lambda i, j: (i, 0)),
        out_shape=jax.ShapeDtypeStruct((t, d), F32),
        scratch_shapes=[pltpu.VMEM((tm, d), BF16), pltpu.VMEM((tm, d), F32)],
        compiler_params=_params("parallel", "arbitrary"),
        name="swiglu",
    )(x, g, wg, wu, wd, g_final)


def _rope_tables(positions):
    pos = positions.astype(F32)[..., None]
    b, s = positions.shape

    def table(rot_dim, period):
        freqs = ROPE_THETA ** (-jnp.arange(0, rot_dim, 2, dtype=F32) / rot_dim)
        ang = pos * freqs
        ones = jnp.ones((b, s, period - rot_dim), F32)
        cos = jnp.concatenate([jnp.cos(ang), jnp.cos(ang), ones], axis=-1)
        sin = jnp.concatenate([jnp.sin(ang), jnp.sin(ang), 0.0 * ones], axis=-1)
        reps = LANES // period
        return jnp.tile(cos, (1, 1, reps)), jnp.tile(sin, (1, 1, reps))

    return table(MLA_ROPE, LANES), table(DIFF_ROT, DIFF_D)


_SRC_FORGET = 3 * GROUP_W
_SRC_MLA = _SRC_FORGET + GROUP_HEADS
_MLA_SRC_W = MLA_Q_RANK + MLA_KV_RANK + MLA_ROPE
_SRC_SGU = _SRC_MLA + _MLA_SRC_W
_SRC_DIFF = _SRC_SGU + 2 * GROUP_W


def _split_w_in_kernel(w_ref, main_ref, wf_ref):
    copies = ((P_FOX, 0, 3 * GROUP_W), (P_DIFF, _SRC_DIFF, 3 * GROUP_W),
              (P_SGU, _SRC_SGU, 2 * GROUP_W), (P_MLA, _SRC_MLA, _MLA_SRC_W))
    for dst, src, width in copies:
        main_ref[0, :, dst:dst + width] = w_ref[0, :, src:src + width].astype(BF16)
    rows = w_ref.shape[1]
    main_ref[0, :, P_MLA + _MLA_SRC_W:P_W] = jnp.zeros((rows, P_W - P_MLA - _MLA_SRC_W), BF16)
    f = w_ref[0, :, _SRC_FORGET:_SRC_FORGET + LANES]
    lane = lax.broadcasted_iota(jnp.int32, f.shape, 1)
    wf_ref[0] = jnp.where(lane < GROUP_HEADS, f, 0.0).astype(BF16)


def _split_w_in(w, tk=256):
    nl, k, n = w.shape
    return pl.pallas_call(
        _split_w_in_kernel,
        grid=(nl, k // tk),
        in_specs=[pl.BlockSpec((1, tk, n), lambda l, i: (l, i, 0))],
        out_specs=[pl.BlockSpec((1, tk, P_W), lambda l, i: (l, i, 0)),
                   pl.BlockSpec((1, tk, LANES), lambda l, i: (l, i, 0))],
        out_shape=[jax.ShapeDtypeStruct((nl, k, P_W), BF16),
                   jax.ShapeDtypeStruct((nl, k, LANES), BF16)],
        compiler_params=_params("parallel", "parallel"),
        name="split_w_in",
    )(w)


def _pad_heads(w, used, width):
    nl, k, _ = w.shape
    w = w.reshape(nl, k, GROUP_HEADS, used)
    w = jnp.pad(w, ((0, 0), (0, 0), (0, 0), (0, width - used)))
    return w.reshape(nl, k, GROUP_HEADS * width)


def kernel(x, mem, positions, g_mix, w_in, b_f, g_cq, g_ckv, w_uq, w_ukv, sgu_ln_g, sgu_ln_b, w_s,
           b_s, lam_q1, lam_k1, lam_q2, lam_k2, g_diff, w_o, g_mem, g_cross, w_cq, w_ck, w_cv, w_co,
           g_ffn, w_gate, w_up, w_down, g_final):
    b, s, d = x.shape
    depth = w_in.shape[0]
    t = b * s
    (cos_m, sin_m), (cos_d, sin_d) = _rope_tables(positions)
    xf = x.reshape(t, d)
    memf = mem.reshape(b * mem.shape[1], d)
    row = lambda a: a.reshape(1, -1)

    w_main, w_f = _split_w_in(w_in)
    w_uq_p = _pad_heads(w_uq, MLA_NOPE + MLA_ROPE, MLA_QK_PAD).astype(BF16)
    w_ukv_b = w_ukv.astype(BF16)
    w_o_b, w_cq_b, w_ck_b, w_cv_b, w_co_b, w_gate_b, w_up_b, w_down_b = (
        _cast_bf16(w) for w in (w_o, w_cq, w_ck, w_cv, w_co, w_gate, w_up, w_down))
    b_st = jnp.swapaxes(b_s, 1, 2)
    col_scale = jnp.ones((1, P_W), F32).at[:, P_FOX:P_FOX + GROUP_W].set(HEAD_DIM ** -0.5 * LOG2E)

    for l in range(depth):
        lam_init = 0.8 - 0.6 * math.exp(-0.3 * l)
        p, fb = _in_proj(xf, row(g_mix[l]), w_main, w_f, col_scale, l)
        p3 = p.reshape(b, s, P_W)

        bf_pad = jnp.pad(b_f[l], (0, LANES - GROUP_HEADS)).reshape(1, LANES)
        fcol, frow = _fox_gate(fb.reshape(b, s, LANES), bf_pad)
        y_a = _fox_attn(p3, fcol, frow)

        qm, km, vm = _mla_prep(p3, row(g_cq[l]), row(g_ckv[l]), w_uq_p, w_ukv_b, cos_m, sin_m, l)
        y_b = _mla_attn(qm, km, vm)

        y_c = _sgu(p, row(sgu_ln_g[l]), row(sgu_ln_b[l]), w_s, b_st, l)

        qr, kr = _diff_prep(p3, cos_d, sin_d)
        lamv = jnp.stack([lam_q1[l], lam_k1[l], lam_q2[l], lam_k2[l]])
        y_d = _diff_attn(qr, kr, p3, lamv, row(g_diff[l]), lam_init)

        ys = [y.reshape(t, GROUP_W) for y in (y_a, y_b, y_c, y_d)]
        kv = _mem_kv(memf, row(g_mem), w_ck_b, w_cv_b, l).reshape(b, mem.shape[1], 2 * CROSS_W)
        xf = _mix_cross(xf, ys, w_o_b, row(g_cross[l]), w_cq_b, kv, w_co_b, s, l)

        xf = _ffn(xf, row(g_ffn[l]), w_gate_b, w_up_b, w_down_b, row(g_final),
                  final=(l == depth - 1), l=l)
    return xf.reshape(b, s, d)
```

```python
import functools
import math

import jax
import jax.numpy as jnp
from jax import lax
from jax.experimental import pallas as pl
from jax.experimental.pallas import tpu as pltpu

F32 = jnp.float32
BF16 = jnp.bfloat16

HEAD_DIM = 128
GROUP_HEADS = 4
GROUP_W = GROUP_HEADS * HEAD_DIM
MLA_Q_RANK = 384
MLA_KV_RANK = 256
MLA_NOPE = 128
MLA_ROPE = 64
MLA_V = 128
MLA_QK_PAD = 256
SGU_CHUNK = 128
DIFF_D = HEAD_DIM // 2
DIFF_ROT = DIFF_D // 4
CROSS_HEADS = 4
CROSS_HEAD_DIM = 128
CROSS_W = CROSS_HEADS * CROSS_HEAD_DIM
ROPE_THETA = 500000.0
EPS = 1e-6
LOG2E = math.log2(math.e)
LANES = 128

P_FOX = 0
P_DIFF = 3 * GROUP_W
P_SGU = 6 * GROUP_W
P_MLA = 8 * GROUP_W
MLA_SRC_W = MLA_Q_RANK + MLA_KV_RANK + MLA_ROPE
MLA_BLOCK_W = MLA_SRC_W + LANES - MLA_ROPE
P_W = P_MLA + MLA_BLOCK_W

VMEM_LIMIT = 52 * 2**20


def _params(*sem):
    return pltpu.CompilerParams(dimension_semantics=sem, vmem_limit_bytes=VMEM_LIMIT)


def _rms(x, g):
    return x * lax.rsqrt(jnp.mean(x * x, axis=-1, keepdims=True) + EPS) * g


def _dot(a, b):
    return jnp.dot(a, b, preferred_element_type=F32)


def _dot_nt(a, b):
    return lax.dot_general(a, b, (((1,), (1,)), ((), ())), preferred_element_type=F32)


def _cast_kernel(w_ref, o_ref):
    o_ref[...] = w_ref[...].astype(o_ref.dtype)


def _cast_bf16(w, block_bytes=4 * 2**20):
    nl, k, n = w.shape
    tk = k
    while tk * n * 4 > block_bytes and tk % 16 == 0:
        tk //= 2
    return pl.pallas_call(
        _cast_kernel,
        grid=(nl, k // tk),
        in_specs=[pl.BlockSpec((1, tk, n), lambda l, i: (l, i, 0))],
        out_specs=pl.BlockSpec((1, tk, n), lambda l, i: (l, i, 0)),
        out_shape=jax.ShapeDtypeStruct(w.shape, BF16),
        compiler_params=_params("parallel", "parallel"),
        name="cast_bf16",
    )(w)


_SRC_FORGET = 3 * GROUP_W
_SRC_MLA = _SRC_FORGET + GROUP_HEADS
_SRC_SGU = _SRC_MLA + MLA_SRC_W
_SRC_DIFF = _SRC_SGU + 2 * GROUP_W


def _split_w_in_kernel(w_ref, main_ref, wf_ref):
    copies = ((P_FOX, 0, 3 * GROUP_W), (P_DIFF, _SRC_DIFF, 3 * GROUP_W),
              (P_SGU, _SRC_SGU, 2 * GROUP_W), (P_MLA, _SRC_MLA, MLA_SRC_W))
    for dst, src, width in copies:
        main_ref[0, :, dst:dst + width] = w_ref[0, :, src:src + width].astype(BF16)
    rows = w_ref.shape[1]
    main_ref[0, :, P_MLA + MLA_SRC_W:P_W] = jnp.zeros((rows, P_W - P_MLA - MLA_SRC_W), BF16)
    f = w_ref[0, :, _SRC_FORGET:_SRC_FORGET + LANES]
    lane = lax.broadcasted_iota(jnp.int32, f.shape, 1)
    wf_ref[0] = jnp.where(lane < GROUP_HEADS, f, 0.0).astype(BF16)


def _split_w_in(w, tk=256):
    nl, k, n = w.shape
    return pl.pallas_call(
        _split_w_in_kernel,
        grid=(nl, k // tk),
        in_specs=[pl.BlockSpec((1, tk, n), lambda l, i: (l, i, 0))],
        out_specs=[pl.BlockSpec((1, tk, P_W), lambda l, i: (l, i, 0)),
                   pl.BlockSpec((1, tk, LANES), lambda l, i: (l, i, 0))],
        out_shape=[jax.ShapeDtypeStruct((nl, k, P_W), BF16),
                   jax.ShapeDtypeStruct((nl, k, LANES), BF16)],
        compiler_params=_params("parallel", "parallel"),
        name="split_w_in",
    )(w)


def _rope_pairs(x, cos, sin, half, period):
    lane = lax.broadcasted_iota(jnp.int32, x.shape, x.ndim - 1)
    first = (lane & (period - 1)) < half
    rot = jnp.where(first, -pltpu.roll(x, LANES - half, x.ndim - 1), pltpu.roll(x, half, x.ndim - 1))
    return x * cos + rot * sin


def _sgu_mix(uv, lg_ref, lb_ref, ws_ref, bs_ref, o_ref):
    z = 0.5 * uv * (1.0 + jnp.tanh(math.sqrt(2.0 / math.pi) * (uv + 0.044715 * (uv * uv * uv))))
    u = z[:, 0:GROUP_W]
    v = z[:, GROUP_W:2 * GROUP_W]
    xc = v - jnp.mean(v, axis=-1, keepdims=True)
    vn = xc * lax.rsqrt(jnp.mean(xc * xc, axis=-1, keepdims=True) + EPS) * lg_ref[...] + lb_ref[...]
    vn = vn.astype(BF16)
    t = SGU_CHUNK
    tril = (lax.broadcasted_iota(jnp.int32, (t, t), 1) <= lax.broadcasted_iota(jnp.int32, (t, t), 0))
    for g in range(GROUP_HEADS):
        w = jnp.where(tril, ws_ref[g], 0.0).astype(BF16)
        bias = bs_ref[:, g:g + 1]
        for c in range(uv.shape[0] // t):
            rows = slice(c * t, (c + 1) * t)
            cols = slice(g * HEAD_DIM, (g + 1) * HEAD_DIM)
            mixed = _dot(w, vn[rows, cols]) + bias
            o_ref[rows, cols] = (u[rows, cols] * mixed).astype(o_ref.dtype)


def _mla_expand(lat, gq_ref, gkv_ref, wq_ref, wkv_ref, cos, sin, q_ref, k_ref, v_ref, qscale):
    cq = lat[:, 0:MLA_Q_RANK]
    ckv = lat[:, MLA_Q_RANK:MLA_Q_RANK + MLA_KV_RANK]
    kr = lat[:, MLA_Q_RANK + MLA_KV_RANK:MLA_BLOCK_W]
    q = _dot(_rms(cq, gq_ref[...]).astype(BF16), wq_ref[...])
    kv = _dot(_rms(ckv, gkv_ref[...]).astype(BF16), wkv_ref[...])
    kr = _rope_pairs(kr, cos, sin, MLA_ROPE // 2, LANES).astype(BF16)
    for h in range(GROUP_HEADS):
        o = h * MLA_QK_PAD
        q_ref[:, o:o + LANES] = (q[:, o:o + LANES] * qscale).astype(BF16)
        q_ref[:, o + LANES:o + 2 * LANES] = (_rope_pairs(
            q[:, o + LANES:o + 2 * LANES], cos, sin, MLA_ROPE // 2, LANES) * qscale).astype(BF16)
        k_ref[:, o:o + LANES] = kv[:, o:o + LANES].astype(BF16)
        k_ref[:, o + LANES:o + 2 * LANES] = kr
        v_ref[:, h * MLA_V:(h + 1) * MLA_V] = kv[:, o + LANES:o + 2 * LANES].astype(BF16)


def _front_kernel(x_ref, g_ref, w_ref, wf_ref, cosd_ref, sind_ref, cosm_ref, sinm_ref,
                  lg_ref, lb_ref, ws_ref, bs_ref, gq_ref, gkv_ref, wuq_ref, wukv_ref,
                  fox_ref, fb_ref, dq_ref, dk_ref, dv_ref, yc_ref, qm_ref, km_ref, vm_ref,
                  *, fox_scale, diff_scale, mla_scale):
    xb = _rms(x_ref[...], g_ref[...]).astype(BF16)
    fb_ref[...] = _dot(xb, wf_ref[...])

    fox = _dot(xb, w_ref[:, P_FOX:P_FOX + 3 * GROUP_W])
    fox_ref[:, 0:GROUP_W] = (fox[:, 0:GROUP_W] * fox_scale).astype(BF16)
    fox_ref[:, GROUP_W:3 * GROUP_W] = fox[:, GROUP_W:3 * GROUP_W].astype(BF16)

    d = _dot(xb, w_ref[:, P_DIFF:P_DIFF + 3 * GROUP_W])
    cos = cosd_ref[...]
    sin = sind_ref[...]
    for c in range(GROUP_W // LANES):
        lanes = slice(c * LANES, (c + 1) * LANES)
        klanes = slice(GROUP_W + c * LANES, GROUP_W + (c + 1) * LANES)
        dq_ref[:, lanes] = (_rope_pairs(d[:, lanes], cos, sin, DIFF_ROT // 2, DIFF_D)
                            * diff_scale).astype(BF16)
        dk_ref[:, lanes] = _rope_pairs(d[:, klanes], cos, sin, DIFF_ROT // 2, DIFF_D).astype(BF16)
    dv_ref[...] = d[:, 2 * GROUP_W:3 * GROUP_W].astype(BF16)

    uv = _dot(xb, w_ref[:, P_SGU:P_SGU + 2 * GROUP_W])
    _sgu_mix(uv, lg_ref, lb_ref, ws_ref, bs_ref, yc_ref)

    lat = _dot(xb, w_ref[:, P_MLA:P_MLA + MLA_BLOCK_W])
    _mla_expand(lat, gq_ref, gkv_ref, wuq_ref, wukv_ref, cosm_ref[...], sinm_ref[...],
                qm_ref, km_ref, vm_ref, mla_scale)


def _front(x, g, w_main, w_f, tables, ln_g, ln_b, w_s, b_st, g_cq, g_ckv, w_uq, w_ukv, l, tm=256):
    t, d = x.shape
    hq = GROUP_HEADS * MLA_QK_PAD
    once = pl.Buffered(1)
    rowblk = lambda width: pl.BlockSpec((tm, width), lambda i: (i, 0))
    vec = lambda width: pl.BlockSpec((1, width), lambda i: (0, 0))
    widths = (3 * GROUP_W, LANES, GROUP_W, GROUP_W, GROUP_W, GROUP_W, hq, hq, GROUP_W)
    dtypes = (BF16, F32, BF16, BF16, BF16, BF16, BF16, BF16, BF16)
    return pl.pallas_call(
        functools.partial(_front_kernel, fox_scale=HEAD_DIM ** -0.5 * LOG2E,
                          diff_scale=DIFF_D ** -0.5 * LOG2E,
                          mla_scale=(MLA_NOPE + MLA_ROPE) ** -0.5 * LOG2E),
        grid=(t // tm,),
        in_specs=[rowblk(d), vec(d),
                  pl.BlockSpec((None, d, P_W), lambda i: (l, 0, 0), pipeline_mode=once),
                  pl.BlockSpec((None, d, LANES), lambda i: (l, 0, 0), pipeline_mode=once),
                  rowblk(LANES), rowblk(LANES), rowblk(LANES), rowblk(LANES),
                  vec(GROUP_W), vec(GROUP_W),
                  pl.BlockSpec((None, GROUP_HEADS, SGU_CHUNK, SGU_CHUNK), lambda i: (l, 0, 0, 0)),
                  pl.BlockSpec((None, SGU_CHUNK, GROUP_HEADS), lambda i: (l, 0, 0)),
                  vec(MLA_Q_RANK), vec(MLA_KV_RANK),
                  pl.BlockSpec((None, MLA_Q_RANK, hq), lambda i: (l, 0, 0)),
                  pl.BlockSpec((None, MLA_KV_RANK, hq), lambda i: (l, 0, 0))],
        out_specs=[rowblk(w) for w in widths],
        out_shape=[jax.ShapeDtypeStruct((t, w), dt) for w, dt in zip(widths, dtypes)],
        compiler_params=_params("parallel"),
        name="front",
    )(x, g, w_main, w_f, *tables, ln_g, ln_b, w_s, b_st, g_cq, g_ckv, w_uq, w_ukv)


def _mem_kv_kernel(x_ref, g_ref, wk_ref, wv_ref, o_ref):
    xb = _rms(x_ref[...], g_ref[...]).astype(BF16)
    o_ref[:, 0:CROSS_W] = _dot(xb, wk_ref[...]).astype(o_ref.dtype)
    o_ref[:, CROSS_W:2 * CROSS_W] = _dot(xb, wv_ref[...]).astype(o_ref.dtype)


def _mem_kv(x, g, wk, wv, l, tm=512):
    t, d = x.shape
    wspec = pl.BlockSpec((None, d, CROSS_W), lambda i: (l, 0, 0))
    return pl.pallas_call(
        _mem_kv_kernel,
        grid=(t // tm,),
        in_specs=[pl.BlockSpec((tm, d), lambda i: (i, 0)),
                  pl.BlockSpec((1, d), lambda i: (0, 0)), wspec, wspec],
        out_specs=pl.BlockSpec((tm, 2 * CROSS_W), lambda i: (i, 0)),
        out_shape=jax.ShapeDtypeStruct((t, 2 * CROSS_W), BF16),
        compiler_params=_params("parallel"),
        name="mem_kv_proj",
    )(x, g, wk, wv)


def _fox_gate_kernel(f_ref, b_ref, col_ref, row_ref):
    x = f_ref[0] + b_ref[...]
    c = jnp.minimum(x, 0.0) - jnp.log1p(jnp.exp(-jnp.abs(x)))
    s = x.shape[0]
    ridx = lax.broadcasted_iota(jnp.int32, x.shape, 0)
    shift = 1
    while shift < s:
        c = c + jnp.where(ridx >= shift, pltpu.roll(c, shift, 0), 0.0)
        shift *= 2
    c = c * LOG2E
    col_ref[0] = c
    row_ref[0] = c.T[0:8, :]


def _fox_gate(fb, b_f):
    b, s, _ = fb.shape
    return pl.pallas_call(
        _fox_gate_kernel,
        grid=(b,),
        in_specs=[pl.BlockSpec((1, s, LANES), lambda i: (i, 0, 0)),
                  pl.BlockSpec((1, LANES), lambda i: (0, 0))],
        out_specs=[pl.BlockSpec((1, s, LANES), lambda i: (i, 0, 0)),
                   pl.BlockSpec((1, 8, s), lambda i: (i, 0, 0))],
        out_shape=[jax.ShapeDtypeStruct((b, s, LANES), F32),
                   jax.ShapeDtypeStruct((b, 8, s), F32)],
        compiler_params=_params("parallel"),
        name="fox_gate",
    )(fb, b_f)


def _flash(qs, k_ref, v_ref, st_ref, dk, dv, qi, tq, fqs=None, frow_ref=None):
    nh = len(qs)
    r = qs[0].shape[0]
    assert dv == LANES and tq % (2 * LANES) == 0
    if fqs is not None:
        fqs = [jnp.broadcast_to(f, (r, LANES)) for f in fqs]

    def scores(h, rows, start, nk, diag_row0=None):
        s = _dot_nt(qs[h][rows], k_ref[0, pl.ds(start, nk), h * dk:(h + 1) * dk])
        if frow_ref is not None:
            s = s - frow_ref[0, h:h + 1, pl.ds(start, nk)]
        if diag_row0 is not None:
            row = lax.broadcasted_iota(jnp.int32, s.shape, 0) + diag_row0
            col = lax.broadcasted_iota(jnp.int32, s.shape, 1)
            s = jnp.where(col <= row, s, -jnp.inf)
        return s

    def absorb(s, h, rows, start, old):
        nr, nk = s.shape
        mz = jnp.broadcast_to(jnp.max(s, axis=-1, keepdims=True), (nr, LANES))
        if fqs is not None:
            mz = mz + fqs[h][rows]
        if old is None:
            m_new = mz
        else:
            m_new = jnp.maximum(old[0], mz)
            alpha = jnp.exp2(old[0] - m_new)
        shift = (fqs[h][rows] - m_new) if fqs is not None else -m_new
        p = jnp.exp2(s + jnp.concatenate([shift] * (nk // LANES), axis=1))
        psum = p[:, 0:LANES]
        for g in range(1, nk // LANES):
            psum = psum + p[:, g * LANES:(g + 1) * LANES]
        pv = _dot(p.astype(BF16), v_ref[0, pl.ds(start, nk), h * dv:(h + 1) * dv])
        if old is None:
            return m_new, psum, pv
        return m_new, alpha * old[1] + psum, alpha * old[2] + pv

    half = tq // 2
    diag_start = pl.multiple_of(qi * tq, tq)
    pieces = []
    for rep in range(r // tq):
        pieces.append((slice(rep * tq, rep * tq + half), half, 0))
        pieces.append((slice(rep * tq + half, (rep + 1) * tq), tq, half))
    diag = [[scores(h, rows, diag_start, nk, row0) for rows, nk, row0 in pieces] for h in range(nh)]
    for h in range(nh):
        parts = [absorb(s, h, rows, diag_start, None) for s, (rows, _, _) in zip(diag[h], pieces)]
        for i in range(3):
            st_ref[i, h] = jnp.concatenate([part[i] for part in parts], axis=0)

    def body(j, c):
        start = pl.multiple_of(j * tq, tq)
        every = slice(0, r)
        blocks = [scores(h, every, start, tq) for h in range(nh)]
        for h in range(nh):
            new = absorb(blocks[h], h, every, start, (st_ref[0, h], st_ref[1, h], st_ref[2, h]))
            for i in range(3):
                st_ref[i, h] = new[i]
        return c

    lax.fori_loop(0, qi, body, 0)
    return [(st_ref[2, h], jnp.sum(st_ref[1, h], axis=-1, keepdims=True)) for h in range(nh)]


def _flash_state(rows):
    return pltpu.VMEM((3, GROUP_HEADS, rows, LANES), F32)


def _fox_attn_kernel(q_ref, k_ref, v_ref, fcol_ref, frow_ref, o_ref, st_ref, *, tq):
    qi = pl.program_id(1)
    qs = [q_ref[0, :, h * HEAD_DIM:(h + 1) * HEAD_DIM] for h in range(GROUP_HEADS)]
    fqs = [fcol_ref[0, :, h:h + 1] for h in range(GROUP_HEADS)]
    res = _flash(qs, k_ref, v_ref, st_ref, HEAD_DIM, HEAD_DIM, qi, tq, fqs, frow_ref)
    for h, (acc, l) in enumerate(res):
        o_ref[0, :, h * HEAD_DIM:(h + 1) * HEAD_DIM] = (acc / l).astype(o_ref.dtype)


def _fox_attn(qkv, fcol, frow, tq=512):
    b, s, _ = qkv.shape
    w = GROUP_W
    return pl.pallas_call(
        functools.partial(_fox_attn_kernel, tq=tq),
        grid=(b, s // tq),
        in_specs=[pl.BlockSpec((1, tq, w), lambda i, j: (i, j, 0)),
                  pl.BlockSpec((1, s, w), lambda i, j: (i, 0, 1)),
                  pl.BlockSpec((1, s, w), lambda i, j: (i, 0, 2)),
                  pl.BlockSpec((1, tq, LANES), lambda i, j: (i, j, 0)),
                  pl.BlockSpec((1, 8, s), lambda i, j: (i, 0, 0))],
        out_specs=pl.BlockSpec((1, tq, w), lambda i, j: (i, j, 0)),
        out_shape=jax.ShapeDtypeStruct((b, s, w), BF16),
        scratch_shapes=[_flash_state(tq)],
        compiler_params=_params("parallel", "arbitrary"),
        name="fox_attn",
    )(qkv, qkv, qkv, fcol, frow)


def _mla_attn_kernel(q_ref, k_ref, v_ref, o_ref, st_ref, *, tq):
    qi = pl.program_id(1)
    qs = [q_ref[0, :, h * MLA_QK_PAD:(h + 1) * MLA_QK_PAD] for h in range(GROUP_HEADS)]
    res = _flash(qs, k_ref, v_ref, st_ref, MLA_QK_PAD, MLA_V, qi, tq)
    for h, (acc, l) in enumerate(res):
        o_ref[0, :, h * MLA_V:(h + 1) * MLA_V] = (acc / l).astype(o_ref.dtype)


def _mla_attn(qm, km, vm, tq=512):
    b, s, hq = qm.shape
    return pl.pallas_call(
        functools.partial(_mla_attn_kernel, tq=tq),
        grid=(b, s // tq),
        in_specs=[pl.BlockSpec((1, tq, hq), lambda i, j: (i, j, 0)),
                  pl.BlockSpec((1, s, hq), lambda i, j: (i, 0, 0)),
                  pl.BlockSpec((1, s, GROUP_W), lambda i, j: (i, 0, 0))],
        out_specs=pl.BlockSpec((1, tq, GROUP_W), lambda i, j: (i, j, 0)),
        out_shape=jax.ShapeDtypeStruct((b, s, GROUP_W), BF16),
        scratch_shapes=[_flash_state(tq)],
        compiler_params=_params("parallel", "arbitrary"),
        name="mla_attn",
    )(qm, km, vm)


def _diff_attn_kernel(q_ref, k_ref, v_ref, lam_ref, g_ref, o_ref, st_ref, *, tq, lam_init):
    qi = pl.program_id(1)
    lv = lam_ref[...]
    lam = (jnp.exp(jnp.sum(lv[0:1] * lv[1:2], axis=-1, keepdims=True))
           - jnp.exp(jnp.sum(lv[2:3] * lv[3:4], axis=-1, keepdims=True)) + lam_init)
    lane = lax.broadcasted_iota(jnp.int32, (tq, HEAD_DIM), 1)
    qs = []
    for h in range(GROUP_HEADS):
        q = q_ref[0, :, h * HEAD_DIM:(h + 1) * HEAD_DIM]
        zero = jnp.zeros_like(q)
        qs.append(jnp.concatenate([jnp.where(lane < DIFF_D, q, zero),
                                   jnp.where(lane >= DIFF_D, q, zero)], axis=0))
    res = _flash(qs, k_ref, v_ref, st_ref, HEAD_DIM, HEAD_DIM, qi, tq)
    for h, (acc, l) in enumerate(res):
        o = acc / l
        o = o[0:tq] - lam * o[tq:2 * tq]
        o = _rms(o, g_ref[...]) * (1.0 - lam_init)
        o_ref[0, :, h * HEAD_DIM:(h + 1) * HEAD_DIM] = o.astype(o_ref.dtype)


def _diff_attn(qr, kr, v, lamv, g_diff, lam_init, tq=512):
    b, s, w = qr.shape
    return pl.pallas_call(
        functools.partial(_diff_attn_kernel, tq=tq, lam_init=lam_init),
        grid=(b, s // tq),
        in_specs=[pl.BlockSpec((1, tq, w), lambda i, j: (i, j, 0)),
                  pl.BlockSpec((1, s, w), lambda i, j: (i, 0, 0)),
                  pl.BlockSpec((1, s, w), lambda i, j: (i, 0, 0)),
                  pl.BlockSpec((4, DIFF_D), lambda i, j: (0, 0)),
                  pl.BlockSpec((1, HEAD_DIM), lambda i, j: (0, 0))],
        out_specs=pl.BlockSpec((1, tq, w), lambda i, j: (i, j, 0)),
        out_shape=jax.ShapeDtypeStruct((b, s, w), BF16),
        scratch_shapes=[_flash_state(2 * tq)],
        compiler_params=_params("parallel", "arbitrary"),
        name="diff_attn",
    )(qr, kr, v, lamv, g_diff)


def _mix_cross_kernel(x_ref, ya_ref, yb_ref, yc_ref, yd_ref, wo_ref, g_ref, wq_ref, kv_ref,
                      wco_ref, o_ref, oh_ref, *, scale):
    acc = x_ref[...]
    for m, y_ref in enumerate((ya_ref, yb_ref, yc_ref, yd_ref)):
        acc = acc + _dot(y_ref[...], wo_ref[m * GROUP_W:(m + 1) * GROUP_W, :])
    o_ref[...] = acc
    q = _dot(_rms(o_ref[...], g_ref[...]).astype(BF16), wq_ref[...]).astype(BF16)
    for h in range(CROSS_HEADS):
        cols = slice(h * CROSS_HEAD_DIM, (h + 1) * CROSS_HEAD_DIM)
        k = kv_ref[0, :, cols]
        v = kv_ref[0, :, CROSS_W + h * CROSS_HEAD_DIM:CROSS_W + (h + 1) * CROSS_HEAD_DIM]
        s = _dot_nt(q[:, cols], k) * scale
        p = jnp.exp(s - jnp.max(s, axis=-1, keepdims=True))
        l = jnp.sum(p, axis=-1, keepdims=True)
        oh_ref[:, cols] = (_dot(p.astype(BF16), v) / l).astype(BF16)
    o_ref[...] += _dot(oh_ref[...], wco_ref[...])


def _mix_cross(x, ys, w_o, g, wq, kv, wco, seq, l, tm=512):
    t, d = x.shape
    per_b = seq // tm
    m = kv.shape[1]
    once = pl.Buffered(1)
    yspec = pl.BlockSpec((tm, GROUP_W), lambda i: (i, 0))
    return pl.pallas_call(
        functools.partial(_mix_cross_kernel, scale=CROSS_HEAD_DIM ** -0.5),
        grid=(t // tm,),
        in_specs=[pl.BlockSpec((tm, d), lambda i: (i, 0)), yspec, yspec, yspec, yspec,
                  pl.BlockSpec((None, w_o.shape[1], d), lambda i: (l, 0, 0), pipeline_mode=once),
                  pl.BlockSpec((1, d), lambda i: (0, 0)),
                  pl.BlockSpec((None, d, CROSS_W), lambda i: (l, 0, 0), pipeline_mode=once),
                  pl.BlockSpec((1, m, 2 * CROSS_W), lambda i: (i // per_b, 0, 0)),
                  pl.BlockSpec((None, CROSS_W, d), lambda i: (l, 0, 0), pipeline_mode=once)],
        out_specs=pl.BlockSpec((tm, d), lambda i: (i, 0)),
        out_shape=jax.ShapeDtypeStruct((t, d), F32),
        scratch_shapes=[pltpu.VMEM((tm, CROSS_W), BF16)],
        compiler_params=_params("parallel"),
        name="mix_cross",
    )(x, *ys, w_o, g, wq, kv, wco)


def _ffn_kernel(x_ref, g_ref, wg_ref, wu_ref, wd_ref, gf_ref, o_ref, xn_ref, acc_ref, *, final):
    j = pl.program_id(1)

    @pl.when(j == 0)
    def _():
        xn_ref[...] = _rms(x_ref[...], g_ref[...]).astype(BF16)
        acc_ref[...] = jnp.zeros_like(acc_ref)

    xn = xn_ref[...]
    gate = _dot(xn, wg_ref[...])
    up = _dot(xn, wu_ref[...])
    hid = (gate * jax.nn.sigmoid(gate) * up).astype(BF16)
    acc_ref[...] += _dot(hid, wd_ref[...])

    @pl.when(j == pl.num_programs(1) - 1)
    def _():
        y = x_ref[...] + acc_ref[...]
        o_ref[...] = _rms(y, gf_ref[...]) if final else y


def _ffn(x, g, wg, wu, wd, g_final, final, l, tm=512, th=512):
    t, d = x.shape
    hdim = wg.shape[2]
    return pl.pallas_call(
        functools.partial(_ffn_kernel, final=final),
        grid=(t // tm, hdim // th),
        in_specs=[pl.BlockSpec((tm, d), lambda i, j: (i, 0)),
                  pl.BlockSpec((1, d), lambda i, j: (0, 0)),
                  pl.BlockSpec((None, d, th), lambda i, j: (l, 0, j)),
                  pl.BlockSpec((None, d, th), lambda i, j: (l, 0, j)),
                  pl.BlockSpec((None, th, d), lambda i, j: (l, j, 0)),
                  pl.BlockSpec((1, d), lambda i, j: (0, 0))],
        out_specs=pl.BlockSpec((tm, d), lambda i, j: (i, 0)),
        out_shape=jax.ShapeDtypeStruct((t, d), F32),
        scratch_shapes=[pltpu.VMEM((tm, d), BF16), pltpu.VMEM((tm, d), F32)],
        compiler_params=_params("parallel", "arbitrary"),
        name="swiglu",
    )(x, g, wg, wu, wd, g_final)


def _rope_tables(positions):
    pos = positions.astype(F32)[..., None]
    b, s = positions.shape

    def table(rot_dim, period):
        freqs = ROPE_THETA ** (-jnp.arange(0, rot_dim, 2, dtype=F32) / rot_dim)
        ang = pos * freqs
        ones = jnp.ones((b, s, period - rot_dim), F32)
        cos = jnp.concatenate([jnp.cos(ang), jnp.cos(ang), ones], axis=-1)
        sin = jnp.concatenate([jnp.sin(ang), jnp.sin(ang), 0.0 * ones], axis=-1)
        reps = LANES // period
        return (jnp.tile(cos, (1, 1, reps)).reshape(b * s, LANES),
                jnp.tile(sin, (1, 1, reps)).reshape(b * s, LANES))

    return table(DIFF_ROT, DIFF_D) + table(MLA_ROPE, LANES)


def _pad_heads(w, used, width):
    nl, k, _ = w.shape
    w = w.reshape(nl, k, GROUP_HEADS, used)
    w = jnp.pad(w, ((0, 0), (0, 0), (0, 0), (0, width - used)))
    return w.reshape(nl, k, GROUP_HEADS * width)


def kernel(x, mem, positions, g_mix, w_in, b_f, g_cq, g_ckv, w_uq, w_ukv, sgu_ln_g, sgu_ln_b, w_s,
           b_s, lam_q1, lam_k1, lam_q2, lam_k2, g_diff, w_o, g_mem, g_cross, w_cq, w_ck, w_cv, w_co,
           g_ffn, w_gate, w_up, w_down, g_final):
    b, s, d = x.shape
    depth = w_in.shape[0]
    t = b * s
    tables = _rope_tables(positions)
    xf = x.reshape(t, d)
    memf = mem.reshape(b * mem.shape[1], d)
    row = lambda a: a.reshape(1, -1)
    seq = lambda a: a.reshape(b, s, a.shape[-1])

    w_main, w_f = _split_w_in(w_in)
    w_uq_p = _pad_heads(w_uq, MLA_NOPE + MLA_ROPE, MLA_QK_PAD).astype(BF16)
    w_ukv_b = w_ukv.astype(BF16)
    w_o_b, w_cq_b, w_ck_b, w_cv_b, w_co_b, w_gate_b, w_up_b, w_down_b = (
        _cast_bf16(w) for w in (w_o, w_cq, w_ck, w_cv, w_co, w_gate, w_up, w_down))
    b_st = jnp.swapaxes(b_s, 1, 2)

    for l in range(depth):
        lam_init = 0.8 - 0.6 * math.exp(-0.3 * l)
        fox, fb, dq, dk, dv, y_c, qm, km, vm = _front(
            xf, row(g_mix[l]), w_main, w_f, tables, row(sgu_ln_g[l]), row(sgu_ln_b[l]), w_s, b_st,
            row(g_cq[l]), row(g_ckv[l]), w_uq_p, w_ukv_b, l)

        bf_pad = jnp.pad(b_f[l], (0, LANES - GROUP_HEADS)).reshape(1, LANES)
        fcol, frow = _fox_gate(seq(fb), bf_pad)
        y_a = _fox_attn(seq(fox), fcol, frow)
        y_b = _mla_attn(seq(qm), seq(km), seq(vm))
        lamv = jnp.stack([lam_q1[l], lam_k1[l], lam_q2[l], lam_k2[l]])
        y_d = _diff_attn(seq(dq), seq(dk), seq(dv), lamv, row(g_diff[l]), lam_init)

        ys = [y_a.reshape(t, GROUP_W), y_b.reshape(t, GROUP_W), y_c, y_d.reshape(t, GROUP_W)]
        kv = _mem_kv(memf, row(g_mem), w_ck_b, w_cv_b, l).reshape(b, mem.shape[1], 2 * CROSS_W)
        xf = _mix_cross(xf, ys, w_o_b, row(g_cross[l]), w_cq_b, kv, w_co_b, s, l)

        xf = _ffn(xf, row(g_ffn[l]), w_gate_b, w_up_b, w_down_b, row(g_final),
                  final=(l == depth - 1), l=l)
    return xf.reshape(b, s, d)
```

```python
import functools
import math

import jax
import jax.numpy as jnp
from jax import lax
from jax.experimental import pallas as pl
from jax.experimental.pallas import tpu as pltpu

F32 = jnp.float32
BF16 = jnp.bfloat16

HEAD_DIM = 128
GROUP_HEADS = 4
GROUP_W = GROUP_HEADS * HEAD_DIM
MLA_Q_RANK = 384
MLA_KV_RANK = 256
MLA_NOPE = 128
MLA_ROPE = 64
MLA_V = 128
MLA_QK_PAD = 256
SGU_CHUNK = 128
DIFF_D = HEAD_DIM // 2
DIFF_ROT = DIFF_D // 4
CROSS_HEADS = 4
CROSS_HEAD_DIM = 128
CROSS_W = CROSS_HEADS * CROSS_HEAD_DIM
ROPE_THETA = 500000.0
EPS = 1e-6
LOG2E = math.log2(math.e)
LANES = 128

P_FOX = 0
P_DIFF = 3 * GROUP_W
P_SGU = 6 * GROUP_W
P_MLA = 8 * GROUP_W
MLA_SRC_W = MLA_Q_RANK + MLA_KV_RANK + MLA_ROPE
MLA_BLOCK_W = MLA_SRC_W + LANES - MLA_ROPE
P_W = P_MLA + MLA_BLOCK_W
FORGET_LANE = MLA_ROPE

VMEM_LIMIT = 52 * 2**20


def _params(*sem):
    return pltpu.CompilerParams(dimension_semantics=sem, vmem_limit_bytes=VMEM_LIMIT)


def _rms(x, g):
    return x * lax.rsqrt(jnp.mean(x * x, axis=-1, keepdims=True) + EPS) * g


def _dot(a, b):
    return jnp.dot(a, b, preferred_element_type=F32)


def _dot_nt(a, b):
    return lax.dot_general(a, b, (((1,), (1,)), ((), ())), preferred_element_type=F32)


def _cast_kernel(w_ref, o_ref):
    o_ref[...] = w_ref[...].astype(o_ref.dtype)


def _cast_bf16(w, block_bytes=4 * 2**20):
    nl, k, n = w.shape
    tk = k
    while tk * n * 4 > block_bytes and tk % 16 == 0:
        tk //= 2
    return pl.pallas_call(
        _cast_kernel,
        grid=(nl, k // tk),
        in_specs=[pl.BlockSpec((1, tk, n), lambda l, i: (l, i, 0))],
        out_specs=pl.BlockSpec((1, tk, n), lambda l, i: (l, i, 0)),
        out_shape=jax.ShapeDtypeStruct(w.shape, BF16),
        compiler_params=_params("parallel", "parallel"),
        name="cast_bf16",
    )(w)


_SRC_FORGET = 3 * GROUP_W
_SRC_MLA = _SRC_FORGET + GROUP_HEADS
_SRC_SGU = _SRC_MLA + MLA_SRC_W
_SRC_DIFF = _SRC_SGU + 2 * GROUP_W


def _split_w_in_kernel(w_ref, main_ref):
    latent_w = MLA_Q_RANK + MLA_KV_RANK
    copies = ((P_FOX, 0, 3 * GROUP_W), (P_DIFF, _SRC_DIFF, 3 * GROUP_W),
              (P_SGU, _SRC_SGU, 2 * GROUP_W), (P_MLA, _SRC_MLA, latent_w))
    for dst, src, width in copies:
        main_ref[0, :, dst:dst + width] = w_ref[0, :, src:src + width].astype(BF16)
    key = w_ref[0, :, _SRC_MLA + latent_w:_SRC_MLA + latent_w + LANES]
    forget = w_ref[0, :, _SRC_FORGET - FORGET_LANE:_SRC_FORGET - FORGET_LANE + LANES]
    lane = lax.broadcasted_iota(jnp.int32, key.shape, 1)
    group = jnp.where(lane < MLA_ROPE, key,
                      jnp.where(lane < FORGET_LANE + GROUP_HEADS, forget, 0.0))
    main_ref[0, :, P_MLA + latent_w:P_W] = group.astype(BF16)


def _split_w_in(w, tk=256):
    nl, k, n = w.shape
    return pl.pallas_call(
        _split_w_in_kernel,
        grid=(nl, k // tk),
        in_specs=[pl.BlockSpec((1, tk, n), lambda l, i: (l, i, 0))],
        out_specs=pl.BlockSpec((1, tk, P_W), lambda l, i: (l, i, 0)),
        out_shape=jax.ShapeDtypeStruct((nl, k, P_W), BF16),
        compiler_params=_params("parallel", "parallel"),
        name="split_w_in",
    )(w)


def _rope_pairs(x, cos, sin, half, period):
    lane = lax.broadcasted_iota(jnp.int32, x.shape, x.ndim - 1)
    first = (lane & (period - 1)) < half
    rot = jnp.where(first, -pltpu.roll(x, LANES - half, x.ndim - 1), pltpu.roll(x, half, x.ndim - 1))
    return x * cos + rot * sin


def _sgu_mix(uv, lg_ref, lb_ref, ws_ref, bs_ref, o_ref):
    z = 0.5 * uv * (1.0 + jnp.tanh(math.sqrt(2.0 / math.pi) * (uv + 0.044715 * (uv * uv * uv))))
    u = z[:, 0:GROUP_W]
    v = z[:, GROUP_W:2 * GROUP_W]
    xc = v - jnp.mean(v, axis=-1, keepdims=True)
    vn = xc * lax.rsqrt(jnp.mean(xc * xc, axis=-1, keepdims=True) + EPS) * lg_ref[...] + lb_ref[...]
    vn = vn.astype(BF16)
    t = SGU_CHUNK
    tril = (lax.broadcasted_iota(jnp.int32, (t, t), 1) <= lax.broadcasted_iota(jnp.int32, (t, t), 0))
    for g in range(GROUP_HEADS):
        w = jnp.where(tril, ws_ref[g], 0.0).astype(BF16)
        bias = bs_ref[:, g:g + 1]
        for c in range(uv.shape[0] // t):
            rows = slice(c * t, (c + 1) * t)
            cols = slice(g * HEAD_DIM, (g + 1) * HEAD_DIM)
            mixed = _dot(w, vn[rows, cols]) + bias
            o_ref[rows, cols] = (u[rows, cols] * mixed).astype(o_ref.dtype)


def _mla_expand(lat, gq_ref, gkv_ref, wq_ref, wkv_ref, cos, sin, q_ref, k_ref, v_ref, qscale):
    cq = lat[:, 0:MLA_Q_RANK]
    ckv = lat[:, MLA_Q_RANK:MLA_Q_RANK + MLA_KV_RANK]
    kr = lat[:, MLA_Q_RANK + MLA_KV_RANK:MLA_BLOCK_W]
    lane = lax.broadcasted_iota(jnp.int32, kr.shape, 1)
    kr = jnp.where(lane < MLA_ROPE, kr, 0.0)
    q = _dot(_rms(cq, gq_ref[...]).astype(BF16), wq_ref[...])
    kv = _dot(_rms(ckv, gkv_ref[...]).astype(BF16), wkv_ref[...])
    kr = _rope_pairs(kr, cos, sin, MLA_ROPE // 2, LANES).astype(BF16)
    for h in range(GROUP_HEADS):
        o = h * MLA_QK_PAD
        q_ref[:, o:o + LANES] = (q[:, o:o + LANES] * qscale).astype(BF16)
        q_ref[:, o + LANES:o + 2 * LANES] = (_rope_pairs(
            q[:, o + LANES:o + 2 * LANES], cos, sin, MLA_ROPE // 2, LANES) * qscale).astype(BF16)
        k_ref[:, o:o + LANES] = kv[:, o:o + LANES].astype(BF16)
        k_ref[:, o + LANES:o + 2 * LANES] = kr
        v_ref[:, h * MLA_V:(h + 1) * MLA_V] = kv[:, o + LANES:o + 2 * LANES].astype(BF16)


def _front_kernel(x_ref, g_ref, w_ref, cosd_ref, sind_ref, cosm_ref, sinm_ref,
                  lg_ref, lb_ref, ws_ref, bs_ref, gq_ref, gkv_ref, wuq_ref, wukv_ref,
                  fox_ref, fb_ref, dq_ref, dk_ref, dv_ref, yc_ref, qm_ref, km_ref, vm_ref,
                  *, fox_scale, diff_scale, mla_scale):
    xb = _rms(x_ref[...], g_ref[...]).astype(BF16)

    lat = _dot(xb, w_ref[:, P_MLA:P_MLA + MLA_BLOCK_W])
    fb_ref[...] = lat[:, MLA_BLOCK_W - LANES:MLA_BLOCK_W]
    _mla_expand(lat, gq_ref, gkv_ref, wuq_ref, wukv_ref, cosm_ref[...], sinm_ref[...],
                qm_ref, km_ref, vm_ref, mla_scale)

    uv = _dot(xb, w_ref[:, P_SGU:P_SGU + 2 * GROUP_W])
    _sgu_mix(uv, lg_ref, lb_ref, ws_ref, bs_ref, yc_ref)

    d = _dot(xb, w_ref[:, P_DIFF:P_DIFF + 3 * GROUP_W])
    cos = cosd_ref[...]
    sin = sind_ref[...]
    for c in range(GROUP_W // LANES):
        lanes = slice(c * LANES, (c + 1) * LANES)
        klanes = slice(GROUP_W + c * LANES, GROUP_W + (c + 1) * LANES)
        dq_ref[:, lanes] = (_rope_pairs(d[:, lanes], cos, sin, DIFF_ROT // 2, DIFF_D)
                            * diff_scale).astype(BF16)
        dk_ref[:, lanes] = _rope_pairs(d[:, klanes], cos, sin, DIFF_ROT // 2, DIFF_D).astype(BF16)
    dv_ref[...] = d[:, 2 * GROUP_W:3 * GROUP_W].astype(BF16)

    fox = _dot(xb, w_ref[:, P_FOX:P_FOX + 3 * GROUP_W])
    fox_ref[:, 0:GROUP_W] = (fox[:, 0:GROUP_W] * fox_scale).astype(BF16)
    fox_ref[:, GROUP_W:3 * GROUP_W] = fox[:, GROUP_W:3 * GROUP_W].astype(BF16)


def _front(x, g, w_main, tables, ln_g, ln_b, w_s, b_st, g_cq, g_ckv, w_uq, w_ukv, l, tm=256):
    t, d = x.shape
    hq = GROUP_HEADS * MLA_QK_PAD
    once = pl.Buffered(1)
    rowblk = lambda width: pl.BlockSpec((tm, width), lambda i: (i, 0))
    vec = lambda width: pl.BlockSpec((1, width), lambda i: (0, 0))
    widths = (3 * GROUP_W, LANES, GROUP_W, GROUP_W, GROUP_W, GROUP_W, hq, hq, GROUP_W)
    dtypes = (BF16, F32, BF16, BF16, BF16, BF16, BF16, BF16, BF16)
    return pl.pallas_call(
        functools.partial(_front_kernel, fox_scale=HEAD_DIM ** -0.5 * LOG2E,
                          diff_scale=DIFF_D ** -0.5 * LOG2E,
                          mla_scale=(MLA_NOPE + MLA_ROPE) ** -0.5 * LOG2E),
        grid=(t // tm,),
        in_specs=[rowblk(d), vec(d),
                  pl.BlockSpec((None, d, P_W), lambda i: (l, 0, 0), pipeline_mode=once),
                  rowblk(LANES), rowblk(LANES), rowblk(LANES), rowblk(LANES),
                  vec(GROUP_W), vec(GROUP_W),
                  pl.BlockSpec((None, GROUP_HEADS, SGU_CHUNK, SGU_CHUNK), lambda i: (l, 0, 0, 0)),
                  pl.BlockSpec((None, SGU_CHUNK, GROUP_HEADS), lambda i: (l, 0, 0)),
                  vec(MLA_Q_RANK), vec(MLA_KV_RANK),
                  pl.BlockSpec((None, MLA_Q_RANK, hq), lambda i: (l, 0, 0)),
                  pl.BlockSpec((None, MLA_KV_RANK, hq), lambda i: (l, 0, 0))],
        out_specs=[rowblk(w) for w in widths],
        out_shape=[jax.ShapeDtypeStruct((t, w), dt) for w, dt in zip(widths, dtypes)],
        compiler_params=_params("parallel"),
        name="front",
    )(x, g, w_main, *tables, ln_g, ln_b, w_s, b_st, g_cq, g_ckv, w_uq, w_ukv)


def _mem_kv_kernel(x_ref, g_ref, wk_ref, wv_ref, o_ref):
    xb = _rms(x_ref[...], g_ref[...]).astype(BF16)
    o_ref[:, 0:CROSS_W] = _dot(xb, wk_ref[...]).astype(o_ref.dtype)
    o_ref[:, CROSS_W:2 * CROSS_W] = _dot(xb, wv_ref[...]).astype(o_ref.dtype)


def _mem_kv(x, g, wk, wv, l, tm=512):
    t, d = x.shape
    wspec = pl.BlockSpec((None, d, CROSS_W), lambda i: (l, 0, 0))
    return pl.pallas_call(
        _mem_kv_kernel,
        grid=(t // tm,),
        in_specs=[pl.BlockSpec((tm, d), lambda i: (i, 0)),
                  pl.BlockSpec((1, d), lambda i: (0, 0)), wspec, wspec],
        out_specs=pl.BlockSpec((tm, 2 * CROSS_W), lambda i: (i, 0)),
        out_shape=jax.ShapeDtypeStruct((t, 2 * CROSS_W), BF16),
        compiler_params=_params("parallel"),
        name="mem_kv_proj",
    )(x, g, wk, wv)


def _fox_gate_kernel(f_ref, b_ref, col_ref, row_ref):
    x = f_ref[0] + b_ref[...]
    c = jnp.minimum(x, 0.0) - jnp.log1p(jnp.exp(-jnp.abs(x)))
    s = x.shape[0]
    ridx = lax.broadcasted_iota(jnp.int32, x.shape, 0)
    shift = 1
    while shift < s:
        c = c + jnp.where(ridx >= shift, pltpu.roll(c, shift, 0), 0.0)
        shift *= 2
    c = c * LOG2E
    col_ref[0] = c
    row_ref[0] = c.T[FORGET_LANE:FORGET_LANE + 8, :]


def _fox_gate(fb, b_f):
    b, s, _ = fb.shape
    return pl.pallas_call(
        _fox_gate_kernel,
        grid=(b,),
        in_specs=[pl.BlockSpec((1, s, LANES), lambda i: (i, 0, 0)),
                  pl.BlockSpec((1, LANES), lambda i: (0, 0))],
        out_specs=[pl.BlockSpec((1, s, LANES), lambda i: (i, 0, 0)),
                   pl.BlockSpec((1, 8, s), lambda i: (i, 0, 0))],
        out_shape=[jax.ShapeDtypeStruct((b, s, LANES), F32),
                   jax.ShapeDtypeStruct((b, 8, s), F32)],
        compiler_params=_params("parallel"),
        name="fox_gate",
    )(fb, b_f)


def _flash(qs, k_ref, v_ref, st_ref, dk, dv, qi, tq, fqs=None, frow_ref=None):
    nh = len(qs)
    r = qs[0].shape[0]
    assert dv == LANES and tq % (2 * LANES) == 0
    if fqs is not None:
        fqs = [jnp.broadcast_to(f, (r, LANES)) for f in fqs]

    def scores(h, rows, start, nk, diag_row0=None):
        s = _dot_nt(qs[h][rows], k_ref[0, pl.ds(start, nk), h * dk:(h + 1) * dk])
        if frow_ref is not None:
            s = s - frow_ref[0, h:h + 1, pl.ds(start, nk)]
        if diag_row0 is not None:
            row = lax.broadcasted_iota(jnp.int32, s.shape, 0) + diag_row0
            col = lax.broadcasted_iota(jnp.int32, s.shape, 1)
            s = jnp.where(col <= row, s, -jnp.inf)
        return s

    def absorb(s, h, rows, start, old):
        nr, nk = s.shape
        mz = jnp.broadcast_to(jnp.max(s, axis=-1, keepdims=True), (nr, LANES))
        if fqs is not None:
            mz = mz + fqs[h][rows]
        if old is None:
            m_new = mz
        else:
            m_new = jnp.maximum(old[0], mz)
            alpha = jnp.exp2(old[0] - m_new)
        shift = (fqs[h][rows] - m_new) if fqs is not None else -m_new
        p = jnp.exp2(s + jnp.concatenate([shift] * (nk // LANES), axis=1))
        psum = p[:, 0:LANES]
        for g in range(1, nk // LANES):
            psum = psum + p[:, g * LANES:(g + 1) * LANES]
        pv = _dot(p.astype(BF16), v_ref[0, pl.ds(start, nk), h * dv:(h + 1) * dv])
        if old is None:
            return m_new, psum, pv
        return m_new, alpha * old[1] + psum, alpha * old[2] + pv

    half = tq // 2
    diag_start = pl.multiple_of(qi * tq, tq)
    pieces = []
    for rep in range(r // tq):
        pieces.append((slice(rep * tq, rep * tq + half), half, 0))
        pieces.append((slice(rep * tq + half, (rep + 1) * tq), tq, half))
    diag = [[scores(h, rows, diag_start, nk, row0) for rows, nk, row0 in pieces] for h in range(nh)]
    for h in range(nh):
        parts = [absorb(s, h, rows, diag_start, None) for s, (rows, _, _) in zip(diag[h], pieces)]
        for i in range(3):
            st_ref[i, h] = jnp.concatenate([part[i] for part in parts], axis=0)

    def body(j, c):
        start = pl.multiple_of(j * tq, tq)
        every = slice(0, r)
        blocks = [scores(h, every, start, tq) for h in range(nh)]
        for h in range(nh):
            new = absorb(blocks[h], h, every, start, (st_ref[0, h], st_ref[1, h], st_ref[2, h]))
            for i in range(3):
                st_ref[i, h] = new[i]
        return c

    lax.fori_loop(0, qi, body, 0)
    return [(st_ref[2, h], jnp.sum(st_ref[1, h], axis=-1, keepdims=True)) for h in range(nh)]


def _flash_state(rows):
    return pltpu.VMEM((3, GROUP_HEADS, rows, LANES), F32)


def _fox_attn_kernel(q_ref, k_ref, v_ref, fcol_ref, frow_ref, o_ref, st_ref, *, tq):
    qi = pl.program_id(1)
    qs = [q_ref[0, :, h * HEAD_DIM:(h + 1) * HEAD_DIM] for h in range(GROUP_HEADS)]
    fqs = [fcol_ref[0, :, FORGET_LANE + h:FORGET_LANE + h + 1] for h in range(GROUP_HEADS)]
    res = _flash(qs, k_ref, v_ref, st_ref, HEAD_DIM, HEAD_DIM, qi, tq, fqs, frow_ref)
    for h, (acc, l) in enumerate(res):
        o_ref[0, :, h * HEAD_DIM:(h + 1) * HEAD_DIM] = (acc / l).astype(o_ref.dtype)


def _fox_attn(qkv, fcol, frow, tq=512):
    b, s, _ = qkv.shape
    w = GROUP_W
    return pl.pallas_call(
        functools.partial(_fox_attn_kernel, tq=tq),
        grid=(b, s // tq),
        in_specs=[pl.BlockSpec((1, tq, w), lambda i, j: (i, j, 0)),
                  pl.BlockSpec((1, s, w), lambda i, j: (i, 0, 1)),
                  pl.BlockSpec((1, s, w), lambda i, j: (i, 0, 2)),
                  pl.BlockSpec((1, tq, LANES), lambda i, j: (i, j, 0)),
                  pl.BlockSpec((1, 8, s), lambda i, j: (i, 0, 0))],
        out_specs=pl.BlockSpec((1, tq, w), lambda i, j: (i, j, 0)),
        out_shape=jax.ShapeDtypeStruct((b, s, w), BF16),
        scratch_shapes=[_flash_state(tq)],
        compiler_params=_params("parallel", "arbitrary"),
        name="fox_attn",
    )(qkv, qkv, qkv, fcol, frow)


def _mla_attn_kernel(q_ref, k_ref, v_ref, o_ref, st_ref, *, tq):
    qi = pl.program_id(1)
    qs = [q_ref[0, :, h * MLA_QK_PAD:(h + 1) * MLA_QK_PAD] for h in range(GROUP_HEADS)]
    res = _flash(qs, k_ref, v_ref, st_ref, MLA_QK_PAD, MLA_V, qi, tq)
    for h, (acc, l) in enumerate(res):
        o_ref[0, :, h * MLA_V:(h + 1) * MLA_V] = (acc / l).astype(o_ref.dtype)


def _mla_attn(qm, km, vm, tq=512):
    b, s, hq = qm.shape
    return pl.pallas_call(
        functools.partial(_mla_attn_kernel, tq=tq),
        grid=(b, s // tq),
        in_specs=[pl.BlockSpec((1, tq, hq), lambda i, j: (i, j, 0)),
                  pl.BlockSpec((1, s, hq), lambda i, j: (i, 0, 0)),
                  pl.BlockSpec((1, s, GROUP_W), lambda i, j: (i, 0, 0))],
        out_specs=pl.BlockSpec((1, tq, GROUP_W), lambda i, j: (i, j, 0)),
        out_shape=jax.ShapeDtypeStruct((b, s, GROUP_W), BF16),
        scratch_shapes=[_flash_state(tq)],
        compiler_params=_params("parallel", "arbitrary"),
        name="mla_attn",
    )(qm, km, vm)


def _diff_attn_kernel(q_ref, k_ref, v_ref, lam_ref, g_ref, o_ref, st_ref, *, tq, lam_init):
    qi = pl.program_id(1)
    lv = lam_ref[...]
    lam = (jnp.exp(jnp.sum(lv[0:1] * lv[1:2], axis=-1, keepdims=True))
           - jnp.exp(jnp.sum(lv[2:3] * lv[3:4], axis=-1, keepdims=True)) + lam_init)
    lane = lax.broadcasted_iota(jnp.int32, (tq, HEAD_DIM), 1)
    qs = []
    for h in range(GROUP_HEADS):
        q = q_ref[0, :, h * HEAD_DIM:(h + 1) * HEAD_DIM]
        zero = jnp.zeros_like(q)
        qs.append(jnp.concatenate([jnp.where(lane < DIFF_D, q, zero),
                                   jnp.where(lane >= DIFF_D, q, zero)], axis=0))
    res = _flash(qs, k_ref, v_ref, st_ref, HEAD_DIM, HEAD_DIM, qi, tq)
    for h, (acc, l) in enumerate(res):
        o = acc / l
        o = o[0:tq] - lam * o[tq:2 * tq]
        o = _rms(o, g_ref[...]) * (1.0 - lam_init)
        o_ref[0, :, h * HEAD_DIM:(h + 1) * HEAD_DIM] = o.astype(o_ref.dtype)


def _diff_attn(qr, kr, v, lamv, g_diff, lam_init, tq=512):
    b, s, w = qr.shape
    return pl.pallas_call(
        functools.partial(_diff_attn_kernel, tq=tq, lam_init=lam_init),
        grid=(b, s // tq),
        in_specs=[pl.BlockSpec((1, tq, w), lambda i, j: (i, j, 0)),
                  pl.BlockSpec((1, s, w), lambda i, j: (i, 0, 0)),
                  pl.BlockSpec((1, s, w), lambda i, j: (i, 0, 0)),
                  pl.BlockSpec((4, DIFF_D), lambda i, j: (0, 0)),
                  pl.BlockSpec((1, HEAD_DIM), lambda i, j: (0, 0))],
        out_specs=pl.BlockSpec((1, tq, w), lambda i, j: (i, j, 0)),
        out_shape=jax.ShapeDtypeStruct((b, s, w), BF16),
        scratch_shapes=[_flash_state(2 * tq)],
        compiler_params=_params("parallel", "arbitrary"),
        name="diff_attn",
    )(qr, kr, v, lamv, g_diff)


def _mix_cross_kernel(x_ref, ya_ref, yb_ref, yc_ref, yd_ref, wo_ref, g_ref, wq_ref, kv_ref,
                      wco_ref, o_ref, oh_ref, *, scale):
    acc = x_ref[...]
    for m, y_ref in enumerate((ya_ref, yb_ref, yc_ref, yd_ref)):
        acc = acc + _dot(y_ref[...], wo_ref[m * GROUP_W:(m + 1) * GROUP_W, :])
    o_ref[...] = acc
    q = _dot(_rms(o_ref[...], g_ref[...]).astype(BF16), wq_ref[...]).astype(BF16)
    for h in range(CROSS_HEADS):
        cols = slice(h * CROSS_HEAD_DIM, (h + 1) * CROSS_HEAD_DIM)
        k = kv_ref[0, :, cols]
        v = kv_ref[0, :, CROSS_W + h * CROSS_HEAD_DIM:CROSS_W + (h + 1) * CROSS_HEAD_DIM]
        s = _dot_nt(q[:, cols], k) * scale
        p = jnp.exp(s - jnp.max(s, axis=-1, keepdims=True))
        l = jnp.sum(p, axis=-1, keepdims=True)
        oh_ref[:, cols] = (_dot(p.astype(BF16), v) / l).astype(BF16)
    o_ref[...] += _dot(oh_ref[...], wco_ref[...])


def _mix_cross(x, ys, w_o, g, wq, kv, wco, seq, l, tm=512):
    t, d = x.shape
    per_b = seq // tm
    m = kv.shape[1]
    once = pl.Buffered(1)
    yspec = pl.BlockSpec((tm, GROUP_W), lambda i: (i, 0))
    return pl.pallas_call(
        functools.partial(_mix_cross_kernel, scale=CROSS_HEAD_DIM ** -0.5),
        grid=(t // tm,),
        in_specs=[pl.BlockSpec((tm, d), lambda i: (i, 0)), yspec, yspec, yspec, yspec,
                  pl.BlockSpec((None, w_o.shape[1], d), lambda i: (l, 0, 0), pipeline_mode=once),
                  pl.BlockSpec((1, d), lambda i: (0, 0)),
                  pl.BlockSpec((None, d, CROSS_W), lambda i: (l, 0, 0), pipeline_mode=once),
                  pl.BlockSpec((1, m, 2 * CROSS_W), lambda i: (i // per_b, 0, 0)),
                  pl.BlockSpec((None, CROSS_W, d), lambda i: (l, 0, 0), pipeline_mode=once)],
        out_specs=pl.BlockSpec((tm, d), lambda i: (i, 0)),
        out_shape=jax.ShapeDtypeStruct((t, d), F32),
        scratch_shapes=[pltpu.VMEM((tm, CROSS_W), BF16)],
        compiler_params=_params("parallel"),
        name="mix_cross",
    )(x, *ys, w_o, g, wq, kv, wco)


def _ffn_kernel(x_ref, g_ref, wg_ref, wu_ref, wd_ref, gf_ref, o_ref, xn_ref, acc_ref, *, final):
    j = pl.program_id(1)

    @pl.when(j == 0)
    def _():
        xn_ref[...] = _rms(x_ref[...], g_ref[...]).astype(BF16)
        acc_ref[...] = jnp.zeros_like(acc_ref)

    xn = xn_ref[...]
    gate = _dot(xn, wg_ref[...])
    up = _dot(xn, wu_ref[...])
    hid = (gate * jax.nn.sigmoid(gate) * up).astype(BF16)
    acc_ref[...] += _dot(hid, wd_ref[...])

    @pl.when(j == pl.num_programs(1) - 1)
    def _():
        y = x_ref[...] + acc_ref[...]
        o_ref[...] = _rms(y, gf_ref[...]) if final else y


def _ffn(x, g, wg, wu, wd, g_final, final, l, tm=512, th=512):
    t, d = x.shape
    hdim = wg.shape[2]
    return pl.pallas_call(
        functools.partial(_ffn_kernel, final=final),
        grid=(t // tm, hdim // th),
        in_specs=[pl.BlockSpec((tm, d), lambda i, j: (i, 0)),
                  pl.BlockSpec((1, d), lambda i, j: (0, 0)),
                  pl.BlockSpec((None, d, th), lambda i, j: (l, 0, j)),
                  pl.BlockSpec((None, d, th), lambda i, j: (l, 0, j)),
                  pl.BlockSpec((None, th, d), lambda i, j: (l, j, 0)),
                  pl.BlockSpec((1, d), lambda i, j: (0, 0))],
        out_specs=pl.BlockSpec((tm, d), lambda i, j: (i, 0)),
        out_shape=jax.ShapeDtypeStruct((t, d), F32),
        scratch_shapes=[pltpu.VMEM((tm, d), BF16), pltpu.VMEM((tm, d), F32)],
        compiler_params=_params("parallel", "arbitrary"),
        name="swiglu",
    )(x, g, wg, wu, wd, g_final)


def _rope_tables(positions):
    pos = positions.astype(F32)[..., None]
    b, s = positions.shape

    def table(rot_dim, period):
        freqs = ROPE_THETA ** (-jnp.arange(0, rot_dim, 2, dtype=F32) / rot_dim)
        ang = pos * freqs
        ones = jnp.ones((b, s, period - rot_dim), F32)
        cos = jnp.concatenate([jnp.cos(ang), jnp.cos(ang), ones], axis=-1)
        sin = jnp.concatenate([jnp.sin(ang), jnp.sin(ang), 0.0 * ones], axis=-1)
        reps = LANES // period
        return (jnp.tile(cos, (1, 1, reps)).reshape(b * s, LANES),
                jnp.tile(sin, (1, 1, reps)).reshape(b * s, LANES))

    return table(DIFF_ROT, DIFF_D) + table(MLA_ROPE, LANES)


def _pad_heads(w, used, width):
    nl, k, _ = w.shape
    w = w.reshape(nl, k, GROUP_HEADS, used)
    w = jnp.pad(w, ((0, 0), (0, 0), (0, 0), (0, width - used)))
    return w.reshape(nl, k, GROUP_HEADS * width)


def kernel(x, mem, positions, g_mix, w_in, b_f, g_cq, g_ckv, w_uq, w_ukv, sgu_ln_g, sgu_ln_b, w_s,
           b_s, lam_q1, lam_k1, lam_q2, lam_k2, g_diff, w_o, g_mem, g_cross, w_cq, w_ck, w_cv, w_co,
           g_ffn, w_gate, w_up, w_down, g_final):
    b, s, d = x.shape
    depth = w_in.shape[0]
    t = b * s
    tables = _rope_tables(positions)
    xf = x.reshape(t, d)
    memf = mem.reshape(b * mem.shape[1], d)
    row = lambda a: a.reshape(1, -1)
    seq = lambda a: a.reshape(b, s, a.shape[-1])

    w_main = _split_w_in(w_in)
    w_uq_p = _pad_heads(w_uq, MLA_NOPE + MLA_ROPE, MLA_QK_PAD).astype(BF16)
    w_ukv_b = w_ukv.astype(BF16)
    w_o_b, w_cq_b, w_ck_b, w_cv_b, w_co_b, w_gate_b, w_up_b, w_down_b = (
        _cast_bf16(w) for w in (w_o, w_cq, w_ck, w_cv, w_co, w_gate, w_up, w_down))
    b_st = jnp.swapaxes(b_s, 1, 2)

    for l in range(depth):
        lam_init = 0.8 - 0.6 * math.exp(-0.3 * l)
        fox, fb, dq, dk, dv, y_c, qm, km, vm = _front(
            xf, row(g_mix[l]), w_main, tables, row(sgu_ln_g[l]), row(sgu_ln_b[l]), w_s, b_st,
            row(g_cq[l]), row(g_ckv[l]), w_uq_p, w_ukv_b, l)

        bf_pad = jnp.pad(b_f[l], (FORGET_LANE, LANES - FORGET_LANE - GROUP_HEADS)).reshape(1, LANES)
        fcol, frow = _fox_gate(seq(fb), bf_pad)
        y_a = _fox_attn(seq(fox), fcol, frow)
        y_b = _mla_attn(seq(qm), seq(km), seq(vm))
        lamv = jnp.stack([lam_q1[l], lam_k1[l], lam_q2[l], lam_k2[l]])
        y_d = _diff_attn(seq(dq), seq(dk), seq(dv), lamv, row(g_diff[l]), lam_init)

        ys = [y_a.reshape(t, GROUP_W), y_b.reshape(t, GROUP_W), y_c, y_d.reshape(t, GROUP_W)]
        kv = _mem_kv(memf, row(g_mem), w_ck_b, w_cv_b, l).reshape(b, mem.shape[1], 2 * CROSS_W)
        xf = _mix_cross(xf, ys, w_o_b, row(g_cross[l]), w_cq_b, kv, w_co_b, s, l)

        xf = _ffn(xf, row(g_ffn[l]), w_gate_b, w_up_b, w_down_b, row(g_final),
                  final=(l == depth - 1), l=l)
    return xf.reshape(b, s, d)
```

```python
import functools
import math

import jax
import jax.numpy as jnp
from jax import lax
from jax.experimental import pallas as pl
from jax.experimental.pallas import tpu as pltpu

F32 = jnp.float32
BF16 = jnp.bfloat16

HEAD_DIM = 128
GROUP_HEADS = 4
GROUP_W = GROUP_HEADS * HEAD_DIM
MLA_Q_RANK = 384
MLA_KV_RANK = 256
MLA_NOPE = 128
MLA_ROPE = 64
MLA_V = 128
MLA_QK_PAD = 256
SGU_CHUNK = 128
DIFF_D = HEAD_DIM // 2
DIFF_ROT = DIFF_D // 4
CROSS_HEADS = 4
CROSS_HEAD_DIM = 128
CROSS_W = CROSS_HEADS * CROSS_HEAD_DIM
ROPE_THETA = 500000.0
EPS = 1e-6
LOG2E = math.log2(math.e)
LANES = 128

P_FOX = 0
P_DIFF = 3 * GROUP_W
P_SGU = 6 * GROUP_W
P_MLA = 8 * GROUP_W
MLA_SRC_W = MLA_Q_RANK + MLA_KV_RANK + MLA_ROPE
MLA_BLOCK_W = MLA_SRC_W + LANES - MLA_ROPE
P_W = P_MLA + MLA_BLOCK_W
FORGET_LANE = MLA_ROPE

VMEM_LIMIT = 52 * 2**20


def _params(*sem):
    return pltpu.CompilerParams(dimension_semantics=sem, vmem_limit_bytes=VMEM_LIMIT)


def _rms(x, g):
    return x * lax.rsqrt(jnp.mean(x * x, axis=-1, keepdims=True) + EPS) * g


def _dot(a, b):
    return jnp.dot(a, b, preferred_element_type=F32)


def _dot_nt(a, b):
    return lax.dot_general(a, b, (((1,), (1,)), ((), ())), preferred_element_type=F32)


def _cast_kernel(w_ref, o_ref):
    o_ref[...] = w_ref[...].astype(o_ref.dtype)


def _cast_bf16(w, block_bytes=4 * 2**20):
    nl, k, n = w.shape
    tk = k
    while tk * n * 4 > block_bytes and tk % 16 == 0:
        tk //= 2
    return pl.pallas_call(
        _cast_kernel,
        grid=(nl, k // tk),
        in_specs=[pl.BlockSpec((1, tk, n), lambda l, i: (l, i, 0))],
        out_specs=pl.BlockSpec((1, tk, n), lambda l, i: (l, i, 0)),
        out_shape=jax.ShapeDtypeStruct(w.shape, BF16),
        compiler_params=_params("parallel", "parallel"),
        name="cast_bf16",
    )(w)


_SRC_FORGET = 3 * GROUP_W
_SRC_MLA = _SRC_FORGET + GROUP_HEADS
_SRC_SGU = _SRC_MLA + MLA_SRC_W
_SRC_DIFF = _SRC_SGU + 2 * GROUP_W


def _split_w_in_kernel(w_ref, main_ref):
    latent_w = MLA_Q_RANK + MLA_KV_RANK
    copies = ((P_FOX, 0, 3 * GROUP_W), (P_DIFF, _SRC_DIFF, 3 * GROUP_W),
              (P_SGU, _SRC_SGU, 2 * GROUP_W), (P_MLA, _SRC_MLA, latent_w))
    for dst, src, width in copies:
        main_ref[0, :, dst:dst + width] = w_ref[0, :, src:src + width].astype(BF16)
    key = w_ref[0, :, _SRC_MLA + latent_w:_SRC_MLA + latent_w + LANES]
    forget = w_ref[0, :, _SRC_FORGET - FORGET_LANE:_SRC_FORGET - FORGET_LANE + LANES]
    lane = lax.broadcasted_iota(jnp.int32, key.shape, 1)
    group = jnp.where(lane < MLA_ROPE, key,
                      jnp.where(lane < FORGET_LANE + GROUP_HEADS, forget, 0.0))
    main_ref[0, :, P_MLA + latent_w:P_W] = group.astype(BF16)


def _split_w_in(w, tk=256):
    nl, k, n = w.shape
    return pl.pallas_call(
        _split_w_in_kernel,
        grid=(nl, k // tk),
        in_specs=[pl.BlockSpec((1, tk, n), lambda l, i: (l, i, 0))],
        out_specs=pl.BlockSpec((1, tk, P_W), lambda l, i: (l, i, 0)),
        out_shape=jax.ShapeDtypeStruct((nl, k, P_W), BF16),
        compiler_params=_params("parallel", "parallel"),
        name="split_w_in",
    )(w)


def _rope_pairs(x, cos, sin, half, period):
    lane = lax.broadcasted_iota(jnp.int32, x.shape, x.ndim - 1)
    first = (lane & (period - 1)) < half
    rot = jnp.where(first, -pltpu.roll(x, LANES - half, x.ndim - 1), pltpu.roll(x, half, x.ndim - 1))
    return x * cos + rot * sin


def _sgu_mix(uv, lg_ref, lb_ref, ws_ref, bs_ref, o_ref):
    z = 0.5 * uv * (1.0 + jnp.tanh(math.sqrt(2.0 / math.pi) * (uv + 0.044715 * (uv * uv * uv))))
    u = z[:, 0:GROUP_W]
    v = z[:, GROUP_W:2 * GROUP_W]
    xc = v - jnp.mean(v, axis=-1, keepdims=True)
    vn = xc * lax.rsqrt(jnp.mean(xc * xc, axis=-1, keepdims=True) + EPS) * lg_ref[...] + lb_ref[...]
    vn = vn.astype(BF16)
    t = SGU_CHUNK
    tril = (lax.broadcasted_iota(jnp.int32, (t, t), 1) <= lax.broadcasted_iota(jnp.int32, (t, t), 0))
    for g in range(GROUP_HEADS):
        w = jnp.where(tril, ws_ref[g], 0.0).astype(BF16)
        bias = bs_ref[:, g:g + 1]
        for c in range(uv.shape[0] // t):
            rows = slice(c * t, (c + 1) * t)
            cols = slice(g * HEAD_DIM, (g + 1) * HEAD_DIM)
            mixed = _dot(w, vn[rows, cols]) + bias
            o_ref[rows, cols] = (u[rows, cols] * mixed).astype(o_ref.dtype)


def _mla_expand(lat, gq_ref, gkv_ref, wq_ref, wkv_ref, cos, sin, q_ref, k_ref, v_ref, qscale):
    cq = lat[:, 0:MLA_Q_RANK]
    ckv = lat[:, MLA_Q_RANK:MLA_Q_RANK + MLA_KV_RANK]
    kr = lat[:, MLA_Q_RANK + MLA_KV_RANK:MLA_BLOCK_W]
    lane = lax.broadcasted_iota(jnp.int32, kr.shape, 1)
    kr = jnp.where(lane < MLA_ROPE, kr, 0.0)
    q = _dot(_rms(cq, gq_ref[...]).astype(BF16), wq_ref[...])
    kv = _dot(_rms(ckv, gkv_ref[...]).astype(BF16), wkv_ref[...])
    kr = _rope_pairs(kr, cos, sin, MLA_ROPE // 2, LANES).astype(BF16)
    for h in range(GROUP_HEADS):
        o = h * MLA_QK_PAD
        q_ref[:, o:o + LANES] = (q[:, o:o + LANES] * qscale).astype(BF16)
        q_ref[:, o + LANES:o + 2 * LANES] = (_rope_pairs(
            q[:, o + LANES:o + 2 * LANES], cos, sin, MLA_ROPE // 2, LANES) * qscale).astype(BF16)
        k_ref[:, o:o + LANES] = kv[:, o:o + LANES].astype(BF16)
        k_ref[:, o + LANES:o + 2 * LANES] = kr
        v_ref[:, h * MLA_V:(h + 1) * MLA_V] = kv[:, o + LANES:o + 2 * LANES].astype(BF16)


def _front_kernel(x_ref, g_ref, w_ref, cosd_ref, sind_ref, cosm_ref, sinm_ref,
                  lg_ref, lb_ref, ws_ref, bs_ref, gq_ref, gkv_ref, wuq_ref, wukv_ref,
                  fox_ref, fb_ref, dq_ref, dk_ref, dv_ref, yc_ref, qm_ref, km_ref, vm_ref,
                  *, fox_scale, diff_scale, mla_scale):
    xb = _rms(x_ref[...], g_ref[...]).astype(BF16)

    lat = _dot(xb, w_ref[:, P_MLA:P_MLA + MLA_BLOCK_W])
    fb_ref[...] = lat[:, MLA_BLOCK_W - LANES:MLA_BLOCK_W]
    _mla_expand(lat, gq_ref, gkv_ref, wuq_ref, wukv_ref, cosm_ref[...], sinm_ref[...],
                qm_ref, km_ref, vm_ref, mla_scale)

    uv = _dot(xb, w_ref[:, P_SGU:P_SGU + 2 * GROUP_W])
    _sgu_mix(uv, lg_ref, lb_ref, ws_ref, bs_ref, yc_ref)

    d = _dot(xb, w_ref[:, P_DIFF:P_DIFF + 3 * GROUP_W])
    cos = cosd_ref[...]
    sin = sind_ref[...]
    for c in range(GROUP_W // LANES):
        lanes = slice(c * LANES, (c + 1) * LANES)
        klanes = slice(GROUP_W + c * LANES, GROUP_W + (c + 1) * LANES)
        dq_ref[:, lanes] = (_rope_pairs(d[:, lanes], cos, sin, DIFF_ROT // 2, DIFF_D)
                            * diff_scale).astype(BF16)
        dk_ref[:, lanes] = _rope_pairs(d[:, klanes], cos, sin, DIFF_ROT // 2, DIFF_D).astype(BF16)
    dv_ref[...] = d[:, 2 * GROUP_W:3 * GROUP_W].astype(BF16)

    fox = _dot(xb, w_ref[:, P_FOX:P_FOX + 3 * GROUP_W])
    fox_ref[:, 0:GROUP_W] = (fox[:, 0:GROUP_W] * fox_scale).astype(BF16)
    fox_ref[:, GROUP_W:3 * GROUP_W] = fox[:, GROUP_W:3 * GROUP_W].astype(BF16)


def _front(x, g, w_main, tables, ln_g, ln_b, w_s, b_st, g_cq, g_ckv, w_uq, w_ukv, l, tm=256):
    t, d = x.shape
    hq = GROUP_HEADS * MLA_QK_PAD
    once = pl.Buffered(1)
    rowblk = lambda width: pl.BlockSpec((tm, width), lambda i: (i, 0))
    vec = lambda width: pl.BlockSpec((1, width), lambda i: (0, 0))
    widths = (3 * GROUP_W, LANES, GROUP_W, GROUP_W, GROUP_W, GROUP_W, hq, hq, GROUP_W)
    dtypes = (BF16, F32, BF16, BF16, BF16, BF16, BF16, BF16, BF16)
    return pl.pallas_call(
        functools.partial(_front_kernel, fox_scale=HEAD_DIM ** -0.5 * LOG2E,
                          diff_scale=DIFF_D ** -0.5 * LOG2E,
                          mla_scale=(MLA_NOPE + MLA_ROPE) ** -0.5 * LOG2E),
        grid=(t // tm,),
        in_specs=[rowblk(d), vec(d),
                  pl.BlockSpec((None, d, P_W), lambda i: (l, 0, 0), pipeline_mode=once),
                  rowblk(LANES), rowblk(LANES), rowblk(LANES), rowblk(LANES),
                  vec(GROUP_W), vec(GROUP_W),
                  pl.BlockSpec((None, GROUP_HEADS, SGU_CHUNK, SGU_CHUNK), lambda i: (l, 0, 0, 0)),
                  pl.BlockSpec((None, SGU_CHUNK, GROUP_HEADS), lambda i: (l, 0, 0)),
                  vec(MLA_Q_RANK), vec(MLA_KV_RANK),
                  pl.BlockSpec((None, MLA_Q_RANK, hq), lambda i: (l, 0, 0)),
                  pl.BlockSpec((None, MLA_KV_RANK, hq), lambda i: (l, 0, 0))],
        out_specs=[rowblk(w) for w in widths],
        out_shape=[jax.ShapeDtypeStruct((t, w), dt) for w, dt in zip(widths, dtypes)],
        compiler_params=_params("parallel"),
        name="front",
    )(x, g, w_main, *tables, ln_g, ln_b, w_s, b_st, g_cq, g_ckv, w_uq, w_ukv)


def _mem_kv_kernel(x_ref, g_ref, wk_ref, wv_ref, o_ref):
    xb = _rms(x_ref[...], g_ref[...]).astype(BF16)
    o_ref[:, 0:CROSS_W] = _dot(xb, wk_ref[...]).astype(o_ref.dtype)
    o_ref[:, CROSS_W:2 * CROSS_W] = _dot(xb, wv_ref[...]).astype(o_ref.dtype)


def _mem_kv(x, g, wk, wv, l, tm=512):
    t, d = x.shape
    wspec = pl.BlockSpec((None, d, CROSS_W), lambda i: (l, 0, 0))
    return pl.pallas_call(
        _mem_kv_kernel,
        grid=(t // tm,),
        in_specs=[pl.BlockSpec((tm, d), lambda i: (i, 0)),
                  pl.BlockSpec((1, d), lambda i: (0, 0)), wspec, wspec],
        out_specs=pl.BlockSpec((tm, 2 * CROSS_W), lambda i: (i, 0)),
        out_shape=jax.ShapeDtypeStruct((t, 2 * CROSS_W), BF16),
        compiler_params=_params("parallel"),
        name="mem_kv_proj",
    )(x, g, wk, wv)


def _fox_gate_kernel(f_ref, b_ref, col_ref, row_ref):
    x = f_ref[0] + b_ref[...]
    c = jnp.minimum(x, 0.0) - jnp.log1p(jnp.exp(-jnp.abs(x)))
    s = x.shape[0]
    ridx = lax.broadcasted_iota(jnp.int32, x.shape, 0)
    shift = 1
    while shift < s:
        c = c + jnp.where(ridx >= shift, pltpu.roll(c, shift, 0), 0.0)
        shift *= 2
    c = c * LOG2E
    col_ref[0] = c
    row_ref[0] = c.T[FORGET_LANE:FORGET_LANE + 8, :]


def _fox_gate(fb, b_f):
    b, s, _ = fb.shape
    return pl.pallas_call(
        _fox_gate_kernel,
        grid=(b,),
        in_specs=[pl.BlockSpec((1, s, LANES), lambda i: (i, 0, 0)),
                  pl.BlockSpec((1, LANES), lambda i: (0, 0))],
        out_specs=[pl.BlockSpec((1, s, LANES), lambda i: (i, 0, 0)),
                   pl.BlockSpec((1, 8, s), lambda i: (i, 0, 0))],
        out_shape=[jax.ShapeDtypeStruct((b, s, LANES), F32),
                   jax.ShapeDtypeStruct((b, 8, s), F32)],
        compiler_params=_params("parallel"),
        name="fox_gate",
    )(fb, b_f)


def _flash(qs, k_ref, v_ref, st_ref, dk, dv, qi, tq, fqs=None, frow_ref=None):
    nh = len(qs)
    r = qs[0].shape[0]
    assert dv == LANES and tq % (2 * LANES) == 0
    if fqs is not None:
        fqs = [jnp.broadcast_to(f, (r, LANES)) for f in fqs]

    def scores(h, rows, start, nk, diag_row0=None):
        s = _dot_nt(qs[h][rows], k_ref[0, pl.ds(start, nk), h * dk:(h + 1) * dk])
        if frow_ref is not None:
            s = s - frow_ref[0, h:h + 1, pl.ds(start, nk)]
        if diag_row0 is not None:
            row = lax.broadcasted_iota(jnp.int32, s.shape, 0) + diag_row0
            col = lax.broadcasted_iota(jnp.int32, s.shape, 1)
            s = jnp.where(col <= row, s, -jnp.inf)
        return s

    def absorb(s, h, rows, start, old):
        nr, nk = s.shape
        mz = jnp.broadcast_to(jnp.max(s, axis=-1, keepdims=True), (nr, LANES))
        if fqs is not None:
            mz = mz + fqs[h][rows]
        if old is None:
            m_new = mz
        else:
            m_new = jnp.maximum(old[0], mz)
            alpha = jnp.exp2(old[0] - m_new)
        shift = (fqs[h][rows] - m_new) if fqs is not None else -m_new
        p = jnp.exp2(s + jnp.concatenate([shift] * (nk // LANES), axis=1))
        psum = p[:, 0:LANES]
        for g in range(1, nk // LANES):
            psum = psum + p[:, g * LANES:(g + 1) * LANES]
        pv = _dot(p.astype(BF16), v_ref[0, pl.ds(start, nk), h * dv:(h + 1) * dv])
        if old is None:
            return m_new, psum, pv
        return m_new, alpha * old[1] + psum, alpha * old[2] + pv

    half = tq // 2
    diag_start = pl.multiple_of(qi * tq, tq)
    pieces = []
    for rep in range(r // tq):
        pieces.append((slice(rep * tq, rep * tq + half), half, 0))
        pieces.append((slice(rep * tq + half, (rep + 1) * tq), tq, half))
    diag = [[scores(h, rows, diag_start, nk, row0) for rows, nk, row0 in pieces] for h in range(nh)]
    for h in range(nh):
        parts = [absorb(s, h, rows, diag_start, None) for s, (rows, _, _) in zip(diag[h], pieces)]
        for i in range(3):
            st_ref[i, h] = jnp.concatenate([part[i] for part in parts], axis=0)

    def body(j, c):
        start = pl.multiple_of(j * tq, tq)
        every = slice(0, r)
        blocks = [scores(h, every, start, tq) for h in range(nh)]
        for h in range(nh):
            new = absorb(blocks[h], h, every, start, (st_ref[0, h], st_ref[1, h], st_ref[2, h]))
            for i in range(3):
                st_ref[i, h] = new[i]
        return c

    lax.fori_loop(0, qi, body, 0)
    return [(st_ref[2, h], jnp.sum(st_ref[1, h], axis=-1, keepdims=True)) for h in range(nh)]


def _flash_state(rows):
    return pltpu.VMEM((3, GROUP_HEADS, rows, LANES), F32)


def _fox_attn_kernel(q_ref, k_ref, v_ref, fcol_ref, frow_ref, o_ref, st_ref, *, tq):
    qi = pl.program_id(1)
    qs = [q_ref[0, :, h * HEAD_DIM:(h + 1) * HEAD_DIM] for h in range(GROUP_HEADS)]
    fqs = [fcol_ref[0, :, FORGET_LANE + h:FORGET_LANE + h + 1] for h in range(GROUP_HEADS)]
    res = _flash(qs, k_ref, v_ref, st_ref, HEAD_DIM, HEAD_DIM, qi, tq, fqs, frow_ref)
    for h, (acc, l) in enumerate(res):
        o_ref[0, :, h * HEAD_DIM:(h + 1) * HEAD_DIM] = (acc / l).astype(o_ref.dtype)


def _fox_attn(qkv, fcol, frow, tq=512):
    b, s, _ = qkv.shape
    w = GROUP_W
    return pl.pallas_call(
        functools.partial(_fox_attn_kernel, tq=tq),
        grid=(b, s // tq),
        in_specs=[pl.BlockSpec((1, tq, w), lambda i, j: (i, j, 0)),
                  pl.BlockSpec((1, s, w), lambda i, j: (i, 0, 1)),
                  pl.BlockSpec((1, s, w), lambda i, j: (i, 0, 2)),
                  pl.BlockSpec((1, tq, LANES), lambda i, j: (i, j, 0)),
                  pl.BlockSpec((1, 8, s), lambda i, j: (i, 0, 0))],
        out_specs=pl.BlockSpec((1, tq, w), lambda i, j: (i, j, 0)),
        out_shape=jax.ShapeDtypeStruct((b, s, w), BF16),
        scratch_shapes=[_flash_state(tq)],
        compiler_params=_params("parallel", "arbitrary"),
        name="fox_attn",
    )(qkv, qkv, qkv, fcol, frow)


def _mla_attn_kernel(q_ref, k_ref, v_ref, o_ref, st_ref, *, tq):
    qi = pl.program_id(1)
    qs = [q_ref[0, :, h * MLA_QK_PAD:(h + 1) * MLA_QK_PAD] for h in range(GROUP_HEADS)]
    res = _flash(qs, k_ref, v_ref, st_ref, MLA_QK_PAD, MLA_V, qi, tq)
    for h, (acc, l) in enumerate(res):
        o_ref[0, :, h * MLA_V:(h + 1) * MLA_V] = (acc / l).astype(o_ref.dtype)


def _mla_attn(qm, km, vm, tq=512):
    b, s, hq = qm.shape
    return pl.pallas_call(
        functools.partial(_mla_attn_kernel, tq=tq),
        grid=(b, s // tq),
        in_specs=[pl.BlockSpec((1, tq, hq), lambda i, j: (i, j, 0)),
                  pl.BlockSpec((1, s, hq), lambda i, j: (i, 0, 0)),
                  pl.BlockSpec((1, s, GROUP_W), lambda i, j: (i, 0, 0))],
        out_specs=pl.BlockSpec((1, tq, GROUP_W), lambda i, j: (i, j, 0)),
        out_shape=jax.ShapeDtypeStruct((b, s, GROUP_W), BF16),
        scratch_shapes=[_flash_state(tq)],
        compiler_params=_params("parallel", "arbitrary"),
        name="mla_attn",
    )(qm, km, vm)


def _diff_attn_kernel(q_ref, k_ref, v_ref, lam_ref, g_ref, o_ref, st_ref, *, tq, lam_init):
    qi = pl.program_id(1)
    lv = lam_ref[...]
    lam = (jnp.exp(jnp.sum(lv[0:1] * lv[1:2], axis=-1, keepdims=True))
           - jnp.exp(jnp.sum(lv[2:3] * lv[3:4], axis=-1, keepdims=True)) + lam_init)
    lane = lax.broadcasted_iota(jnp.int32, (tq, HEAD_DIM), 1)
    qs = []
    for h in range(GROUP_HEADS):
        q = q_ref[0, :, h * HEAD_DIM:(h + 1) * HEAD_DIM]
        zero = jnp.zeros_like(q)
        qs.append(jnp.concatenate([jnp.where(lane < DIFF_D, q, zero),
                                   jnp.where(lane >= DIFF_D, q, zero)], axis=0))
    res = _flash(qs, k_ref, v_ref, st_ref, HEAD_DIM, HEAD_DIM, qi, tq)
    for h, (acc, l) in enumerate(res):
        o = acc / l
        o = o[0:tq] - lam * o[tq:2 * tq]
        o = _rms(o, g_ref[...]) * (1.0 - lam_init)
        o_ref[0, :, h * HEAD_DIM:(h + 1) * HEAD_DIM] = o.astype(o_ref.dtype)


def _diff_attn(qr, kr, v, lamv, g_diff, lam_init, tq=512):
    b, s, w = qr.shape
    return pl.pallas_call(
        functools.partial(_diff_attn_kernel, tq=tq, lam_init=lam_init),
        grid=(b, s // tq),
        in_specs=[pl.BlockSpec((1, tq, w), lambda i, j: (i, j, 0)),
                  pl.BlockSpec((1, s, w), lambda i, j: (i, 0, 0)),
                  pl.BlockSpec((1, s, w), lambda i, j: (i, 0, 0)),
                  pl.BlockSpec((4, DIFF_D), lambda i, j: (0, 0)),
                  pl.BlockSpec((1, HEAD_DIM), lambda i, j: (0, 0))],
        out_specs=pl.BlockSpec((1, tq, w), lambda i, j: (i, j, 0)),
        out_shape=jax.ShapeDtypeStruct((b, s, w), BF16),
        scratch_shapes=[_flash_state(2 * tq)],
        compiler_params=_params("parallel", "arbitrary"),
        name="diff_attn",
    )(qr, kr, v, lamv, g_diff)


def _mix_cross_kernel(x_ref, ya_ref, yb_ref, yc_ref, yd_ref, wo_ref, g_ref, wq_ref, kv_ref,
                      wco_ref, o_ref, oh_ref, *, scale):
    acc = x_ref[...]
    for m, y_ref in enumerate((ya_ref, yb_ref, yc_ref, yd_ref)):
        acc = acc + _dot(y_ref[...], wo_ref[m * GROUP_W:(m + 1) * GROUP_W, :])
    o_ref[...] = acc
    q = _dot(_rms(o_ref[...], g_ref[...]).astype(BF16), wq_ref[...]).astype(BF16)
    for h in range(CROSS_HEADS):
        cols = slice(h * CROSS_HEAD_DIM, (h + 1) * CROSS_HEAD_DIM)
        k = kv_ref[0, :, cols]
        v = kv_ref[0, :, CROSS_W + h * CROSS_HEAD_DIM:CROSS_W + (h + 1) * CROSS_HEAD_DIM]
        s = _dot_nt(q[:, cols], k) * scale
        p = jnp.exp(s - jnp.max(s, axis=-1, keepdims=True))
        l = jnp.sum(p, axis=-1, keepdims=True)
        oh_ref[:, cols] = (_dot(p.astype(BF16), v) / l).astype(BF16)
    o_ref[...] += _dot(oh_ref[...], wco_ref[...])


def _mix_cross(x, ys, w_o, g, wq, kv, wco, seq, l, tm=512):
    t, d = x.shape
    per_b = seq // tm
    m = kv.shape[1]
    once = pl.Buffered(1)
    yspec = pl.BlockSpec((tm, GROUP_W), lambda i: (i, 0))
    return pl.pallas_call(
        functools.partial(_mix_cross_kernel, scale=CROSS_HEAD_DIM ** -0.5),
        grid=(t // tm,),
        in_specs=[pl.BlockSpec((tm, d), lambda i: (i, 0)), yspec, yspec, yspec, yspec,
                  pl.BlockSpec((None, w_o.shape[1], d), lambda i: (l, 0, 0), pipeline_mode=once),
                  pl.BlockSpec((1, d), lambda i: (0, 0)),
                  pl.BlockSpec((None, d, CROSS_W), lambda i: (l, 0, 0), pipeline_mode=once),
                  pl.BlockSpec((1, m, 2 * CROSS_W), lambda i: (i // per_b, 0, 0)),
                  pl.BlockSpec((None, CROSS_W, d), lambda i: (l, 0, 0), pipeline_mode=once)],
        out_specs=pl.BlockSpec((tm, d), lambda i: (i, 0)),
        out_shape=jax.ShapeDtypeStruct((t, d), F32),
        scratch_shapes=[pltpu.VMEM((tm, CROSS_W), BF16)],
        compiler_params=_params("parallel"),
        name="mix_cross",
    )(x, *ys, w_o, g, wq, kv, wco)


def _ffn_kernel(x_ref, g_ref, wg_ref, wu_ref, wd_ref, gf_ref, o_ref, xn_ref, *, final):
    j = pl.program_id(1)

    @pl.when(j == 0)
    def _():
        x = x_ref[...]
        xn_ref[...] = _rms(x, g_ref[...]).astype(BF16)
        o_ref[...] = x

    xn = xn_ref[...]
    half = wg_ref.shape[1] // 2
    hids = []
    for c in range(2):
        cols = slice(c * half, (c + 1) * half)
        gate = _dot(xn, wg_ref[:, cols])
        up = _dot(xn, wu_ref[:, cols])
        hids.append((gate * jax.nn.sigmoid(gate) * up).astype(BF16))
    o_ref[...] += _dot(hids[0], wd_ref[0:half, :]) + _dot(hids[1], wd_ref[half:2 * half, :])

    if final:
        @pl.when(j == pl.num_programs(1) - 1)
        def _():
            o_ref[...] = _rms(o_ref[...], gf_ref[...])


def _ffn(x, g, wg, wu, wd, g_final, final, l, tm=512, th=512):
    t, d = x.shape
    hdim = wg.shape[2]
    return pl.pallas_call(
        functools.partial(_ffn_kernel, final=final),
        grid=(t // tm, hdim // th),
        in_specs=[pl.BlockSpec((tm, d), lambda i, j: (i, 0)),
                  pl.BlockSpec((1, d), lambda i, j: (0, 0)),
                  pl.BlockSpec((None, d, th), lambda i, j: (l, 0, j)),
                  pl.BlockSpec((None, d, th), lambda i, j: (l, 0, j)),
                  pl.BlockSpec((None, th, d), lambda i, j: (l, j, 0)),
                  pl.BlockSpec((1, d), lambda i, j: (0, 0))],
        out_specs=pl.BlockSpec((tm, d), lambda i, j: (i, 0)),
        out_shape=jax.ShapeDtypeStruct((t, d), F32),
        scratch_shapes=[pltpu.VMEM((tm, d), BF16)],
        compiler_params=_params("parallel", "arbitrary"),
        name="swiglu",
    )(x, g, wg, wu, wd, g_final)


def _rope_tables(positions):
    pos = positions.astype(F32)[..., None]
    b, s = positions.shape

    def table(rot_dim, period):
        freqs = ROPE_THETA ** (-jnp.arange(0, rot_dim, 2, dtype=F32) / rot_dim)
        ang = pos * freqs
        ones = jnp.ones((b, s, period - rot_dim), F32)
        cos = jnp.concatenate([jnp.cos(ang), jnp.cos(ang), ones], axis=-1)
        sin = jnp.concatenate([jnp.sin(ang), jnp.sin(ang), 0.0 * ones], axis=-1)
        reps = LANES // period
        return (jnp.tile(cos, (1, 1, reps)).reshape(b * s, LANES),
                jnp.tile(sin, (1, 1, reps)).reshape(b * s, LANES))

    return table(DIFF_ROT, DIFF_D) + table(MLA_ROPE, LANES)


def _pad_heads(w, used, width):
    nl, k, _ = w.shape
    w = w.reshape(nl, k, GROUP_HEADS, used)
    w = jnp.pad(w, ((0, 0), (0, 0), (0, 0), (0, width - used)))
    return w.reshape(nl, k, GROUP_HEADS * width)


def kernel(x, mem, positions, g_mix, w_in, b_f, g_cq, g_ckv, w_uq, w_ukv, sgu_ln_g, sgu_ln_b, w_s,
           b_s, lam_q1, lam_k1, lam_q2, lam_k2, g_diff, w_o, g_mem, g_cross, w_cq, w_ck, w_cv, w_co,
           g_ffn, w_gate, w_up, w_down, g_final):
    b, s, d = x.shape
    depth = w_in.shape[0]
    t = b * s
    tables = _rope_tables(positions)
    xf = x.reshape(t, d)
    memf = mem.reshape(b * mem.shape[1], d)
    row = lambda a: a.reshape(1, -1)
    seq = lambda a: a.reshape(b, s, a.shape[-1])

    w_main = _split_w_in(w_in)
    w_uq_p = _pad_heads(w_uq, MLA_NOPE + MLA_ROPE, MLA_QK_PAD).astype(BF16)
    w_ukv_b = w_ukv.astype(BF16)
    w_o_b, w_cq_b, w_ck_b, w_cv_b, w_co_b, w_gate_b, w_up_b, w_down_b = (
        _cast_bf16(w) for w in (w_o, w_cq, w_ck, w_cv, w_co, w_gate, w_up, w_down))
    b_st = jnp.swapaxes(b_s, 1, 2)

    for l in range(depth):
        lam_init = 0.8 - 0.6 * math.exp(-0.3 * l)
        fox, fb, dq, dk, dv, y_c, qm, km, vm = _front(
            xf, row(g_mix[l]), w_main, tables, row(sgu_ln_g[l]), row(sgu_ln_b[l]), w_s, b_st,
            row(g_cq[l]), row(g_ckv[l]), w_uq_p, w_ukv_b, l)

        bf_pad = jnp.pad(b_f[l], (FORGET_LANE, LANES - FORGET_LANE - GROUP_HEADS)).reshape(1, LANES)
        fcol, frow = _fox_gate(seq(fb), bf_pad)
        y_a = _fox_attn(seq(fox), fcol, frow)
        y_b = _mla_attn(seq(qm), seq(km), seq(vm))
        lamv = jnp.stack([lam_q1[l], lam_k1[l], lam_q2[l], lam_k2[l]])
        y_d = _diff_attn(seq(dq), seq(dk), seq(dv), lamv, row(g_diff[l]), lam_init)

        ys = [y_a.reshape(t, GROUP_W), y_b.reshape(t, GROUP_W), y_c, y_d.reshape(t, GROUP_W)]
        kv = _mem_kv(memf, row(g_mem), w_ck_b, w_cv_b, l).reshape(b, mem.shape[1], 2 * CROSS_W)
        xf = _mix_cross(xf, ys, w_o_b, row(g_cross[l]), w_cq_b, kv, w_co_b, s, l)

        xf = _ffn(xf, row(g_ffn[l]), w_gate_b, w_up_b, w_down_b, row(g_final),
                  final=(l == depth - 1), l=l)
    return xf.reshape(b, s, d)
```

```python
import functools
import math

import jax
import jax.numpy as jnp
from jax import lax
from jax.experimental import pallas as pl
from jax.experimental.pallas import tpu as pltpu

F32 = jnp.float32
BF16 = jnp.bfloat16

HEAD_DIM = 128
GROUP_HEADS = 4
GROUP_W = GROUP_HEADS * HEAD_DIM
MLA_Q_RANK = 384
MLA_KV_RANK = 256
MLA_NOPE = 128
MLA_ROPE = 64
MLA_V = 128
MLA_QK_PAD = 256
SGU_CHUNK = 128
DIFF_D = HEAD_DIM // 2
DIFF_ROT = DIFF_D // 4
CROSS_HEADS = 4
CROSS_HEAD_DIM = 128
CROSS_W = CROSS_HEADS * CROSS_HEAD_DIM
ROPE_THETA = 500000.0
EPS = 1e-6
LOG2E = math.log2(math.e)
LANES = 128

P_FOX = 0
P_DIFF = 3 * GROUP_W
P_SGU = 6 * GROUP_W
P_MLA = 8 * GROUP_W
MLA_SRC_W = MLA_Q_RANK + MLA_KV_RANK + MLA_ROPE
MLA_BLOCK_W = MLA_SRC_W + LANES - MLA_ROPE
P_W = P_MLA + MLA_BLOCK_W
FORGET_LANE = MLA_ROPE

VMEM_LIMIT = 52 * 2**20


def _params(*sem, vmem_limit=VMEM_LIMIT):
    return pltpu.CompilerParams(dimension_semantics=sem, vmem_limit_bytes=vmem_limit)


def _rms(x, g):
    return x * lax.rsqrt(jnp.mean(x * x, axis=-1, keepdims=True) + EPS) * g


def _dot(a, b):
    return jnp.dot(a, b, preferred_element_type=F32)


def _dot_nt(a, b):
    return lax.dot_general(a, b, (((1,), (1,)), ((), ())), preferred_element_type=F32)


def _cast_kernel(w_ref, o_ref):
    o_ref[...] = w_ref[...].astype(o_ref.dtype)


def _cast_bf16(w, block_bytes=4 * 2**20):
    nl, k, n = w.shape
    tk = k
    while tk * n * 4 > block_bytes and tk % 16 == 0:
        tk //= 2
    return pl.pallas_call(
        _cast_kernel,
        grid=(nl, k // tk),
        in_specs=[pl.BlockSpec((1, tk, n), lambda l, i: (l, i, 0))],
        out_specs=pl.BlockSpec((1, tk, n), lambda l, i: (l, i, 0)),
        out_shape=jax.ShapeDtypeStruct(w.shape, BF16),
        compiler_params=_params("parallel", "parallel"),
        name="cast_bf16",
    )(w)


_SRC_FORGET = 3 * GROUP_W
_SRC_MLA = _SRC_FORGET + GROUP_HEADS
_SRC_SGU = _SRC_MLA + MLA_SRC_W
_SRC_DIFF = _SRC_SGU + 2 * GROUP_W


def _split_w_in_kernel(w_ref, main_ref):
    latent_w = MLA_Q_RANK + MLA_KV_RANK
    copies = ((P_FOX, 0, 3 * GROUP_W), (P_DIFF, _SRC_DIFF, 3 * GROUP_W),
              (P_SGU, _SRC_SGU, 2 * GROUP_W), (P_MLA, _SRC_MLA, latent_w))
    for dst, src, width in copies:
        main_ref[0, :, dst:dst + width] = w_ref[0, :, src:src + width].astype(BF16)
    key = w_ref[0, :, _SRC_MLA + latent_w:_SRC_MLA + latent_w + LANES]
    forget = w_ref[0, :, _SRC_FORGET - FORGET_LANE:_SRC_FORGET - FORGET_LANE + LANES]
    lane = lax.broadcasted_iota(jnp.int32, key.shape, 1)
    group = jnp.where(lane < MLA_ROPE, key,
                      jnp.where(lane < FORGET_LANE + GROUP_HEADS, forget, 0.0))
    main_ref[0, :, P_MLA + latent_w:P_W] = group.astype(BF16)


def _split_w_in(w, tk=256):
    nl, k, n = w.shape
    return pl.pallas_call(
        _split_w_in_kernel,
        grid=(nl, k // tk),
        in_specs=[pl.BlockSpec((1, tk, n), lambda l, i: (l, i, 0))],
        out_specs=pl.BlockSpec((1, tk, P_W), lambda l, i: (l, i, 0)),
        out_shape=jax.ShapeDtypeStruct((nl, k, P_W), BF16),
        compiler_params=_params("parallel", "parallel"),
        name="split_w_in",
    )(w)


def _rope_pairs(x, cos, sin, half, period):
    lane = lax.broadcasted_iota(jnp.int32, x.shape, x.ndim - 1)
    first = (lane & (period - 1)) < half
    rot = jnp.where(first, -pltpu.roll(x, LANES - half, x.ndim - 1), pltpu.roll(x, half, x.ndim - 1))
    return x * cos + rot * sin


def _sgu_mix(uv, lg_ref, lb_ref, ws_ref, bs_ref, o_ref):
    z = 0.5 * uv * (1.0 + jnp.tanh(math.sqrt(2.0 / math.pi) * (uv + 0.044715 * (uv * uv * uv))))
    u = z[:, 0:GROUP_W]
    v = z[:, GROUP_W:2 * GROUP_W]
    xc = v - jnp.mean(v, axis=-1, keepdims=True)
    vn = xc * lax.rsqrt(jnp.mean(xc * xc, axis=-1, keepdims=True) + EPS) * lg_ref[...] + lb_ref[...]
    vn = vn.astype(BF16)
    t = SGU_CHUNK
    tril = (lax.broadcasted_iota(jnp.int32, (t, t), 1) <= lax.broadcasted_iota(jnp.int32, (t, t), 0))
    for g in range(GROUP_HEADS):
        w = jnp.where(tril, ws_ref[g], 0.0).astype(BF16)
        bias = bs_ref[:, g:g + 1]
        for c in range(uv.shape[0] // t):
            rows = slice(c * t, (c + 1) * t)
            cols = slice(g * HEAD_DIM, (g + 1) * HEAD_DIM)
            mixed = _dot(w, vn[rows, cols]) + bias
            o_ref[rows, cols] = (u[rows, cols] * mixed).astype(o_ref.dtype)


def _mla_expand(lat, gq_ref, gkv_ref, wq_ref, wkv_ref, cos, sin, q_ref, k_ref, v_ref, qscale):
    cq = lat[:, 0:MLA_Q_RANK]
    ckv = lat[:, MLA_Q_RANK:MLA_Q_RANK + MLA_KV_RANK]
    kr = lat[:, MLA_Q_RANK + MLA_KV_RANK:MLA_BLOCK_W]
    lane = lax.broadcasted_iota(jnp.int32, kr.shape, 1)
    kr = jnp.where(lane < MLA_ROPE, kr, 0.0)
    q = _dot(_rms(cq, gq_ref[...]).astype(BF16), wq_ref[...])
    kv = _dot(_rms(ckv, gkv_ref[...]).astype(BF16), wkv_ref[...])
    kr = _rope_pairs(kr, cos, sin, MLA_ROPE // 2, LANES).astype(BF16)
    for h in range(GROUP_HEADS):
        o = h * MLA_QK_PAD
        q_ref[:, o:o + LANES] = (q[:, o:o + LANES] * qscale).astype(BF16)
        q_ref[:, o + LANES:o + 2 * LANES] = (_rope_pairs(
            q[:, o + LANES:o + 2 * LANES], cos, sin, MLA_ROPE // 2, LANES) * qscale).astype(BF16)
        k_ref[:, o:o + LANES] = kv[:, o:o + LANES].astype(BF16)
        k_ref[:, o + LANES:o + 2 * LANES] = kr
        v_ref[:, h * MLA_V:(h + 1) * MLA_V] = kv[:, o + LANES:o + 2 * LANES].astype(BF16)


def _front_kernel(x_ref, g_ref, w_ref, cosd_ref, sind_ref, cosm_ref, sinm_ref,
                  lg_ref, lb_ref, ws_ref, bs_ref, gq_ref, gkv_ref, wuq_ref, wukv_ref,
                  fox_ref, fb_ref, dq_ref, dk_ref, dv_ref, yc_ref, qm_ref, km_ref, vm_ref,
                  *, fox_scale, diff_scale, mla_scale):
    xb = _rms(x_ref[...], g_ref[...]).astype(BF16)

    lat = _dot(xb, w_ref[:, P_MLA:P_MLA + MLA_BLOCK_W])
    fb_ref[...] = lat[:, MLA_BLOCK_W - LANES:MLA_BLOCK_W]
    _mla_expand(lat, gq_ref, gkv_ref, wuq_ref, wukv_ref, cosm_ref[...], sinm_ref[...],
                qm_ref, km_ref, vm_ref, mla_scale)

    uv = _dot(xb, w_ref[:, P_SGU:P_SGU + 2 * GROUP_W])
    _sgu_mix(uv, lg_ref, lb_ref, ws_ref, bs_ref, yc_ref)

    d = _dot(xb, w_ref[:, P_DIFF:P_DIFF + 3 * GROUP_W])
    cos = cosd_ref[...]
    sin = sind_ref[...]
    for c in range(GROUP_W // LANES):
        lanes = slice(c * LANES, (c + 1) * LANES)
        klanes = slice(GROUP_W + c * LANES, GROUP_W + (c + 1) * LANES)
        dq_ref[:, lanes] = (_rope_pairs(d[:, lanes], cos, sin, DIFF_ROT // 2, DIFF_D)
                            * diff_scale).astype(BF16)
        dk_ref[:, lanes] = _rope_pairs(d[:, klanes], cos, sin, DIFF_ROT // 2, DIFF_D).astype(BF16)
    dv_ref[...] = d[:, 2 * GROUP_W:3 * GROUP_W].astype(BF16)

    fox = _dot(xb, w_ref[:, P_FOX:P_FOX + 3 * GROUP_W])
    fox_ref[:, 0:GROUP_W] = (fox[:, 0:GROUP_W] * fox_scale).astype(BF16)
    fox_ref[:, GROUP_W:3 * GROUP_W] = fox[:, GROUP_W:3 * GROUP_W].astype(BF16)


def _front(x, g, w_main, tables, ln_g, ln_b, w_s, b_st, g_cq, g_ckv, w_uq, w_ukv, l, tm=256):
    t, d = x.shape
    hq = GROUP_HEADS * MLA_QK_PAD
    once = pl.Buffered(1)
    rowblk = lambda width: pl.BlockSpec((tm, width), lambda i: (i, 0))
    vec = lambda width: pl.BlockSpec((1, width), lambda i: (0, 0))
    widths = (3 * GROUP_W, LANES, GROUP_W, GROUP_W, GROUP_W, GROUP_W, hq, hq, GROUP_W)
    dtypes = (BF16, F32, BF16, BF16, BF16, BF16, BF16, BF16, BF16)
    return pl.pallas_call(
        functools.partial(_front_kernel, fox_scale=HEAD_DIM ** -0.5 * LOG2E,
                          diff_scale=DIFF_D ** -0.5 * LOG2E,
                          mla_scale=(MLA_NOPE + MLA_ROPE) ** -0.5 * LOG2E),
        grid=(t // tm,),
        in_specs=[rowblk(d), vec(d),
                  pl.BlockSpec((None, d, P_W), lambda i: (l, 0, 0), pipeline_mode=once),
                  rowblk(LANES), rowblk(LANES), rowblk(LANES), rowblk(LANES),
                  vec(GROUP_W), vec(GROUP_W),
                  pl.BlockSpec((None, GROUP_HEADS, SGU_CHUNK, SGU_CHUNK), lambda i: (l, 0, 0, 0)),
                  pl.BlockSpec((None, SGU_CHUNK, GROUP_HEADS), lambda i: (l, 0, 0)),
                  vec(MLA_Q_RANK), vec(MLA_KV_RANK),
                  pl.BlockSpec((None, MLA_Q_RANK, hq), lambda i: (l, 0, 0)),
                  pl.BlockSpec((None, MLA_KV_RANK, hq), lambda i: (l, 0, 0))],
        out_specs=[rowblk(w) for w in widths],
        out_shape=[jax.ShapeDtypeStruct((t, w), dt) for w, dt in zip(widths, dtypes)],
        compiler_params=_params("parallel"),
        name="front",
    )(x, g, w_main, *tables, ln_g, ln_b, w_s, b_st, g_cq, g_ckv, w_uq, w_ukv)


def _mem_kv_kernel(x_ref, g_ref, wk_ref, wv_ref, o_ref):
    xb = _rms(x_ref[...], g_ref[...]).astype(BF16)
    o_ref[:, 0:CROSS_W] = _dot(xb, wk_ref[...]).astype(o_ref.dtype)
    o_ref[:, CROSS_W:2 * CROSS_W] = _dot(xb, wv_ref[...]).astype(o_ref.dtype)


def _mem_kv(x, g, wk, wv, l, tm=512):
    t, d = x.shape
    wspec = pl.BlockSpec((None, d, CROSS_W), lambda i: (l, 0, 0))
    return pl.pallas_call(
        _mem_kv_kernel,
        grid=(t // tm,),
        in_specs=[pl.BlockSpec((tm, d), lambda i: (i, 0)),
                  pl.BlockSpec((1, d), lambda i: (0, 0)), wspec, wspec],
        out_specs=pl.BlockSpec((tm, 2 * CROSS_W), lambda i: (i, 0)),
        out_shape=jax.ShapeDtypeStruct((t, 2 * CROSS_W), BF16),
        compiler_params=_params("parallel"),
        name="mem_kv_proj",
    )(x, g, wk, wv)


def _fox_gate_kernel(f_ref, b_ref, col_ref, row_ref):
    x = f_ref[0] + b_ref[...]
    c = jnp.minimum(x, 0.0) - jnp.log1p(jnp.exp(-jnp.abs(x)))
    s = x.shape[0]
    ridx = lax.broadcasted_iota(jnp.int32, x.shape, 0)
    shift = 1
    while shift < s:
        c = c + jnp.where(ridx >= shift, pltpu.roll(c, shift, 0), 0.0)
        shift *= 2
    c = c * LOG2E
    col_ref[0] = c
    row_ref[0] = c.T[FORGET_LANE:FORGET_LANE + 8, :]


def _fox_gate(fb, b_f):
    b, s, _ = fb.shape
    return pl.pallas_call(
        _fox_gate_kernel,
        grid=(b,),
        in_specs=[pl.BlockSpec((1, s, LANES), lambda i: (i, 0, 0)),
                  pl.BlockSpec((1, LANES), lambda i: (0, 0))],
        out_specs=[pl.BlockSpec((1, s, LANES), lambda i: (i, 0, 0)),
                   pl.BlockSpec((1, 8, s), lambda i: (i, 0, 0))],
        out_shape=[jax.ShapeDtypeStruct((b, s, LANES), F32),
                   jax.ShapeDtypeStruct((b, 8, s), F32)],
        compiler_params=_params("parallel"),
        name="fox_gate",
    )(fb, b_f)


def _flash(qs, k_ref, v_ref, st_ref, dk, dv, qi, tq, fqs=None, frow_ref=None):
    nh = len(qs)
    r = qs[0].shape[0]
    assert dv == LANES and tq % (2 * LANES) == 0
    if fqs is not None:
        fqs = [jnp.broadcast_to(f, (r, LANES)) for f in fqs]

    def scores(h, rows, start, nk, diag_row0=None):
        s = _dot_nt(qs[h][rows], k_ref[0, pl.ds(start, nk), h * dk:(h + 1) * dk])
        if frow_ref is not None:
            s = s - frow_ref[0, h:h + 1, pl.ds(start, nk)]
        if diag_row0 is not None:
            row = lax.broadcasted_iota(jnp.int32, s.shape, 0) + diag_row0
            col = lax.broadcasted_iota(jnp.int32, s.shape, 1)
            s = jnp.where(col <= row, s, -jnp.inf)
        return s

    def absorb(s, h, rows, start, old):
        nr, nk = s.shape
        mz = jnp.broadcast_to(jnp.max(s, axis=-1, keepdims=True), (nr, LANES))
        if fqs is not None:
            mz = mz + fqs[h][rows]
        if old is None:
            m_new = mz
        else:
            m_new = jnp.maximum(old[0], mz)
            alpha = jnp.exp2(old[0] - m_new)
        shift = (fqs[h][rows] - m_new) if fqs is not None else -m_new
        p = jnp.exp2(s + jnp.concatenate([shift] * (nk // LANES), axis=1))
        psum = p[:, 0:LANES]
        for g in range(1, nk // LANES):
            psum = psum + p[:, g * LANES:(g + 1) * LANES]
        pv = _dot(p.astype(BF16), v_ref[0, pl.ds(start, nk), h * dv:(h + 1) * dv])
        if old is None:
            return m_new, psum, pv
        return m_new, alpha * old[1] + psum, alpha * old[2] + pv

    half = tq // 2
    diag_start = pl.multiple_of(qi * tq, tq)
    pieces = []
    for rep in range(r // tq):
        pieces.append((slice(rep * tq, rep * tq + half), half, 0))
        pieces.append((slice(rep * tq + half, (rep + 1) * tq), tq, half))
    diag = [[scores(h, rows, diag_start, nk, row0) for rows, nk, row0 in pieces] for h in range(nh)]
    for h in range(nh):
        parts = [absorb(s, h, rows, diag_start, None) for s, (rows, _, _) in zip(diag[h], pieces)]
        for i in range(3):
            st_ref[i, h] = jnp.concatenate([part[i] for part in parts], axis=0)

    def body(j, c):
        start = pl.multiple_of(j * tq, tq)
        every = slice(0, r)
        blocks = [scores(h, every, start, tq) for h in range(nh)]
        for h in range(nh):
            new = absorb(blocks[h], h, every, start, (st_ref[0, h], st_ref[1, h], st_ref[2, h]))
            for i in range(3):
                st_ref[i, h] = new[i]
        return c

    lax.fori_loop(0, qi, body, 0)
    return [(st_ref[2, h], jnp.sum(st_ref[1, h], axis=-1, keepdims=True)) for h in range(nh)]


def _flash_state(rows):
    return pltpu.VMEM((3, GROUP_HEADS, rows, LANES), F32)


def _fox_attn_kernel(q_ref, k_ref, v_ref, fcol_ref, frow_ref, o_ref, st_ref, *, tq):
    qi = pl.program_id(1)
    qs = [q_ref[0, :, h * HEAD_DIM:(h + 1) * HEAD_DIM] for h in range(GROUP_HEADS)]
    fqs = [fcol_ref[0, :, FORGET_LANE + h:FORGET_LANE + h + 1] for h in range(GROUP_HEADS)]
    res = _flash(qs, k_ref, v_ref, st_ref, HEAD_DIM, HEAD_DIM, qi, tq, fqs, frow_ref)
    for h, (acc, l) in enumerate(res):
        o_ref[0, :, h * HEAD_DIM:(h + 1) * HEAD_DIM] = (acc / l).astype(o_ref.dtype)


def _fox_attn(qkv, fcol, frow, tq=512):
    b, s, _ = qkv.shape
    w = GROUP_W
    return pl.pallas_call(
        functools.partial(_fox_attn_kernel, tq=tq),
        grid=(b, s // tq),
        in_specs=[pl.BlockSpec((1, tq, w), lambda i, j: (i, j, 0)),
                  pl.BlockSpec((1, s, w), lambda i, j: (i, 0, 1)),
                  pl.BlockSpec((1, s, w), lambda i, j: (i, 0, 2)),
                  pl.BlockSpec((1, tq, LANES), lambda i, j: (i, j, 0)),
                  pl.BlockSpec((1, 8, s), lambda i, j: (i, 0, 0))],
        out_specs=pl.BlockSpec((1, tq, w), lambda i, j: (i, j, 0)),
        out_shape=jax.ShapeDtypeStruct((b, s, w), BF16),
        scratch_shapes=[_flash_state(tq)],
        compiler_params=_params("parallel", "arbitrary"),
        name="fox_attn",
    )(qkv, qkv, qkv, fcol, frow)


def _mla_attn_kernel(q_ref, k_ref, v_ref, o_ref, st_ref, *, tq):
    qi = pl.program_id(1)
    qs = [q_ref[0, :, h * MLA_QK_PAD:(h + 1) * MLA_QK_PAD] for h in range(GROUP_HEADS)]
    res = _flash(qs, k_ref, v_ref, st_ref, MLA_QK_PAD, MLA_V, qi, tq)
    for h, (acc, l) in enumerate(res):
        o_ref[0, :, h * MLA_V:(h + 1) * MLA_V] = (acc / l).astype(o_ref.dtype)


def _mla_attn(qm, km, vm, tq=512):
    b, s, hq = qm.shape
    return pl.pallas_call(
        functools.partial(_mla_attn_kernel, tq=tq),
        grid=(b, s // tq),
        in_specs=[pl.BlockSpec((1, tq, hq), lambda i, j: (i, j, 0)),
                  pl.BlockSpec((1, s, hq), lambda i, j: (i, 0, 0)),
                  pl.BlockSpec((1, s, GROUP_W), lambda i, j: (i, 0, 0))],
        out_specs=pl.BlockSpec((1, tq, GROUP_W), lambda i, j: (i, j, 0)),
        out_shape=jax.ShapeDtypeStruct((b, s, GROUP_W), BF16),
        scratch_shapes=[_flash_state(tq)],
        compiler_params=_params("parallel", "arbitrary"),
        name="mla_attn",
    )(qm, km, vm)


def _diff_attn_kernel(q_ref, k_ref, v_ref, lam_ref, g_ref, o_ref, st_ref, *, tq, lam_init):
    qi = pl.program_id(1)
    lv = lam_ref[...]
    lam = (jnp.exp(jnp.sum(lv[0:1] * lv[1:2], axis=-1, keepdims=True))
           - jnp.exp(jnp.sum(lv[2:3] * lv[3:4], axis=-1, keepdims=True)) + lam_init)
    lane = lax.broadcasted_iota(jnp.int32, (tq, HEAD_DIM), 1)
    qs = []
    for h in range(GROUP_HEADS):
        q = q_ref[0, :, h * HEAD_DIM:(h + 1) * HEAD_DIM]
        zero = jnp.zeros_like(q)
        qs.append(jnp.concatenate([jnp.where(lane < DIFF_D, q, zero),
                                   jnp.where(lane >= DIFF_D, q, zero)], axis=0))
    res = _flash(qs, k_ref, v_ref, st_ref, HEAD_DIM, HEAD_DIM, qi, tq)
    for h, (acc, l) in enumerate(res):
        o = acc / l
        o = o[0:tq] - lam * o[tq:2 * tq]
        o = _rms(o, g_ref[...]) * (1.0 - lam_init)
        o_ref[0, :, h * HEAD_DIM:(h + 1) * HEAD_DIM] = o.astype(o_ref.dtype)


def _diff_attn(qr, kr, v, lamv, g_diff, lam_init, tq=512):
    b, s, w = qr.shape
    return pl.pallas_call(
        functools.partial(_diff_attn_kernel, tq=tq, lam_init=lam_init),
        grid=(b, s // tq),
        in_specs=[pl.BlockSpec((1, tq, w), lambda i, j: (i, j, 0)),
                  pl.BlockSpec((1, s, w), lambda i, j: (i, 0, 0)),
                  pl.BlockSpec((1, s, w), lambda i, j: (i, 0, 0)),
                  pl.BlockSpec((4, DIFF_D), lambda i, j: (0, 0)),
                  pl.BlockSpec((1, HEAD_DIM), lambda i, j: (0, 0))],
        out_specs=pl.BlockSpec((1, tq, w), lambda i, j: (i, j, 0)),
        out_shape=jax.ShapeDtypeStruct((b, s, w), BF16),
        scratch_shapes=[_flash_state(2 * tq)],
        compiler_params=_params("parallel", "arbitrary"),
        name="diff_attn",
    )(qr, kr, v, lamv, g_diff)


def _mix_cross_kernel(x_ref, ya_ref, yb_ref, yc_ref, yd_ref, wo_ref, g_ref, wq_ref, kv_ref,
                      wco_ref, o_ref, oh_ref, *, scale):
    acc = x_ref[...]
    for m, y_ref in enumerate((ya_ref, yb_ref, yc_ref, yd_ref)):
        acc = acc + _dot(y_ref[...], wo_ref[m * GROUP_W:(m + 1) * GROUP_W, :])
    o_ref[...] = acc
    q = _dot(_rms(o_ref[...], g_ref[...]).astype(BF16), wq_ref[...]).astype(BF16)
    for h in range(CROSS_HEADS):
        cols = slice(h * CROSS_HEAD_DIM, (h + 1) * CROSS_HEAD_DIM)
        k = kv_ref[0, :, cols]
        v = kv_ref[0, :, CROSS_W + h * CROSS_HEAD_DIM:CROSS_W + (h + 1) * CROSS_HEAD_DIM]
        s = _dot_nt(q[:, cols], k) * scale
        p = jnp.exp(s - jnp.max(s, axis=-1, keepdims=True))
        l = jnp.sum(p, axis=-1, keepdims=True)
        oh_ref[:, cols] = (_dot(p.astype(BF16), v) / l).astype(BF16)
    o_ref[...] += _dot(oh_ref[...], wco_ref[...])


def _mix_cross(x, ys, w_o, g, wq, kv, wco, seq, l, tm=512):
    t, d = x.shape
    per_b = seq // tm
    m = kv.shape[1]
    once = pl.Buffered(1)
    yspec = pl.BlockSpec((tm, GROUP_W), lambda i: (i, 0))
    return pl.pallas_call(
        functools.partial(_mix_cross_kernel, scale=CROSS_HEAD_DIM ** -0.5),
        grid=(t // tm,),
        in_specs=[pl.BlockSpec((tm, d), lambda i: (i, 0)), yspec, yspec, yspec, yspec,
                  pl.BlockSpec((None, w_o.shape[1], d), lambda i: (l, 0, 0), pipeline_mode=once),
                  pl.BlockSpec((1, d), lambda i: (0, 0)),
                  pl.BlockSpec((None, d, CROSS_W), lambda i: (l, 0, 0), pipeline_mode=once),
                  pl.BlockSpec((1, m, 2 * CROSS_W), lambda i: (i // per_b, 0, 0)),
                  pl.BlockSpec((None, CROSS_W, d), lambda i: (l, 0, 0), pipeline_mode=once)],
        out_specs=pl.BlockSpec((tm, d), lambda i: (i, 0)),
        out_shape=jax.ShapeDtypeStruct((t, d), F32),
        scratch_shapes=[pltpu.VMEM((tm, CROSS_W), BF16)],
        compiler_params=_params("parallel"),
        name="mix_cross",
    )(x, *ys, w_o, g, wq, kv, wco)


def _ffn_kernel(x_ref, g_ref, wg_ref, wu_ref, wd_ref, gf_ref, o_ref, xn_ref, *, final):
    j = pl.program_id(1)

    @pl.when(j == 0)
    def _():
        x = x_ref[...]
        xn_ref[...] = _rms(x, g_ref[...]).astype(BF16)
        o_ref[...] = x

    xn = xn_ref[...]
    half = wg_ref.shape[1] // 2
    hids = []
    for c in range(2):
        cols = slice(c * half, (c + 1) * half)
        gate = _dot(xn, wg_ref[:, cols])
        up = _dot(xn, wu_ref[:, cols])
        hids.append((gate * jax.nn.sigmoid(gate) * up).astype(BF16))
    o_ref[...] += _dot(hids[0], wd_ref[0:half, :]) + _dot(hids[1], wd_ref[half:2 * half, :])

    if final:
        @pl.when(j == pl.num_programs(1) - 1)
        def _():
            o_ref[...] = _rms(o_ref[...], gf_ref[...])


def _ffn(x, g, wg, wu, wd, g_final, final, l, tm=1024, th=512):
    t, d = x.shape
    hdim = wg.shape[2]
    windows = 2 * 2 * tm * d * 4 + 2 * 3 * d * th * 2 + tm * d * 2
    vmem = windows + 8 * 2**20
    return pl.pallas_call(
        functools.partial(_ffn_kernel, final=final),
        grid=(t // tm, hdim // th),
        in_specs=[pl.BlockSpec((tm, d), lambda i, j: (i, 0)),
                  pl.BlockSpec((1, d), lambda i, j: (0, 0)),
                  pl.BlockSpec((None, d, th), lambda i, j: (l, 0, j)),
                  pl.BlockSpec((None, d, th), lambda i, j: (l, 0, j)),
                  pl.BlockSpec((None, th, d), lambda i, j: (l, j, 0)),
                  pl.BlockSpec((1, d), lambda i, j: (0, 0))],
        out_specs=pl.BlockSpec((tm, d), lambda i, j: (i, 0)),
        out_shape=jax.ShapeDtypeStruct((t, d), F32),
        scratch_shapes=[pltpu.VMEM((tm, d), BF16)],
        compiler_params=_params("parallel", "arbitrary", vmem_limit=vmem),
        name="swiglu",
    )(x, g, wg, wu, wd, g_final)


def _rope_tables(positions):
    pos = positions.astype(F32)[..., None]
    b, s = positions.shape

    def table(rot_dim, period):
        freqs = ROPE_THETA ** (-jnp.arange(0, rot_dim, 2, dtype=F32) / rot_dim)
        ang = pos * freqs
        ones = jnp.ones((b, s, period - rot_dim), F32)
        cos = jnp.concatenate([jnp.cos(ang), jnp.cos(ang), ones], axis=-1)
        sin = jnp.concatenate([jnp.sin(ang), jnp.sin(ang), 0.0 * ones], axis=-1)
        reps = LANES // period
        return (jnp.tile(cos, (1, 1, reps)).reshape(b * s, LANES),
                jnp.tile(sin, (1, 1, reps)).reshape(b * s, LANES))

    return table(DIFF_ROT, DIFF_D) + table(MLA_ROPE, LANES)


def _pad_heads(w, used, width):
    nl, k, _ = w.shape
    w = w.reshape(nl, k, GROUP_HEADS, used)
    w = jnp.pad(w, ((0, 0), (0, 0), (0, 0), (0, width - used)))
    return w.reshape(nl, k, GROUP_HEADS * width)


def kernel(x, mem, positions, g_mix, w_in, b_f, g_cq, g_ckv, w_uq, w_ukv, sgu_ln_g, sgu_ln_b, w_s,
           b_s, lam_q1, lam_k1, lam_q2, lam_k2, g_diff, w_o, g_mem, g_cross, w_cq, w_ck, w_cv, w_co,
           g_ffn, w_gate, w_up, w_down, g_final):
    b, s, d = x.shape
    depth = w_in.shape[0]
    t = b * s
    tables = _rope_tables(positions)
    xf = x.reshape(t, d)
    memf = mem.reshape(b * mem.shape[1], d)
    row = lambda a: a.reshape(1, -1)
    seq = lambda a: a.reshape(b, s, a.shape[-1])

    w_main = _split_w_in(w_in)
    w_uq_p = _pad_heads(w_uq, MLA_NOPE + MLA_ROPE, MLA_QK_PAD).astype(BF16)
    w_ukv_b = w_ukv.astype(BF16)
    w_o_b, w_cq_b, w_ck_b, w_cv_b, w_co_b, w_gate_b, w_up_b, w_down_b = (
        _cast_bf16(w) for w in (w_o, w_cq, w_ck, w_cv, w_co, w_gate, w_up, w_down))
    b_st = jnp.swapaxes(b_s, 1, 2)

    for l in range(depth):
        lam_init = 0.8 - 0.6 * math.exp(-0.3 * l)
        fox, fb, dq, dk, dv, y_c, qm, km, vm = _front(
            xf, row(g_mix[l]), w_main, tables, row(sgu_ln_g[l]), row(sgu_ln_b[l]), w_s, b_st,
            row(g_cq[l]), row(g_ckv[l]), w_uq_p, w_ukv_b, l)

        bf_pad = jnp.pad(b_f[l], (FORGET_LANE, LANES - FORGET_LANE - GROUP_HEADS)).reshape(1, LANES)
        fcol, frow = _fox_gate(seq(fb), bf_pad)
        y_a = _fox_attn(seq(fox), fcol, frow)
        y_b = _mla_attn(seq(qm), seq(km), seq(vm))
        lamv = jnp.stack([lam_q1[l], lam_k1[l], lam_q2[l], lam_k2[l]])
        y_d = _diff_attn(seq(dq), seq(dk), seq(dv), lamv, row(g_diff[l]), lam_init)

        ys = [y_a.reshape(t, GROUP_W), y_b.reshape(t, GROUP_W), y_c, y_d.reshape(t, GROUP_W)]
        kv = _mem_kv(memf, row(g_mem), w_ck_b, w_cv_b, l).reshape(b, mem.shape[1], 2 * CROSS_W)
        xf = _mix_cross(xf, ys, w_o_b, row(g_cross[l]), w_cq_b, kv, w_co_b, s, l)

        xf = _ffn(xf, row(g_ffn[l]), w_gate_b, w_up_b, w_down_b, row(g_final),
                  final=(l == depth - 1), l=l)
    return xf.reshape(b, s, d)
```

```python
import functools
import math

import jax
import jax.numpy as jnp
from jax import lax
from jax.experimental import pallas as pl
from jax.experimental.pallas import tpu as pltpu

F32 = jnp.float32
BF16 = jnp.bfloat16

HEAD_DIM = 128
GROUP_HEADS = 4
GROUP_W = GROUP_HEADS * HEAD_DIM
MLA_Q_RANK = 384
MLA_KV_RANK = 256
MLA_NOPE = 128
MLA_ROPE = 64
MLA_V = 128
MLA_QK_PAD = 256
SGU_CHUNK = 128
DIFF_D = HEAD_DIM // 2
DIFF_ROT = DIFF_D // 4
CROSS_HEADS = 4
CROSS_HEAD_DIM = 128
CROSS_W = CROSS_HEADS * CROSS_HEAD_DIM
ROPE_THETA = 500000.0
EPS = 1e-6
LOG2E = math.log2(math.e)
LANES = 128

P_FOX = 0
P_DIFF = 3 * GROUP_W
P_SGU = 6 * GROUP_W
P_MLA = 8 * GROUP_W
MLA_SRC_W = MLA_Q_RANK + MLA_KV_RANK + MLA_ROPE
MLA_BLOCK_W = MLA_SRC_W + LANES - MLA_ROPE
P_W = P_MLA + MLA_BLOCK_W
FORGET_LANE = MLA_ROPE

VMEM_LIMIT = 52 * 2**20


def _params(*sem, vmem_limit=VMEM_LIMIT):
    return pltpu.CompilerParams(dimension_semantics=sem, vmem_limit_bytes=vmem_limit)


def _rms(x, g):
    return x * lax.rsqrt(jnp.mean(x * x, axis=-1, keepdims=True) + EPS) * g


def _dot(a, b):
    return jnp.dot(a, b, preferred_element_type=F32)


def _dot_nt(a, b):
    return lax.dot_general(a, b, (((1,), (1,)), ((), ())), preferred_element_type=F32)


def _cast_kernel(w_ref, o_ref):
    o_ref[...] = w_ref[...].astype(o_ref.dtype)


def _cast_bf16(w, block_bytes=4 * 2**20):
    nl, k, n = w.shape
    tk = k
    while tk * n * 4 > block_bytes and tk % 16 == 0:
        tk //= 2
    return pl.pallas_call(
        _cast_kernel,
        grid=(nl, k // tk),
        in_specs=[pl.BlockSpec((1, tk, n), lambda l, i: (l, i, 0))],
        out_specs=pl.BlockSpec((1, tk, n), lambda l, i: (l, i, 0)),
        out_shape=jax.ShapeDtypeStruct(w.shape, BF16),
        compiler_params=_params("parallel", "parallel"),
        name="cast_bf16",
    )(w)


_SRC_FORGET = 3 * GROUP_W
_SRC_MLA = _SRC_FORGET + GROUP_HEADS
_SRC_SGU = _SRC_MLA + MLA_SRC_W
_SRC_DIFF = _SRC_SGU + 2 * GROUP_W


def _split_w_in_kernel(w_ref, main_ref):
    latent_w = MLA_Q_RANK + MLA_KV_RANK
    copies = ((P_FOX, 0, 3 * GROUP_W), (P_DIFF, _SRC_DIFF, 3 * GROUP_W),
              (P_SGU, _SRC_SGU, 2 * GROUP_W), (P_MLA, _SRC_MLA, latent_w))
    for dst, src, width in copies:
        main_ref[0, :, dst:dst + width] = w_ref[0, :, src:src + width].astype(BF16)
    key = w_ref[0, :, _SRC_MLA + latent_w:_SRC_MLA + latent_w + LANES]
    forget = w_ref[0, :, _SRC_FORGET - FORGET_LANE:_SRC_FORGET - FORGET_LANE + LANES]
    lane = lax.broadcasted_iota(jnp.int32, key.shape, 1)
    group = jnp.where(lane < MLA_ROPE, key,
                      jnp.where(lane < FORGET_LANE + GROUP_HEADS, forget, 0.0))
    main_ref[0, :, P_MLA + latent_w:P_W] = group.astype(BF16)


def _split_w_in(w, tk=256):
    nl, k, n = w.shape
    return pl.pallas_call(
        _split_w_in_kernel,
        grid=(nl, k // tk),
        in_specs=[pl.BlockSpec((1, tk, n), lambda l, i: (l, i, 0))],
        out_specs=pl.BlockSpec((1, tk, P_W), lambda l, i: (l, i, 0)),
        out_shape=jax.ShapeDtypeStruct((nl, k, P_W), BF16),
        compiler_params=_params("parallel", "parallel"),
        name="split_w_in",
    )(w)


def _rope_pairs(x, cos, sin, half, period):
    lane = lax.broadcasted_iota(jnp.int32, x.shape, x.ndim - 1)
    first = (lane & (period - 1)) < half
    rot = jnp.where(first, -pltpu.roll(x, LANES - half, x.ndim - 1), pltpu.roll(x, half, x.ndim - 1))
    return x * cos + rot * sin


def _sgu_mix(uv, lg_ref, lb_ref, ws_ref, bs_ref, o_ref):
    z = 0.5 * uv * (1.0 + jnp.tanh(math.sqrt(2.0 / math.pi) * (uv + 0.044715 * (uv * uv * uv))))
    u = z[:, 0:GROUP_W]
    v = z[:, GROUP_W:2 * GROUP_W]
    xc = v - jnp.mean(v, axis=-1, keepdims=True)
    vn = xc * lax.rsqrt(jnp.mean(xc * xc, axis=-1, keepdims=True) + EPS) * lg_ref[...] + lb_ref[...]
    vn = vn.astype(BF16)
    t = SGU_CHUNK
    tril = (lax.broadcasted_iota(jnp.int32, (t, t), 1) <= lax.broadcasted_iota(jnp.int32, (t, t), 0))
    for g in range(GROUP_HEADS):
        w = jnp.where(tril, ws_ref[g], 0.0).astype(BF16)
        bias = bs_ref[:, g:g + 1]
        for c in range(uv.shape[0] // t):
            rows = slice(c * t, (c + 1) * t)
            cols = slice(g * HEAD_DIM, (g + 1) * HEAD_DIM)
            mixed = _dot(w, vn[rows, cols]) + bias
            o_ref[rows, cols] = (u[rows, cols] * mixed).astype(o_ref.dtype)


def _mla_expand(lat, gq_ref, gkv_ref, wq_ref, wkv_ref, cos, sin, q_ref, k_ref, v_ref, qscale):
    cq = lat[:, 0:MLA_Q_RANK]
    ckv = lat[:, MLA_Q_RANK:MLA_Q_RANK + MLA_KV_RANK]
    kr = lat[:, MLA_Q_RANK + MLA_KV_RANK:MLA_BLOCK_W]
    lane = lax.broadcasted_iota(jnp.int32, kr.shape, 1)
    kr = jnp.where(lane < MLA_ROPE, kr, 0.0)
    q = _dot(_rms(cq, gq_ref[...]).astype(BF16), wq_ref[...])
    kv = _dot(_rms(ckv, gkv_ref[...]).astype(BF16), wkv_ref[...])
    kr = _rope_pairs(kr, cos, sin, MLA_ROPE // 2, LANES).astype(BF16)
    for h in range(GROUP_HEADS):
        o = h * MLA_QK_PAD
        q_ref[:, o:o + LANES] = (q[:, o:o + LANES] * qscale).astype(BF16)
        q_ref[:, o + LANES:o + 2 * LANES] = (_rope_pairs(
            q[:, o + LANES:o + 2 * LANES], cos, sin, MLA_ROPE // 2, LANES) * qscale).astype(BF16)
        k_ref[:, o:o + LANES] = kv[:, o:o + LANES].astype(BF16)
        k_ref[:, o + LANES:o + 2 * LANES] = kr
        v_ref[:, h * MLA_V:(h + 1) * MLA_V] = kv[:, o + LANES:o + 2 * LANES].astype(BF16)


def _front_kernel(x_ref, g_ref, w_ref, cosd_ref, sind_ref, cosm_ref, sinm_ref,
                  lg_ref, lb_ref, ws_ref, bs_ref, gq_ref, gkv_ref, wuq_ref, wukv_ref,
                  fox_ref, fb_ref, dq_ref, dk_ref, dv_ref, yc_ref, qm_ref, km_ref, vm_ref,
                  *, fox_scale, diff_scale, mla_scale):
    xb = _rms(x_ref[...], g_ref[...]).astype(BF16)

    lat = _dot(xb, w_ref[:, P_MLA:P_MLA + MLA_BLOCK_W])
    fb_ref[...] = lat[:, MLA_BLOCK_W - LANES:MLA_BLOCK_W]
    _mla_expand(lat, gq_ref, gkv_ref, wuq_ref, wukv_ref, cosm_ref[...], sinm_ref[...],
                qm_ref, km_ref, vm_ref, mla_scale)

    uv = _dot(xb, w_ref[:, P_SGU:P_SGU + 2 * GROUP_W])
    _sgu_mix(uv, lg_ref, lb_ref, ws_ref, bs_ref, yc_ref)

    d = _dot(xb, w_ref[:, P_DIFF:P_DIFF + 3 * GROUP_W])
    cos = cosd_ref[...]
    sin = sind_ref[...]
    for c in range(GROUP_W // LANES):
        lanes = slice(c * LANES, (c + 1) * LANES)
        klanes = slice(GROUP_W + c * LANES, GROUP_W + (c + 1) * LANES)
        dq_ref[:, lanes] = (_rope_pairs(d[:, lanes], cos, sin, DIFF_ROT // 2, DIFF_D)
                            * diff_scale).astype(BF16)
        dk_ref[:, lanes] = _rope_pairs(d[:, klanes], cos, sin, DIFF_ROT // 2, DIFF_D).astype(BF16)
    dv_ref[...] = d[:, 2 * GROUP_W:3 * GROUP_W].astype(BF16)

    fox = _dot(xb, w_ref[:, P_FOX:P_FOX + 3 * GROUP_W])
    fox_ref[:, 0:GROUP_W] = (fox[:, 0:GROUP_W] * fox_scale).astype(BF16)
    fox_ref[:, GROUP_W:3 * GROUP_W] = fox[:, GROUP_W:3 * GROUP_W].astype(BF16)


def _front(x, g, w_main, tables, ln_g, ln_b, w_s, b_st, g_cq, g_ckv, w_uq, w_ukv, l, tm=512):
    t, d = x.shape
    hq = GROUP_HEADS * MLA_QK_PAD
    once = pl.Buffered(1)
    rowblk = lambda width: pl.BlockSpec((tm, width), lambda i: (i, 0))
    vec = lambda width: pl.BlockSpec((1, width), lambda i: (0, 0))
    widths = (3 * GROUP_W, LANES, GROUP_W, GROUP_W, GROUP_W, GROUP_W, hq, hq, GROUP_W)
    dtypes = (BF16, F32, BF16, BF16, BF16, BF16, BF16, BF16, BF16)
    return pl.pallas_call(
        functools.partial(_front_kernel, fox_scale=HEAD_DIM ** -0.5 * LOG2E,
                          diff_scale=DIFF_D ** -0.5 * LOG2E,
                          mla_scale=(MLA_NOPE + MLA_ROPE) ** -0.5 * LOG2E),
        grid=(t // tm,),
        in_specs=[rowblk(d), vec(d),
                  pl.BlockSpec((None, d, P_W), lambda i: (l, 0, 0), pipeline_mode=once),
                  rowblk(LANES), rowblk(LANES), rowblk(LANES), rowblk(LANES),
                  vec(GROUP_W), vec(GROUP_W),
                  pl.BlockSpec((None, GROUP_HEADS, SGU_CHUNK, SGU_CHUNK), lambda i: (l, 0, 0, 0)),
                  pl.BlockSpec((None, SGU_CHUNK, GROUP_HEADS), lambda i: (l, 0, 0)),
                  vec(MLA_Q_RANK), vec(MLA_KV_RANK),
                  pl.BlockSpec((None, MLA_Q_RANK, hq), lambda i: (l, 0, 0)),
                  pl.BlockSpec((None, MLA_KV_RANK, hq), lambda i: (l, 0, 0))],
        out_specs=[rowblk(w) for w in widths],
        out_shape=[jax.ShapeDtypeStruct((t, w), dt) for w, dt in zip(widths, dtypes)],
        compiler_params=_params("parallel", vmem_limit=56 * 2**20),
        name="front",
    )(x, g, w_main, *tables, ln_g, ln_b, w_s, b_st, g_cq, g_ckv, w_uq, w_ukv)


def _mem_kv_kernel(x_ref, g_ref, wk_ref, wv_ref, o_ref):
    xb = _rms(x_ref[...], g_ref[...]).astype(BF16)
    o_ref[:, 0:CROSS_W] = _dot(xb, wk_ref[...]).astype(o_ref.dtype)
    o_ref[:, CROSS_W:2 * CROSS_W] = _dot(xb, wv_ref[...]).astype(o_ref.dtype)


def _mem_kv(x, g, wk, wv, l, tm=512):
    t, d = x.shape
    wspec = pl.BlockSpec((None, d, CROSS_W), lambda i: (l, 0, 0))
    return pl.pallas_call(
        _mem_kv_kernel,
        grid=(t // tm,),
        in_specs=[pl.BlockSpec((tm, d), lambda i: (i, 0)),
                  pl.BlockSpec((1, d), lambda i: (0, 0)), wspec, wspec],
        out_specs=pl.BlockSpec((tm, 2 * CROSS_W), lambda i: (i, 0)),
        out_shape=jax.ShapeDtypeStruct((t, 2 * CROSS_W), BF16),
        compiler_params=_params("parallel"),
        name="mem_kv_proj",
    )(x, g, wk, wv)


def _fox_gate_kernel(f_ref, b_ref, col_ref, row_ref):
    x = f_ref[0] + b_ref[...]
    c = jnp.minimum(x, 0.0) - jnp.log1p(jnp.exp(-jnp.abs(x)))
    s = x.shape[0]
    ridx = lax.broadcasted_iota(jnp.int32, x.shape, 0)
    shift = 1
    while shift < s:
        c = c + jnp.where(ridx >= shift, pltpu.roll(c, shift, 0), 0.0)
        shift *= 2
    c = c * LOG2E
    col_ref[0] = c
    row_ref[0] = c.T[FORGET_LANE:FORGET_LANE + 8, :]


def _fox_gate(fb, b_f):
    b, s, _ = fb.shape
    return pl.pallas_call(
        _fox_gate_kernel,
        grid=(b,),
        in_specs=[pl.BlockSpec((1, s, LANES), lambda i: (i, 0, 0)),
                  pl.BlockSpec((1, LANES), lambda i: (0, 0))],
        out_specs=[pl.BlockSpec((1, s, LANES), lambda i: (i, 0, 0)),
                   pl.BlockSpec((1, 8, s), lambda i: (i, 0, 0))],
        out_shape=[jax.ShapeDtypeStruct((b, s, LANES), F32),
                   jax.ShapeDtypeStruct((b, 8, s), F32)],
        compiler_params=_params("parallel"),
        name="fox_gate",
    )(fb, b_f)


def _flash(qs, k_ref, v_ref, st_ref, dk, dv, qi, tq, fqs=None, frow_ref=None):
    nh = len(qs)
    r = qs[0].shape[0]
    assert dv == LANES and tq % (2 * LANES) == 0
    if fqs is not None:
        fqs = [jnp.broadcast_to(f, (r, LANES)) for f in fqs]

    def scores(h, rows, start, nk, diag_row0=None):
        s = _dot_nt(qs[h][rows], k_ref[0, pl.ds(start, nk), h * dk:(h + 1) * dk])
        if frow_ref is not None:
            s = s - frow_ref[0, h:h + 1, pl.ds(start, nk)]
        if diag_row0 is not None:
            row = lax.broadcasted_iota(jnp.int32, s.shape, 0) + diag_row0
            col = lax.broadcasted_iota(jnp.int32, s.shape, 1)
            s = jnp.where(col <= row, s, -jnp.inf)
        return s

    def absorb(s, h, rows, start, old):
        nr, nk = s.shape
        mz = jnp.broadcast_to(jnp.max(s, axis=-1, keepdims=True), (nr, LANES))
        if fqs is not None:
            mz = mz + fqs[h][rows]
        if old is None:
            m_new = mz
        else:
            m_new = jnp.maximum(old[0], mz)
            alpha = jnp.exp2(old[0] - m_new)
        shift = (fqs[h][rows] - m_new) if fqs is not None else -m_new
        p = jnp.exp2(s + jnp.concatenate([shift] * (nk // LANES), axis=1))
        psum = p[:, 0:LANES]
        for g in range(1, nk // LANES):
            psum = psum + p[:, g * LANES:(g + 1) * LANES]
        pv = _dot(p.astype(BF16), v_ref[0, pl.ds(start, nk), h * dv:(h + 1) * dv])
        if old is None:
            return m_new, psum, pv
        return m_new, alpha * old[1] + psum, alpha * old[2] + pv

    half = tq // 2
    diag_start = pl.multiple_of(qi * tq, tq)
    pieces = []
    for rep in range(r // tq):
        pieces.append((slice(rep * tq, rep * tq + half), half, 0))
        pieces.append((slice(rep * tq + half, (rep + 1) * tq), tq, half))
    diag = [[scores(h, rows, diag_start, nk, row0) for rows, nk, row0 in pieces] for h in range(nh)]
    for h in range(nh):
        parts = [absorb(s, h, rows, diag_start, None) for s, (rows, _, _) in zip(diag[h], pieces)]
        for i in range(3):
            st_ref[i, h] = jnp.concatenate([part[i] for part in parts], axis=0)

    def body(j, c):
        start = pl.multiple_of(j * tq, tq)
        every = slice(0, r)
        blocks = [scores(h, every, start, tq) for h in range(nh)]
        for h in range(nh):
            new = absorb(blocks[h], h, every, start, (st_ref[0, h], st_ref[1, h], st_ref[2, h]))
            for i in range(3):
                st_ref[i, h] = new[i]
        return c

    lax.fori_loop(0, qi, body, 0)
    return [(st_ref[2, h], jnp.sum(st_ref[1, h], axis=-1, keepdims=True)) for h in range(nh)]


def _flash_state(rows):
    return pltpu.VMEM((3, GROUP_HEADS, rows, LANES), F32)


def _fox_attn_kernel(q_ref, k_ref, v_ref, fcol_ref, frow_ref, o_ref, st_ref, *, tq):
    qi = pl.program_id(1)
    qs = [q_ref[0, :, h * HEAD_DIM:(h + 1) * HEAD_DIM] for h in range(GROUP_HEADS)]
    fqs = [fcol_ref[0, :, FORGET_LANE + h:FORGET_LANE + h + 1] for h in range(GROUP_HEADS)]
    res = _flash(qs, k_ref, v_ref, st_ref, HEAD_DIM, HEAD_DIM, qi, tq, fqs, frow_ref)
    for h, (acc, l) in enumerate(res):
        o_ref[0, :, h * HEAD_DIM:(h + 1) * HEAD_DIM] = (acc / l).astype(o_ref.dtype)


def _fox_attn(qkv, fcol, frow, tq=512):
    b, s, _ = qkv.shape
    w = GROUP_W
    return pl.pallas_call(
        functools.partial(_fox_attn_kernel, tq=tq),
        grid=(b, s // tq),
        in_specs=[pl.BlockSpec((1, tq, w), lambda i, j: (i, j, 0)),
                  pl.BlockSpec((1, s, w), lambda i, j: (i, 0, 1)),
                  pl.BlockSpec((1, s, w), lambda i, j: (i, 0, 2)),
                  pl.BlockSpec((1, tq, LANES), lambda i, j: (i, j, 0)),
                  pl.BlockSpec((1, 8, s), lambda i, j: (i, 0, 0))],
        out_specs=pl.BlockSpec((1, tq, w), lambda i, j: (i, j, 0)),
        out_shape=jax.ShapeDtypeStruct((b, s, w), BF16),
        scratch_shapes=[_flash_state(tq)],
        compiler_params=_params("parallel", "arbitrary"),
        name="fox_attn",
    )(qkv, qkv, qkv, fcol, frow)


def _mla_attn_kernel(q_ref, k_ref, v_ref, o_ref, st_ref, *, tq):
    qi = pl.program_id(1)
    qs = [q_ref[0, :, h * MLA_QK_PAD:(h + 1) * MLA_QK_PAD] for h in range(GROUP_HEADS)]
    res = _flash(qs, k_ref, v_ref, st_ref, MLA_QK_PAD, MLA_V, qi, tq)
    for h, (acc, l) in enumerate(res):
        o_ref[0, :, h * MLA_V:(h + 1) * MLA_V] = (acc / l).astype(o_ref.dtype)


def _mla_attn(qm, km, vm, tq=512):
    b, s, hq = qm.shape
    return pl.pallas_call(
        functools.partial(_mla_attn_kernel, tq=tq),
        grid=(b, s // tq),
        in_specs=[pl.BlockSpec((1, tq, hq), lambda i, j: (i, j, 0)),
                  pl.BlockSpec((1, s, hq), lambda i, j: (i, 0, 0)),
                  pl.BlockSpec((1, s, GROUP_W), lambda i, j: (i, 0, 0))],
        out_specs=pl.BlockSpec((1, tq, GROUP_W), lambda i, j: (i, j, 0)),
        out_shape=jax.ShapeDtypeStruct((b, s, GROUP_W), BF16),
        scratch_shapes=[_flash_state(tq)],
        compiler_params=_params("parallel", "arbitrary"),
        name="mla_attn",
    )(qm, km, vm)


def _diff_attn_kernel(q_ref, k_ref, v_ref, lam_ref, g_ref, o_ref, st_ref, *, tq, lam_init):
    qi = pl.program_id(1)
    lv = lam_ref[...]
    lam = (jnp.exp(jnp.sum(lv[0:1] * lv[1:2], axis=-1, keepdims=True))
           - jnp.exp(jnp.sum(lv[2:3] * lv[3:4], axis=-1, keepdims=True)) + lam_init)
    lane = lax.broadcasted_iota(jnp.int32, (tq, HEAD_DIM), 1)
    qs = []
    for h in range(GROUP_HEADS):
        q = q_ref[0, :, h * HEAD_DIM:(h + 1) * HEAD_DIM]
        zero = jnp.zeros_like(q)
        qs.append(jnp.concatenate([jnp.where(lane < DIFF_D, q, zero),
                                   jnp.where(lane >= DIFF_D, q, zero)], axis=0))
    res = _flash(qs, k_ref, v_ref, st_ref, HEAD_DIM, HEAD_DIM, qi, tq)
    for h, (acc, l) in enumerate(res):
        o = acc / l
        o = o[0:tq] - lam * o[tq:2 * tq]
        o = _rms(o, g_ref[...]) * (1.0 - lam_init)
        o_ref[0, :, h * HEAD_DIM:(h + 1) * HEAD_DIM] = o.astype(o_ref.dtype)


def _diff_attn(qr, kr, v, lamv, g_diff, lam_init, tq=512):
    b, s, w = qr.shape
    return pl.pallas_call(
        functools.partial(_diff_attn_kernel, tq=tq, lam_init=lam_init),
        grid=(b, s // tq),
        in_specs=[pl.BlockSpec((1, tq, w), lambda i, j: (i, j, 0)),
                  pl.BlockSpec((1, s, w), lambda i, j: (i, 0, 0)),
                  pl.BlockSpec((1, s, w), lambda i, j: (i, 0, 0)),
                  pl.BlockSpec((4, DIFF_D), lambda i, j: (0, 0)),
                  pl.BlockSpec((1, HEAD_DIM), lambda i, j: (0, 0))],
        out_specs=pl.BlockSpec((1, tq, w), lambda i, j: (i, j, 0)),
        out_shape=jax.ShapeDtypeStruct((b, s, w), BF16),
        scratch_shapes=[_flash_state(2 * tq)],
        compiler_params=_params("parallel", "arbitrary"),
        name="diff_attn",
    )(qr, kr, v, lamv, g_diff)


def _mix_cross_kernel(x_ref, ya_ref, yb_ref, yc_ref, yd_ref, wo_ref, g_ref, wq_ref, kv_ref,
                      wco_ref, o_ref, oh_ref, *, scale):
    acc = x_ref[...]
    for m, y_ref in enumerate((ya_ref, yb_ref, yc_ref, yd_ref)):
        acc = acc + _dot(y_ref[...], wo_ref[m * GROUP_W:(m + 1) * GROUP_W, :])
    o_ref[...] = acc
    q = _dot(_rms(o_ref[...], g_ref[...]).astype(BF16), wq_ref[...]).astype(BF16)
    for h in range(CROSS_HEADS):
        cols = slice(h * CROSS_HEAD_DIM, (h + 1) * CROSS_HEAD_DIM)
        k = kv_ref[0, :, cols]
        v = kv_ref[0, :, CROSS_W + h * CROSS_HEAD_DIM:CROSS_W + (h + 1) * CROSS_HEAD_DIM]
        s = _dot_nt(q[:, cols], k) * scale
        p = jnp.exp(s - jnp.max(s, axis=-1, keepdims=True))
        l = jnp.sum(p, axis=-1, keepdims=True)
        oh_ref[:, cols] = (_dot(p.astype(BF16), v) / l).astype(BF16)
    o_ref[...] += _dot(oh_ref[...], wco_ref[...])


def _mix_cross(x, ys, w_o, g, wq, kv, wco, seq, l, tm=512):
    t, d = x.shape
    per_b = seq // tm
    m = kv.shape[1]
    once = pl.Buffered(1)
    yspec = pl.BlockSpec((tm, GROUP_W), lambda i: (i, 0))
    return pl.pallas_call(
        functools.partial(_mix_cross_kernel, scale=CROSS_HEAD_DIM ** -0.5),
        grid=(t // tm,),
        in_specs=[pl.BlockSpec((tm, d), lambda i: (i, 0)), yspec, yspec, yspec, yspec,
                  pl.BlockSpec((None, w_o.shape[1], d), lambda i: (l, 0, 0), pipeline_mode=once),
                  pl.BlockSpec((1, d), lambda i: (0, 0)),
                  pl.BlockSpec((None, d, CROSS_W), lambda i: (l, 0, 0), pipeline_mode=once),
                  pl.BlockSpec((1, m, 2 * CROSS_W), lambda i: (i // per_b, 0, 0)),
                  pl.BlockSpec((None, CROSS_W, d), lambda i: (l, 0, 0), pipeline_mode=once)],
        out_specs=pl.BlockSpec((tm, d), lambda i: (i, 0)),
        out_shape=jax.ShapeDtypeStruct((t, d), F32),
        scratch_shapes=[pltpu.VMEM((tm, CROSS_W), BF16)],
        compiler_params=_params("parallel"),
        name="mix_cross",
    )(x, *ys, w_o, g, wq, kv, wco)


def _ffn_kernel(x_ref, g_ref, wg_ref, wu_ref, wd_ref, gf_ref, o_ref, xn_ref, *, final):
    j = pl.program_id(1)

    @pl.when(j == 0)
    def _():
        x = x_ref[...]
        xn_ref[...] = _rms(x, g_ref[...]).astype(BF16)
        o_ref[...] = x

    xn = xn_ref[...]
    half = wg_ref.shape[1] // 2
    hids = []
    for c in range(2):
        cols = slice(c * half, (c + 1) * half)
        gate = _dot(xn, wg_ref[:, cols])
        up = _dot(xn, wu_ref[:, cols])
        hids.append((gate * jax.nn.sigmoid(gate) * up).astype(BF16))
    o_ref[...] += _dot(hids[0], wd_ref[0:half, :]) + _dot(hids[1], wd_ref[half:2 * half, :])

    if final:
        @pl.when(j == pl.num_programs(1) - 1)
        def _():
            o_ref[...] = _rms(o_ref[...], gf_ref[...])


def _ffn(x, g, wg, wu, wd, g_final, final, l, tm=1024, th=512):
    t, d = x.shape
    hdim = wg.shape[2]
    windows = 2 * 2 * tm * d * 4 + 2 * 3 * d * th * 2 + tm * d * 2
    vmem = windows + 8 * 2**20
    return pl.pallas_call(
        functools.partial(_ffn_kernel, final=final),
        grid=(t // tm, hdim // th),
        in_specs=[pl.BlockSpec((tm, d), lambda i, j: (i, 0)),
                  pl.BlockSpec((1, d), lambda i, j: (0, 0)),
                  pl.BlockSpec((None, d, th), lambda i, j: (l, 0, j)),
                  pl.BlockSpec((None, d, th), lambda i, j: (l, 0, j)),
                  pl.BlockSpec((None, th, d), lambda i, j: (l, j, 0)),
                  pl.BlockSpec((1, d), lambda i, j: (0, 0))],
        out_specs=pl.BlockSpec((tm, d), lambda i, j: (i, 0)),
        out_shape=jax.ShapeDtypeStruct((t, d), F32),
        scratch_shapes=[pltpu.VMEM((tm, d), BF16)],
        compiler_params=_params("parallel", "arbitrary", vmem_limit=vmem),
        name="swiglu",
    )(x, g, wg, wu, wd, g_final)


def _rope_tables(positions):
    pos = positions.astype(F32)[..., None]
    b, s = positions.shape

    def table(rot_dim, period):
        freqs = ROPE_THETA ** (-jnp.arange(0, rot_dim, 2, dtype=F32) / rot_dim)
        ang = pos * freqs
        ones = jnp.ones((b, s, period - rot_dim), F32)
        cos = jnp.concatenate([jnp.cos(ang), jnp.cos(ang), ones], axis=-1)
        sin = jnp.concatenate([jnp.sin(ang), jnp.sin(ang), 0.0 * ones], axis=-1)
        reps = LANES // period
        return (jnp.tile(cos, (1, 1, reps)).reshape(b * s, LANES),
                jnp.tile(sin, (1, 1, reps)).reshape(b * s, LANES))

    return table(DIFF_ROT, DIFF_D) + table(MLA_ROPE, LANES)


def _pad_heads(w, used, width):
    nl, k, _ = w.shape
    w = w.reshape(nl, k, GROUP_HEADS, used)
    w = jnp.pad(w, ((0, 0), (0, 0), (0, 0), (0, width - used)))
    return w.reshape(nl, k, GROUP_HEADS * width)


def kernel(x, mem, positions, g_mix, w_in, b_f, g_cq, g_ckv, w_uq, w_ukv, sgu_ln_g, sgu_ln_b, w_s,
           b_s, lam_q1, lam_k1, lam_q2, lam_k2, g_diff, w_o, g_mem, g_cross, w_cq, w_ck, w_cv, w_co,
           g_ffn, w_gate, w_up, w_down, g_final):
    b, s, d = x.shape
    depth = w_in.shape[0]
    t = b * s
    tables = _rope_tables(positions)
    xf = x.reshape(t, d)
    memf = mem.reshape(b * mem.shape[1], d)
    row = lambda a: a.reshape(1, -1)
    seq = lambda a: a.reshape(b, s, a.shape[-1])

    w_main = _split_w_in(w_in)
    w_uq_p = _pad_heads(w_uq, MLA_NOPE + MLA_ROPE, MLA_QK_PAD).astype(BF16)
    w_ukv_b = w_ukv.astype(BF16)
    w_o_b, w_cq_b, w_ck_b, w_cv_b, w_co_b, w_gate_b, w_up_b, w_down_b = (
        _cast_bf16(w) for w in (w_o, w_cq, w_ck, w_cv, w_co, w_gate, w_up, w_down))
    b_st = jnp.swapaxes(b_s, 1, 2)

    for l in range(depth):
        lam_init = 0.8 - 0.6 * math.exp(-0.3 * l)
        fox, fb, dq, dk, dv, y_c, qm, km, vm = _front(
            xf, row(g_mix[l]), w_main, tables, row(sgu_ln_g[l]), row(sgu_ln_b[l]), w_s, b_st,
            row(g_cq[l]), row(g_ckv[l]), w_uq_p, w_ukv_b, l)

        bf_pad = jnp.pad(b_f[l], (FORGET_LANE, LANES - FORGET_LANE - GROUP_HEADS)).reshape(1, LANES)
        fcol, frow = _fox_gate(seq(fb), bf_pad)
        y_a = _fox_attn(seq(fox), fcol, frow)
        y_b = _mla_attn(seq(qm), seq(km), seq(vm))
        lamv = jnp.stack([lam_q1[l], lam_k1[l], lam_q2[l], lam_k2[l]])
        y_d = _diff_attn(seq(dq), seq(dk), seq(dv), lamv, row(g_diff[l]), lam_init)

        ys = [y_a.reshape(t, GROUP_W), y_b.reshape(t, GROUP_W), y_c, y_d.reshape(t, GROUP_W)]
        kv = _mem_kv(memf, row(g_mem), w_ck_b, w_cv_b, l).reshape(b, mem.shape[1], 2 * CROSS_W)
        xf = _mix_cross(xf, ys, w_o_b, row(g_cross[l]), w_cq_b, kv, w_co_b, s, l)

        xf = _ffn(xf, row(g_ffn[l]), w_gate_b, w_up_b, w_down_b, row(g_final),
                  final=(l == depth - 1), l=l)
    return xf.reshape(b, s, d)
```

```python
import functools
import math

import jax
import jax.numpy as jnp
from jax import lax
from jax.experimental import pallas as pl
from jax.experimental.pallas import tpu as pltpu

F32 = jnp.float32
BF16 = jnp.bfloat16

HEAD_DIM = 128
GROUP_HEADS = 4
GROUP_W = GROUP_HEADS * HEAD_DIM
MLA_Q_RANK = 384
MLA_KV_RANK = 256
MLA_NOPE = 128
MLA_ROPE = 64
MLA_V = 128
MLA_QK_PAD = 256
SGU_CHUNK = 128
DIFF_D = HEAD_DIM // 2
DIFF_ROT = DIFF_D // 4
CROSS_HEADS = 4
CROSS_HEAD_DIM = 128
CROSS_W = CROSS_HEADS * CROSS_HEAD_DIM
ROPE_THETA = 500000.0
EPS = 1e-6
LOG2E = math.log2(math.e)
LANES = 128

P_FOX = 0
P_DIFF = 3 * GROUP_W
P_SGU = 6 * GROUP_W
P_MLA = 8 * GROUP_W
MLA_SRC_W = MLA_Q_RANK + MLA_KV_RANK + MLA_ROPE
MLA_BLOCK_W = MLA_SRC_W + LANES - MLA_ROPE
P_W = P_MLA + MLA_BLOCK_W
FORGET_LANE = MLA_ROPE

VMEM_LIMIT = 52 * 2**20
VMEM_LIMIT_BIG = 56 * 2**20


def _params(*sem, vmem_limit=VMEM_LIMIT):
    return pltpu.CompilerParams(dimension_semantics=sem, vmem_limit_bytes=vmem_limit)


def _rms(x, g):
    return x * lax.rsqrt(jnp.mean(x * x, axis=-1, keepdims=True) + EPS) * g


def _dot(a, b):
    return jnp.dot(a, b, preferred_element_type=F32)


def _dot_nt(a, b):
    return lax.dot_general(a, b, (((1,), (1,)), ((), ())), preferred_element_type=F32)


def _cast_kernel(w_ref, o_ref):
    o_ref[...] = w_ref[...].astype(o_ref.dtype)


def _cast_bf16(w, block_bytes=4 * 2**20):
    nl, k, n = w.shape
    tk = k
    while tk * n * 4 > block_bytes and tk % 16 == 0:
        tk //= 2
    return pl.pallas_call(
        _cast_kernel,
        grid=(nl, k // tk),
        in_specs=[pl.BlockSpec((1, tk, n), lambda l, i: (l, i, 0))],
        out_specs=pl.BlockSpec((1, tk, n), lambda l, i: (l, i, 0)),
        out_shape=jax.ShapeDtypeStruct(w.shape, BF16),
        compiler_params=_params("parallel", "parallel"),
        name="cast_bf16",
    )(w)


_SRC_FORGET = 3 * GROUP_W
_SRC_MLA = _SRC_FORGET + GROUP_HEADS
_SRC_SGU = _SRC_MLA + MLA_SRC_W
_SRC_DIFF = _SRC_SGU + 2 * GROUP_W


def _split_w_in_kernel(w_ref, main_ref):
    latent_w = MLA_Q_RANK + MLA_KV_RANK
    copies = ((P_FOX, 0, 3 * GROUP_W), (P_DIFF, _SRC_DIFF, 3 * GROUP_W),
              (P_SGU, _SRC_SGU, 2 * GROUP_W), (P_MLA, _SRC_MLA, latent_w))
    for dst, src, width in copies:
        main_ref[0, :, dst:dst + width] = w_ref[0, :, src:src + width].astype(BF16)
    key = w_ref[0, :, _SRC_MLA + latent_w:_SRC_MLA + latent_w + LANES]
    forget = w_ref[0, :, _SRC_FORGET - FORGET_LANE:_SRC_FORGET - FORGET_LANE + LANES]
    lane = lax.broadcasted_iota(jnp.int32, key.shape, 1)
    group = jnp.where(lane < MLA_ROPE, key,
                      jnp.where(lane < FORGET_LANE + GROUP_HEADS, forget, 0.0))
    main_ref[0, :, P_MLA + latent_w:P_W] = group.astype(BF16)


def _split_w_in(w, tk=256):
    nl, k, n = w.shape
    return pl.pallas_call(
        _split_w_in_kernel,
        grid=(nl, k // tk),
        in_specs=[pl.BlockSpec((1, tk, n), lambda l, i: (l, i, 0))],
        out_specs=pl.BlockSpec((1, tk, P_W), lambda l, i: (l, i, 0)),
        out_shape=jax.ShapeDtypeStruct((nl, k, P_W), BF16),
        compiler_params=_params("parallel", "parallel"),
        name="split_w_in",
    )(w)


def _rope_pairs(x, cos, sin, half, period):
    lane = lax.broadcasted_iota(jnp.int32, x.shape, x.ndim - 1)
    first = (lane & (period - 1)) < half
    rot = jnp.where(first, -pltpu.roll(x, LANES - half, x.ndim - 1), pltpu.roll(x, half, x.ndim - 1))
    return x * cos + rot * sin


def _sgu_mix(uv, lg_ref, lb_ref, ws_ref, bs_ref, o_ref):
    z = 0.5 * uv * (1.0 + jnp.tanh(math.sqrt(2.0 / math.pi) * (uv + 0.044715 * (uv * uv * uv))))
    u = z[:, 0:GROUP_W]
    v = z[:, GROUP_W:2 * GROUP_W]
    xc = v - jnp.mean(v, axis=-1, keepdims=True)
    vn = xc * lax.rsqrt(jnp.mean(xc * xc, axis=-1, keepdims=True) + EPS) * lg_ref[...] + lb_ref[...]
    vn = vn.astype(BF16)
    t = SGU_CHUNK
    tril = (lax.broadcasted_iota(jnp.int32, (t, t), 1) <= lax.broadcasted_iota(jnp.int32, (t, t), 0))
    for g in range(GROUP_HEADS):
        w = jnp.where(tril, ws_ref[g], 0.0).astype(BF16)
        bias = bs_ref[:, g:g + 1]
        for c in range(uv.shape[0] // t):
            rows = slice(c * t, (c + 1) * t)
            cols = slice(g * HEAD_DIM, (g + 1) * HEAD_DIM)
            mixed = _dot(w, vn[rows, cols]) + bias
            o_ref[rows, cols] = (u[rows, cols] * mixed).astype(o_ref.dtype)


def _mla_expand(lat, gq_ref, gkv_ref, wq_ref, wkv_ref, cos, sin, q_ref, k_ref, v_ref, qscale):
    cq = lat[:, 0:MLA_Q_RANK]
    ckv = lat[:, MLA_Q_RANK:MLA_Q_RANK + MLA_KV_RANK]
    kr = lat[:, MLA_Q_RANK + MLA_KV_RANK:MLA_BLOCK_W]
    lane = lax.broadcasted_iota(jnp.int32, kr.shape, 1)
    kr = jnp.where(lane < MLA_ROPE, kr, 0.0)
    q = _dot(_rms(cq, gq_ref[...]).astype(BF16), wq_ref[...])
    kv = _dot(_rms(ckv, gkv_ref[...]).astype(BF16), wkv_ref[...])
    kr = _rope_pairs(kr, cos, sin, MLA_ROPE // 2, LANES).astype(BF16)
    for h in range(GROUP_HEADS):
        o = h * MLA_QK_PAD
        q_ref[:, o:o + LANES] = (q[:, o:o + LANES] * qscale).astype(BF16)
        q_ref[:, o + LANES:o + 2 * LANES] = (_rope_pairs(
            q[:, o + LANES:o + 2 * LANES], cos, sin, MLA_ROPE // 2, LANES) * qscale).astype(BF16)
        k_ref[:, o:o + LANES] = kv[:, o:o + LANES].astype(BF16)
        k_ref[:, o + LANES:o + 2 * LANES] = kr
        v_ref[:, h * MLA_V:(h + 1) * MLA_V] = kv[:, o + LANES:o + 2 * LANES].astype(BF16)


def _front_kernel(x_ref, g_ref, w_ref, cosd_ref, sind_ref, cosm_ref, sinm_ref,
                  lg_ref, lb_ref, ws_ref, bs_ref, gq_ref, gkv_ref, wuq_ref, wukv_ref,
                  fox_ref, fb_ref, dq_ref, dk_ref, dv_ref, yc_ref, qm_ref, km_ref, vm_ref,
                  *, fox_scale, diff_scale, mla_scale):
    xb = _rms(x_ref[...], g_ref[...]).astype(BF16)

    lat = _dot(xb, w_ref[:, P_MLA:P_MLA + MLA_BLOCK_W])
    fb_ref[...] = lat[:, MLA_BLOCK_W - LANES:MLA_BLOCK_W]
    _mla_expand(lat, gq_ref, gkv_ref, wuq_ref, wukv_ref, cosm_ref[...], sinm_ref[...],
                qm_ref, km_ref, vm_ref, mla_scale)

    uv = _dot(xb, w_ref[:, P_SGU:P_SGU + 2 * GROUP_W])
    _sgu_mix(uv, lg_ref, lb_ref, ws_ref, bs_ref, yc_ref)

    d = _dot(xb, w_ref[:, P_DIFF:P_DIFF + 3 * GROUP_W])
    cos = cosd_ref[...]
    sin = sind_ref[...]
    for c in range(GROUP_W // LANES):
        lanes = slice(c * LANES, (c + 1) * LANES)
        klanes = slice(GROUP_W + c * LANES, GROUP_W + (c + 1) * LANES)
        dq_ref[:, lanes] = (_rope_pairs(d[:, lanes], cos, sin, DIFF_ROT // 2, DIFF_D)
                            * diff_scale).astype(BF16)
        dk_ref[:, lanes] = _rope_pairs(d[:, klanes], cos, sin, DIFF_ROT // 2, DIFF_D).astype(BF16)
    dv_ref[...] = d[:, 2 * GROUP_W:3 * GROUP_W].astype(BF16)

    fox = _dot(xb, w_ref[:, P_FOX:P_FOX + 3 * GROUP_W])
    fox_ref[:, 0:GROUP_W] = (fox[:, 0:GROUP_W] * fox_scale).astype(BF16)
    fox_ref[:, GROUP_W:3 * GROUP_W] = fox[:, GROUP_W:3 * GROUP_W].astype(BF16)


def _front(x, g, w_main, tables, ln_g, ln_b, w_s, b_st, g_cq, g_ckv, w_uq, w_ukv, l, tm=512):
    t, d = x.shape
    hq = GROUP_HEADS * MLA_QK_PAD
    once = pl.Buffered(1)
    rowblk = lambda width: pl.BlockSpec((tm, width), lambda i: (i, 0))
    vec = lambda width: pl.BlockSpec((1, width), lambda i: (0, 0))
    widths = (3 * GROUP_W, LANES, GROUP_W, GROUP_W, GROUP_W, GROUP_W, hq, hq, GROUP_W)
    dtypes = (BF16, F32, BF16, BF16, BF16, BF16, BF16, BF16, BF16)
    return pl.pallas_call(
        functools.partial(_front_kernel, fox_scale=HEAD_DIM ** -0.5 * LOG2E,
                          diff_scale=DIFF_D ** -0.5 * LOG2E,
                          mla_scale=(MLA_NOPE + MLA_ROPE) ** -0.5 * LOG2E),
        grid=(t // tm,),
        in_specs=[rowblk(d), vec(d),
                  pl.BlockSpec((None, d, P_W), lambda i: (l, 0, 0), pipeline_mode=once),
                  rowblk(LANES), rowblk(LANES), rowblk(LANES), rowblk(LANES),
                  vec(GROUP_W), vec(GROUP_W),
                  pl.BlockSpec((None, GROUP_HEADS, SGU_CHUNK, SGU_CHUNK), lambda i: (l, 0, 0, 0)),
                  pl.BlockSpec((None, SGU_CHUNK, GROUP_HEADS), lambda i: (l, 0, 0)),
                  vec(MLA_Q_RANK), vec(MLA_KV_RANK),
                  pl.BlockSpec((None, MLA_Q_RANK, hq), lambda i: (l, 0, 0)),
                  pl.BlockSpec((None, MLA_KV_RANK, hq), lambda i: (l, 0, 0))],
        out_specs=[rowblk(w) for w in widths],
        out_shape=[jax.ShapeDtypeStruct((t, w), dt) for w, dt in zip(widths, dtypes)],
        compiler_params=_params("parallel", vmem_limit=VMEM_LIMIT_BIG),
        name="front",
    )(x, g, w_main, *tables, ln_g, ln_b, w_s, b_st, g_cq, g_ckv, w_uq, w_ukv)


def _mem_kv_kernel(x_ref, g_ref, wk_ref, wv_ref, o_ref):
    xb = _rms(x_ref[...], g_ref[...]).astype(BF16)
    o_ref[:, 0:CROSS_W] = _dot(xb, wk_ref[...]).astype(o_ref.dtype)
    o_ref[:, CROSS_W:2 * CROSS_W] = _dot(xb, wv_ref[...]).astype(o_ref.dtype)


def _mem_kv(x, g, wk, wv, l, tm=512):
    t, d = x.shape
    wspec = pl.BlockSpec((None, d, CROSS_W), lambda i: (l, 0, 0))
    return pl.pallas_call(
        _mem_kv_kernel,
        grid=(t // tm,),
        in_specs=[pl.BlockSpec((tm, d), lambda i: (i, 0)),
                  pl.BlockSpec((1, d), lambda i: (0, 0)), wspec, wspec],
        out_specs=pl.BlockSpec((tm, 2 * CROSS_W), lambda i: (i, 0)),
        out_shape=jax.ShapeDtypeStruct((t, 2 * CROSS_W), BF16),
        compiler_params=_params("parallel"),
        name="mem_kv_proj",
    )(x, g, wk, wv)


def _fox_gate_kernel(f_ref, b_ref, col_ref, row_ref):
    x = f_ref[0] + b_ref[...]
    c = jnp.minimum(x, 0.0) - jnp.log1p(jnp.exp(-jnp.abs(x)))
    s = x.shape[0]
    ridx = lax.broadcasted_iota(jnp.int32, x.shape, 0)
    shift = 1
    while shift < s:
        c = c + jnp.where(ridx >= shift, pltpu.roll(c, shift, 0), 0.0)
        shift *= 2
    c = c * LOG2E
    col_ref[0] = c
    row_ref[0] = c.T[FORGET_LANE:FORGET_LANE + 8, :]


def _fox_gate(fb, b_f):
    b, s, _ = fb.shape
    return pl.pallas_call(
        _fox_gate_kernel,
        grid=(b,),
        in_specs=[pl.BlockSpec((1, s, LANES), lambda i: (i, 0, 0)),
                  pl.BlockSpec((1, LANES), lambda i: (0, 0))],
        out_specs=[pl.BlockSpec((1, s, LANES), lambda i: (i, 0, 0)),
                   pl.BlockSpec((1, 8, s), lambda i: (i, 0, 0))],
        out_shape=[jax.ShapeDtypeStruct((b, s, LANES), F32),
                   jax.ShapeDtypeStruct((b, 8, s), F32)],
        compiler_params=_params("parallel"),
        name="fox_gate",
    )(fb, b_f)


def _flash(qs, k_ref, v_ref, st_ref, dk, dv, qi, tq, fqs=None, frow_ref=None):
    nh = len(qs)
    r = qs[0].shape[0]
    assert dv == LANES and tq % (2 * LANES) == 0
    if fqs is not None:
        fqs = [jnp.broadcast_to(f, (r, LANES)) for f in fqs]

    def scores(h, rows, start, nk, diag_row0=None):
        s = _dot_nt(qs[h][rows], k_ref[0, pl.ds(start, nk), h * dk:(h + 1) * dk])
        if frow_ref is not None:
            s = s - frow_ref[0, h:h + 1, pl.ds(start, nk)]
        if diag_row0 is not None:
            row = lax.broadcasted_iota(jnp.int32, s.shape, 0) + diag_row0
            col = lax.broadcasted_iota(jnp.int32, s.shape, 1)
            s = jnp.where(col <= row, s, -jnp.inf)
        return s

    def absorb(s, h, rows, start, old):
        nr, nk = s.shape
        mz = jnp.broadcast_to(jnp.max(s, axis=-1, keepdims=True), (nr, LANES))
        if fqs is not None:
            mz = mz + fqs[h][rows]
        if old is None:
            m_new = mz
        else:
            m_new = jnp.maximum(old[0], mz)
            alpha = jnp.exp2(old[0] - m_new)
        shift = (fqs[h][rows] - m_new) if fqs is not None else -m_new
        p = jnp.exp2(s + jnp.concatenate([shift] * (nk // LANES), axis=1))
        psum = p[:, 0:LANES]
        for g in range(1, nk // LANES):
            psum = psum + p[:, g * LANES:(g + 1) * LANES]
        pv = _dot(p.astype(BF16), v_ref[0, pl.ds(start, nk), h * dv:(h + 1) * dv])
        if old is None:
            return m_new, psum, pv
        return m_new, alpha * old[1] + psum, alpha * old[2] + pv

    half = tq // 2
    diag_start = pl.multiple_of(qi * tq, tq)
    pieces = []
    for rep in range(r // tq):
        pieces.append((slice(rep * tq, rep * tq + half), half, 0))
        pieces.append((slice(rep * tq + half, (rep + 1) * tq), tq, half))
    diag = [[scores(h, rows, diag_start, nk, row0) for rows, nk, row0 in pieces] for h in range(nh)]
    for h in range(nh):
        parts = [absorb(s, h, rows, diag_start, None) for s, (rows, _, _) in zip(diag[h], pieces)]
        for i in range(3):
            st_ref[i, h] = jnp.concatenate([part[i] for part in parts], axis=0)

    def body(j, c):
        start = pl.multiple_of(j * tq, tq)
        every = slice(0, r)
        blocks = [scores(h, every, start, tq) for h in range(nh)]
        for h in range(nh):
            new = absorb(blocks[h], h, every, start, (st_ref[0, h], st_ref[1, h], st_ref[2, h]))
            for i in range(3):
                st_ref[i, h] = new[i]
        return c

    lax.fori_loop(0, qi, body, 0)
    return [(st_ref[2, h], jnp.sum(st_ref[1, h], axis=-1, keepdims=True)) for h in range(nh)]


def _flash_state(rows):
    return pltpu.VMEM((3, GROUP_HEADS, rows, LANES), F32)


def _fox_attn_kernel(q_ref, k_ref, v_ref, fcol_ref, frow_ref, o_ref, st_ref, *, tq):
    qi = pl.program_id(1)
    qs = [q_ref[0, :, h * HEAD_DIM:(h + 1) * HEAD_DIM] for h in range(GROUP_HEADS)]
    fqs = [fcol_ref[0, :, FORGET_LANE + h:FORGET_LANE + h + 1] for h in range(GROUP_HEADS)]
    res = _flash(qs, k_ref, v_ref, st_ref, HEAD_DIM, HEAD_DIM, qi, tq, fqs, frow_ref)
    for h, (acc, l) in enumerate(res):
        o_ref[0, :, h * HEAD_DIM:(h + 1) * HEAD_DIM] = (acc / l).astype(o_ref.dtype)


def _fox_attn(qkv, fcol, frow, tq=512):
    b, s, _ = qkv.shape
    w = GROUP_W
    return pl.pallas_call(
        functools.partial(_fox_attn_kernel, tq=tq),
        grid=(b, s // tq),
        in_specs=[pl.BlockSpec((1, tq, w), lambda i, j: (i, j, 0)),
                  pl.BlockSpec((1, s, w), lambda i, j: (i, 0, 1)),
                  pl.BlockSpec((1, s, w), lambda i, j: (i, 0, 2)),
                  pl.BlockSpec((1, tq, LANES), lambda i, j: (i, j, 0)),
                  pl.BlockSpec((1, 8, s), lambda i, j: (i, 0, 0))],
        out_specs=pl.BlockSpec((1, tq, w), lambda i, j: (i, j, 0)),
        out_shape=jax.ShapeDtypeStruct((b, s, w), BF16),
        scratch_shapes=[_flash_state(tq)],
        compiler_params=_params("parallel", "arbitrary"),
        name="fox_attn",
    )(qkv, qkv, qkv, fcol, frow)


def _mla_attn_kernel(q_ref, k_ref, v_ref, o_ref, st_ref, *, tq):
    qi = pl.program_id(1)
    qs = [q_ref[0, :, h * MLA_QK_PAD:(h + 1) * MLA_QK_PAD] for h in range(GROUP_HEADS)]
    res = _flash(qs, k_ref, v_ref, st_ref, MLA_QK_PAD, MLA_V, qi, tq)
    for h, (acc, l) in enumerate(res):
        o_ref[0, :, h * MLA_V:(h + 1) * MLA_V] = (acc / l).astype(o_ref.dtype)


def _mla_attn(qm, km, vm, tq=512):
    b, s, hq = qm.shape
    return pl.pallas_call(
        functools.partial(_mla_attn_kernel, tq=tq),
        grid=(b, s // tq),
        in_specs=[pl.BlockSpec((1, tq, hq), lambda i, j: (i, j, 0)),
                  pl.BlockSpec((1, s, hq), lambda i, j: (i, 0, 0)),
                  pl.BlockSpec((1, s, GROUP_W), lambda i, j: (i, 0, 0))],
        out_specs=pl.BlockSpec((1, tq, GROUP_W), lambda i, j: (i, j, 0)),
        out_shape=jax.ShapeDtypeStruct((b, s, GROUP_W), BF16),
        scratch_shapes=[_flash_state(tq)],
        compiler_params=_params("parallel", "arbitrary"),
        name="mla_attn",
    )(qm, km, vm)


def _diff_attn_kernel(q_ref, k_ref, v_ref, lam_ref, g_ref, o_ref, st_ref, *, tq, lam_init):
    qi = pl.program_id(1)
    lv = lam_ref[...]
    lam = (jnp.exp(jnp.sum(lv[0:1] * lv[1:2], axis=-1, keepdims=True))
           - jnp.exp(jnp.sum(lv[2:3] * lv[3:4], axis=-1, keepdims=True)) + lam_init)
    lane = lax.broadcasted_iota(jnp.int32, (tq, HEAD_DIM), 1)
    qs = []
    for h in range(GROUP_HEADS):
        q = q_ref[0, :, h * HEAD_DIM:(h + 1) * HEAD_DIM]
        zero = jnp.zeros_like(q)
        qs.append(jnp.concatenate([jnp.where(lane < DIFF_D, q, zero),
                                   jnp.where(lane >= DIFF_D, q, zero)], axis=0))
    res = _flash(qs, k_ref, v_ref, st_ref, HEAD_DIM, HEAD_DIM, qi, tq)
    for h, (acc, l) in enumerate(res):
        o = acc / l
        o = o[0:tq] - lam * o[tq:2 * tq]
        o = _rms(o, g_ref[...]) * (1.0 - lam_init)
        o_ref[0, :, h * HEAD_DIM:(h + 1) * HEAD_DIM] = o.astype(o_ref.dtype)


def _diff_attn(qr, kr, v, lamv, g_diff, lam_init, tq=512):
    b, s, w = qr.shape
    return pl.pallas_call(
        functools.partial(_diff_attn_kernel, tq=tq, lam_init=lam_init),
        grid=(b, s // tq),
        in_specs=[pl.BlockSpec((1, tq, w), lambda i, j: (i, j, 0)),
                  pl.BlockSpec((1, s, w), lambda i, j: (i, 0, 0)),
                  pl.BlockSpec((1, s, w), lambda i, j: (i, 0, 0)),
                  pl.BlockSpec((4, DIFF_D), lambda i, j: (0, 0)),
                  pl.BlockSpec((1, HEAD_DIM), lambda i, j: (0, 0))],
        out_specs=pl.BlockSpec((1, tq, w), lambda i, j: (i, j, 0)),
        out_shape=jax.ShapeDtypeStruct((b, s, w), BF16),
        scratch_shapes=[_flash_state(2 * tq)],
        compiler_params=_params("parallel", "arbitrary"),
        name="diff_attn",
    )(qr, kr, v, lamv, g_diff)


def _mix_cross_kernel(x_ref, ya_ref, yb_ref, yc_ref, yd_ref, wo_ref, g_ref, wq_ref, kv_ref,
                      wco_ref, o_ref, oh_ref, *, scale):
    acc = x_ref[...]
    for m, y_ref in enumerate((ya_ref, yb_ref, yc_ref, yd_ref)):
        acc = acc + _dot(y_ref[...], wo_ref[m * GROUP_W:(m + 1) * GROUP_W, :])
    o_ref[...] = acc
    q = _dot(_rms(o_ref[...], g_ref[...]).astype(BF16), wq_ref[...]).astype(BF16)
    for h in range(CROSS_HEADS):
        cols = slice(h * CROSS_HEAD_DIM, (h + 1) * CROSS_HEAD_DIM)
        k = kv_ref[0, :, cols]
        v = kv_ref[0, :, CROSS_W + h * CROSS_HEAD_DIM:CROSS_W + (h + 1) * CROSS_HEAD_DIM]
        s = _dot_nt(q[:, cols], k) * scale
        p = jnp.exp(s - jnp.max(s, axis=-1, keepdims=True))
        l = jnp.sum(p, axis=-1, keepdims=True)
        oh_ref[:, cols] = (_dot(p.astype(BF16), v) / l).astype(BF16)
    o_ref[...] += _dot(oh_ref[...], wco_ref[...])


def _mix_cross(x, ys, w_o, g, wq, kv, wco, seq, l, tm=512):
    t, d = x.shape
    per_b = seq // tm
    m = kv.shape[1]
    once = pl.Buffered(1)
    yspec = pl.BlockSpec((tm, GROUP_W), lambda i: (i, 0))
    return pl.pallas_call(
        functools.partial(_mix_cross_kernel, scale=CROSS_HEAD_DIM ** -0.5),
        grid=(t // tm,),
        in_specs=[pl.BlockSpec((tm, d), lambda i: (i, 0)), yspec, yspec, yspec, yspec,
                  pl.BlockSpec((None, w_o.shape[1], d), lambda i: (l, 0, 0), pipeline_mode=once),
                  pl.BlockSpec((1, d), lambda i: (0, 0)),
                  pl.BlockSpec((None, d, CROSS_W), lambda i: (l, 0, 0), pipeline_mode=once),
                  pl.BlockSpec((1, m, 2 * CROSS_W), lambda i: (i // per_b, 0, 0)),
                  pl.BlockSpec((None, CROSS_W, d), lambda i: (l, 0, 0), pipeline_mode=once)],
        out_specs=pl.BlockSpec((tm, d), lambda i: (i, 0)),
        out_shape=jax.ShapeDtypeStruct((t, d), F32),
        scratch_shapes=[pltpu.VMEM((tm, CROSS_W), BF16)],
        compiler_params=_params("parallel"),
        name="mix_cross",
    )(x, *ys, w_o, g, wq, kv, wco)


def _ffn_kernel(x_ref, g_ref, wg_ref, wu_ref, wd_ref, gf_ref, o_ref, xn_ref, *, final):
    j = pl.program_id(1)

    @pl.when(j == 0)
    def _():
        x = x_ref[...]
        xn_ref[...] = _rms(x, g_ref[...]).astype(BF16)
        o_ref[...] = x

    xn = xn_ref[...]
    half = wg_ref.shape[1] // 2
    hids = []
    for c in range(2):
        cols = slice(c * half, (c + 1) * half)
        gate = _dot(xn, wg_ref[:, cols])
        up = _dot(xn, wu_ref[:, cols])
        hids.append((gate * jax.nn.sigmoid(gate) * up).astype(BF16))
    o_ref[...] += _dot(hids[0], wd_ref[0:half, :]) + _dot(hids[1], wd_ref[half:2 * half, :])

    if final:
        @pl.when(j == pl.num_programs(1) - 1)
        def _():
            o_ref[...] = _rms(o_ref[...], gf_ref[...])


def _ffn(x, g, wg, wu, wd, g_final, final, l, tm=1024, th=512):
    t, d = x.shape
    hdim = wg.shape[2]
    windows = 2 * 2 * tm * d * 4 + 2 * 3 * d * th * 2 + tm * d * 2
    assert windows < VMEM_LIMIT_BIG
    return pl.pallas_call(
        functools.partial(_ffn_kernel, final=final),
        grid=(t // tm, hdim // th),
        in_specs=[pl.BlockSpec((tm, d), lambda i, j: (i, 0)),
                  pl.BlockSpec((1, d), lambda i, j: (0, 0)),
                  pl.BlockSpec((None, d, th), lambda i, j: (l, 0, j)),
                  pl.BlockSpec((None, d, th), lambda i, j: (l, 0, j)),
                  pl.BlockSpec((None, th, d), lambda i, j: (l, j, 0)),
                  pl.BlockSpec((1, d), lambda i, j: (0, 0))],
        out_specs=pl.BlockSpec((tm, d), lambda i, j: (i, 0)),
        out_shape=jax.ShapeDtypeStruct((t, d), F32),
        scratch_shapes=[pltpu.VMEM((tm, d), BF16)],
        compiler_params=_params("parallel", "arbitrary", vmem_limit=VMEM_LIMIT_BIG),
        name="swiglu",
    )(x, g, wg, wu, wd, g_final)


def _rope_tables(positions):
    pos = positions.astype(F32)[..., None]
    b, s = positions.shape

    def table(rot_dim, period):
        freqs = ROPE_THETA ** (-jnp.arange(0, rot_dim, 2, dtype=F32) / rot_dim)
        ang = pos * freqs
        ones = jnp.ones((b, s, period - rot_dim), F32)
        cos = jnp.concatenate([jnp.cos(ang), jnp.cos(ang), ones], axis=-1)
        sin = jnp.concatenate([jnp.sin(ang), jnp.sin(ang), 0.0 * ones], axis=-1)
        reps = LANES // period
        return (jnp.tile(cos, (1, 1, reps)).reshape(b * s, LANES),
                jnp.tile(sin, (1, 1, reps)).reshape(b * s, LANES))

    return table(DIFF_ROT, DIFF_D) + table(MLA_ROPE, LANES)


def _pad_heads(w, used, width):
    nl, k, _ = w.shape
    w = w.reshape(nl, k, GROUP_HEADS, used)
    w = jnp.pad(w, ((0, 0), (0, 0), (0, 0), (0, width - used)))
    return w.reshape(nl, k, GROUP_HEADS * width)


def kernel(x, mem, positions, g_mix, w_in, b_f, g_cq, g_ckv, w_uq, w_ukv, sgu_ln_g, sgu_ln_b, w_s,
           b_s, lam_q1, lam_k1, lam_q2, lam_k2, g_diff, w_o, g_mem, g_cross, w_cq, w_ck, w_cv, w_co,
           g_ffn, w_gate, w_up, w_down, g_final):
    b, s, d = x.shape
    depth = w_in.shape[0]
    t = b * s
    tables = _rope_tables(positions)
    xf = x.reshape(t, d)
    memf = mem.reshape(b * mem.shape[1], d)
    row = lambda a: a.reshape(1, -1)
    seq = lambda a: a.reshape(b, s, a.shape[-1])

    w_main = _split_w_in(w_in)
    w_uq_p = _pad_heads(w_uq, MLA_NOPE + MLA_ROPE, MLA_QK_PAD).astype(BF16)
    w_ukv_b = w_ukv.astype(BF16)
    w_o_b, w_cq_b, w_ck_b, w_cv_b, w_co_b, w_gate_b, w_up_b, w_down_b = (
        _cast_bf16(w) for w in (w_o, w_cq, w_ck, w_cv, w_co, w_gate, w_up, w_down))
    b_st = jnp.swapaxes(b_s, 1, 2)

    for l in range(depth):
        lam_init = 0.8 - 0.6 * math.exp(-0.3 * l)
        fox, fb, dq, dk, dv, y_c, qm, km, vm = _front(
            xf, row(g_mix[l]), w_main, tables, row(sgu_ln_g[l]), row(sgu_ln_b[l]), w_s, b_st,
            row(g_cq[l]), row(g_ckv[l]), w_uq_p, w_ukv_b, l)

        bf_pad = jnp.pad(b_f[l], (FORGET_LANE, LANES - FORGET_LANE - GROUP_HEADS)).reshape(1, LANES)
        fcol, frow = _fox_gate(seq(fb), bf_pad)
        y_a = _fox_attn(seq(fox), fcol, frow)
        y_b = _mla_attn(seq(qm), seq(km), seq(vm))
        lamv = jnp.stack([lam_q1[l], lam_k1[l], lam_q2[l], lam_k2[l]])
        y_d = _diff_attn(seq(dq), seq(dk), seq(dv), lamv, row(g_diff[l]), lam_init)

        ys = [y_a.reshape(t, GROUP_W), y_b.reshape(t, GROUP_W), y_c, y_d.reshape(t, GROUP_W)]
        kv = _mem_kv(memf, row(g_mem), w_ck_b, w_cv_b, l).reshape(b, mem.shape[1], 2 * CROSS_W)
        xf = _mix_cross(xf, ys, w_o_b, row(g_cross[l]), w_cq_b, kv, w_co_b, s, l)

        xf = _ffn(xf, row(g_ffn[l]), w_gate_b, w_up_b, w_down_b, row(g_final),
                  final=(l == depth - 1), l=l)
    return xf.reshape(b, s, d)
```
